```python
import jax, jax.numpy as jnp
from jax import lax
import numpy as np

D_MODEL = 2048
BATCH = 4
SEQ = 2048
DEPTH = 1
DEC_BATCH = 128
DEC_SEQ = 1
PAST_LEN = 16384
PAGE_SIZE = 128

CHUNK = 128
A_WIDTH = D_MODEL
A_GROUPS = 8
A_GROUP_DIM = A_WIDTH // A_GROUPS
B_WIDTH = D_MODEL
CONV_W = 31
N_EXPERTS = 256
TOP_K = 8
N_EXPERT_GROUPS = 8
TOPK_GROUPS = 4
D_EXPERT = D_MODEL // 4
D_SHARED = D_MODEL // 4
ROUTED_SCALE = 2.5
MOE_BLOCK = 128
N_MOD = 6
EPS = 1e-6
IN_COLS = 2 * A_WIDTH + 2 * B_WIDTH + 2 * D_MODEL
SPLITS = (A_WIDTH, 2 * A_WIDTH, 2 * A_WIDTH + B_WIDTH, 2 * A_WIDTH + 2 * B_WIDTH,
          2 * A_WIDTH + 2 * B_WIDTH + D_MODEL)

kernel_name = "gated_chunkmlp_conformer_moe_decoder_step"


def rms_norm(x, g):
    x32 = x.astype(jnp.float32)
    y = x32 * lax.rsqrt(jnp.mean(x32 * x32, axis=-1, keepdims=True) + EPS)
    return y.astype(x.dtype) * g


def layer_norm(x, g, b):
    x32 = x.astype(jnp.float32)
    mu = jnp.mean(x32, axis=-1, keepdims=True)
    var = jnp.mean(jnp.square(x32 - mu), axis=-1, keepdims=True)
    return ((x32 - mu) * lax.rsqrt(var + EPS)).astype(x.dtype) * g + b


def ada_mod(c, w_ada, b_ada):
    m = jnp.dot(jax.nn.silu(c), w_ada) + b_ada
    return m.reshape(c.shape[0], N_MOD, 1, D_MODEL)


def spatial_gate(v, w_s, b_s):
    n = v.shape[1]
    mask = jnp.tril(jnp.ones((n, n), dtype=bool))
    w = jnp.where(mask[None], w_s[:, :n, :n], 0)
    vg = v.reshape(v.shape[0], n, A_GROUPS, A_GROUP_DIM)
    out = jnp.einsum('gts,nsgc->ntgc', w, vg) + b_s[:, :n].T[None, :, :, None]
    return out.reshape(v.shape)


def depthwise_causal_conv(xpad, conv_w, conv_b):
    y = lax.conv_general_dilated(xpad, conv_w[:, None, :], (1,), 'VALID',
                                 dimension_numbers=('NWC', 'WIO', 'NWC'),
                                 feature_group_count=B_WIDTH)
    return y + conv_b


def token_mixer(h, conv_past, chunk_rows, w_in, a_norm_g, a_norm_b, w_s, b_s, conv_w, conv_b,
                b_norm_g, b_norm_b, w_a_out, w_b_out, w_o):
    n, s, _ = h.shape
    z = jnp.dot(h, w_in)
    u, v, a, b, g_a, g_b = jnp.split(z, SPLITS, axis=-1)
    u = jax.nn.gelu(u)
    v = layer_norm(jax.nn.gelu(v), a_norm_g, a_norm_b)
    v_mix = spatial_gate(v.reshape(n * s // chunk_rows, chunk_rows, A_WIDTH), w_s, b_s)
    y_a = u * v_mix.reshape(n, s, A_WIDTH)
    glu = a * jax.nn.sigmoid(b)
    xpad = jnp.concatenate([conv_past, glu], axis=1)
    y_b = jax.nn.silu(layer_norm(depthwise_causal_conv(xpad, conv_w, conv_b), b_norm_g, b_norm_b))
    merged = (jax.nn.sigmoid(g_a) * jnp.dot(y_a, w_a_out)
              + jax.nn.sigmoid(g_b) * jnp.dot(y_b, w_b_out))
    out = jnp.dot(merged, w_o)
    return out, xpad[:, -(CONV_W - 1):], v[:, s - chunk_rows:]


def moe_ffn(h, w_router, router_bias, w_gate_e, w_up_e, w_down_e, w_gate_s, w_up_s, w_down_s):
    t_tokens = h.shape[0]
    scores = jax.nn.sigmoid(jnp.dot(h.astype(jnp.float32), w_router.astype(jnp.float32)))
    sel = scores + router_bias.astype(jnp.float32)
    grp = sel.reshape(t_tokens, N_EXPERT_GROUPS, N_EXPERTS // N_EXPERT_GROUPS)
    grp_score = lax.top_k(grp, 2)[0].sum(-1)
    _, top_grp = lax.top_k(grp_score, TOPK_GROUPS)
    grp_mask = jax.nn.one_hot(top_grp, N_EXPERT_GROUPS, dtype=jnp.float32).max(axis=1) > 0
    exp_mask = jnp.repeat(grp_mask, N_EXPERTS // N_EXPERT_GROUPS, axis=1)
    _, top_e = lax.top_k(jnp.where(exp_mask, sel, -jnp.inf), TOP_K)
    wts = jnp.take_along_axis(scores, top_e, axis=1)
    wts = wts / jnp.sum(wts, axis=-1, keepdims=True) * ROUTED_SCALE
    n_assign = t_tokens * TOP_K
    flat_e = top_e.reshape(n_assign).astype(jnp.int32)
    flat_t = jnp.repeat(jnp.arange(t_tokens, dtype=jnp.int32), TOP_K)
    flat_w = wts.reshape(n_assign)
    order = jnp.argsort(flat_e)
    e_sorted = flat_e[order]
    counts = jnp.zeros((N_EXPERTS,), jnp.int32).at[flat_e].add(1)
    starts = jnp.cumsum(counts) - counts
    pcounts = (counts + MOE_BLOCK - 1) // MOE_BLOCK * MOE_BLOCK
    pends = jnp.cumsum(pcounts)
    pstarts = pends - pcounts
    dest = pstarts[e_sorted] + (jnp.arange(n_assign, dtype=jnp.int32) - starts[e_sorted])
    n_blocks = -(-n_assign // MOE_BLOCK) + N_EXPERTS
    n_slots = n_blocks * MOE_BLOCK
    slot_t = jnp.full((n_slots,), t_tokens, jnp.int32).at[dest].set(flat_t[order])
    slot_w = jnp.zeros((n_slots,), jnp.float32).at[dest].set(flat_w[order])
    block_e = jnp.clip(jnp.searchsorted(pends, jnp.arange(n_blocks, dtype=jnp.int32) * MOE_BLOCK,
                                        side='right'), 0, N_EXPERTS - 1)
    h_pad = jnp.concatenate([h, jnp.zeros((1, D_MODEL), h.dtype)], axis=0)

    def expert_block(args):
        idx, bw, e = args
        xb = h_pad[idx]
        act = jax.nn.silu(jnp.dot(xb, w_gate_e[e])) * jnp.dot(xb, w_up_e[e])
        return jnp.dot(act, w_down_e[e]) * bw[:, None].astype(h.dtype)

    yb = lax.map(expert_block, (slot_t.reshape(n_blocks, MOE_BLOCK),
                                slot_w.reshape(n_blocks, MOE_BLOCK), block_e))
    routed = jnp.zeros((t_tokens + 1, D_MODEL), h.dtype).at[slot_t].add(
        yb.reshape(n_slots, D_MODEL))[:t_tokens]
    shared = jnp.dot(jax.nn.silu(jnp.dot(h, w_gate_s)) * jnp.dot(h, w_up_s), w_down_s)
    return routed + shared


def setup_inputs(seed: int = 0) -> dict:
    key = jax.random.key(seed)
    ks = jax.random.split(key, 32)

    def nrm(k, shape, scale):
        return jax.random.normal(k, shape, jnp.float32) * scale

    L = DEPTH
    return {
        'x_prompt': nrm(ks[0], (BATCH, SEQ, D_MODEL), 1.0),
        'x_sample': nrm(ks[1], (DEC_BATCH, DEC_SEQ, D_MODEL), 1.0),
        'state_conv': nrm(ks[2], (L, DEC_BATCH, CONV_W - 1, B_WIDTH), 0.5),
        'c_prompt': nrm(ks[3], (BATCH, D_MODEL), 1.0),
        'c_sample': nrm(ks[4], (DEC_BATCH, D_MODEL), 1.0),
        'w_ada': nrm(ks[5], (L, D_MODEL, N_MOD * D_MODEL), 0.5 * D_MODEL ** -0.5),
        'b_ada': nrm(ks[6], (L, N_MOD * D_MODEL), 0.01),
        'g_norm1': 1.0 + nrm(ks[7], (L, D_MODEL), 0.01),
        'w_in': nrm(ks[8], (L, D_MODEL, IN_COLS), D_MODEL ** -0.5),
        'a_norm_g': 1.0 + nrm(ks[9], (L, A_WIDTH), 0.01),
        'a_norm_b': nrm(ks[10], (L, A_WIDTH), 0.01),
        'w_s': nrm(ks[11], (L, A_GROUPS, CHUNK, CHUNK), 0.5 * CHUNK ** -0.5),
        'b_s': 1.0 + nrm(ks[12], (L, A_GROUPS, CHUNK), 0.01),
        'conv_w': nrm(ks[13], (L, CONV_W, B_WIDTH), CONV_W ** -0.5),
        'conv_b': nrm(ks[14], (L, B_WIDTH), 0.01),
        'b_norm_g': 1.0 + nrm(ks[15], (L, B_WIDTH), 0.01),
        'b_norm_b': nrm(ks[16], (L, B_WIDTH), 0.01),
        'w_a_out': nrm(ks[17], (L, A_WIDTH, D_MODEL), A_WIDTH ** -0.5),
        'w_b_out': nrm(ks[18], (L, B_WIDTH, D_MODEL), B_WIDTH ** -0.5),
        'w_o': nrm(ks[19], (L, D_MODEL, D_MODEL), D_MODEL ** -0.5),
        'g_norm2': 1.0 + nrm(ks[20], (L, D_MODEL), 0.01),
        'w_router': nrm(ks[21], (L, D_MODEL, N_EXPERTS), D_MODEL ** -0.5),
        'router_bias': nrm(ks[22], (L, N_EXPERTS), 0.01),
        'w_gate_e': nrm(ks[23], (L, N_EXPERTS, D_MODEL, D_EXPERT), D_MODEL ** -0.5),
        'w_up_e': nrm(ks[24], (L, N_EXPERTS, D_MODEL, D_EXPERT), D_MODEL ** -0.5),
        'w_down_e': nrm(ks[25], (L, N_EXPERTS, D_EXPERT, D_MODEL), D_EXPERT ** -0.5),
        'w_gate_s': nrm(ks[26], (L, D_MODEL, D_SHARED), D_MODEL ** -0.5),
        'w_up_s': nrm(ks[27], (L, D_MODEL, D_SHARED), D_MODEL ** -0.5),
        'w_down_s': nrm(ks[28], (L, D_SHARED, D_MODEL), D_SHARED ** -0.5),
        'g_final': 1.0 + nrm(ks[29], (D_MODEL,), 0.01),
    }


def reference(x_prompt, x_sample, state_conv, c_prompt, c_sample, w_ada, b_ada, g_norm1, w_in,
              a_norm_g, a_norm_b, w_s, b_s, conv_w, conv_b, b_norm_g, b_norm_b, w_a_out, w_b_out,
              w_o, g_norm2, w_router, router_bias, w_gate_e, w_up_e, w_down_e, w_gate_s, w_up_s,
              w_down_s, g_final):
    xp, xs = x_prompt, x_sample
    n_prompt_tok = xp.shape[0] * xp.shape[1]
    conv_p, conv_s, chunkv_p, chunkv_s = [], [], [], []
    for l in range(DEPTH):
        mp = ada_mod(c_prompt, w_ada[l], b_ada[l])
        ms = ada_mod(c_sample, w_ada[l], b_ada[l])
        mix_w = (w_in[l], a_norm_g[l], a_norm_b[l], w_s[l], b_s[l], conv_w[l], conv_b[l],
                 b_norm_g[l], b_norm_b[l], w_a_out[l], w_b_out[l], w_o[l])
        hp = rms_norm(xp, g_norm1[l]) * (1 + mp[:, 1]) + mp[:, 0]
        hs = rms_norm(xs, g_norm1[l]) * (1 + ms[:, 1]) + ms[:, 0]
        zero_past = jnp.zeros((xp.shape[0], CONV_W - 1, B_WIDTH), xp.dtype)
        op, cp_new, vp_new = token_mixer(hp, zero_past, CHUNK, *mix_w)
        os_, cs_new, vs_new = token_mixer(hs, state_conv[l], xs.shape[1], *mix_w)
        xp = xp + mp[:, 2] * op
        xs = xs + ms[:, 2] * os_
        hp2 = rms_norm(xp, g_norm2[l]) * (1 + mp[:, 4]) + mp[:, 3]
        hs2 = rms_norm(xs, g_norm2[l]) * (1 + ms[:, 4]) + ms[:, 3]
        tokens = jnp.concatenate([hp2.reshape(-1, D_MODEL), hs2.reshape(-1, D_MODEL)], axis=0)
        f = moe_ffn(tokens, w_router[l], router_bias[l], w_gate_e[l], w_up_e[l], w_down_e[l],
                    w_gate_s[l], w_up_s[l], w_down_s[l])
        xp = xp + mp[:, 5] * f[:n_prompt_tok].reshape(xp.shape)
        xs = xs + ms[:, 5] * f[n_prompt_tok:].reshape(xs.shape)
        conv_p.append(cp_new)
        conv_s.append(cs_new)
        chunkv_p.append(vp_new)
        chunkv_s.append(vs_new)
    y_prompt = rms_norm(xp, g_final)
    y_sample = rms_norm(xs, g_final)
    return (y_prompt, y_sample, jnp.stack(conv_p), jnp.stack(conv_s), jnp.stack(chunkv_p),
            jnp.stack(chunkv_s))
```

```python
import functools

import jax
import jax.numpy as jnp
from jax import lax
from jax.experimental import pallas as pl
from jax.experimental.pallas import tpu as pltpu

F32 = jnp.float32
BF16 = jnp.bfloat16
I32 = jnp.int32

EPS = 1e-6
CHUNK = 128
A_GROUPS = 8
CONV_W = 31
N_EXPERTS = 256
TOP_K = 8
N_EXPERT_GROUPS = 8
GROUP_SIZE = N_EXPERTS // N_EXPERT_GROUPS
TOPK_GROUPS = 4
ROUTED_SCALE = 2.5
MOE_BLOCK = 128
LANES = 128
SUBLANES = 8
VMEM_LIMIT = 58 * 1024 * 1024

NEG_INF = float("-inf")


def _cparams(*sem):
    return pltpu.CompilerParams(dimension_semantics=sem, vmem_limit_bytes=VMEM_LIMIT)


def _const_spec(shape):
    nd = len(shape)
    return pl.BlockSpec(shape, lambda *_: (0,) * nd, pipeline_mode=pl.Buffered(1))


def _rms(x):
    return x * lax.rsqrt(jnp.mean(x * x, axis=-1, keepdims=True) + EPS)


def _layer_norm(x, g, b):
    mu = jnp.mean(x, axis=-1, keepdims=True)
    xc = x - mu
    var = jnp.mean(xc * xc, axis=-1, keepdims=True)
    return xc * lax.rsqrt(var + EPS) * g + b


def _bdot(a, b):
    return jnp.dot(a, b, preferred_element_type=F32)


def _lane_sum(x):
    return jnp.sum(x, axis=-1, keepdims=True)


def _lane_max(x):
    return jnp.max(x, axis=-1, keepdims=True)


def _lane_min(x):
    return jnp.min(x, axis=-1, keepdims=True)


def _ada_kernel(c_ref, w_ref, b_ref, o_ref):
    c = c_ref[...]
    s = (c * jax.nn.sigmoid(c)).astype(BF16)
    o_ref[...] = _bdot(s, w_ref[...].astype(BF16)) + b_ref[...]


def _ada(c_all, w_ada, b_ada, tn=1024):
    m, d = c_all.shape
    n = w_ada.shape[1]
    return pl.pallas_call(
        _ada_kernel,
        grid=(n // tn,),
        in_specs=[pl.BlockSpec((m, d), lambda j: (0, 0)),
                  pl.BlockSpec((d, tn), lambda j: (0, j)),
                  pl.BlockSpec((1, tn), lambda j: (0, j))],
        out_specs=pl.BlockSpec((m, tn), lambda j: (0, j)),
        out_shape=jax.ShapeDtypeStruct((m, n), F32),
        compiler_params=_cparams("arbitrary"),
        name="ada",
    )(c_all, w_ada, b_ada.reshape(1, n))


def _hmod_kernel(n_ptiles, n_s, xp_ref, xs_ref, g_ref, scp_ref, shp_ref, scs_ref, shs_ref, o_ref):
    i = pl.program_id(0)

    def body(x, sc, sh):
        return ((_rms(x) * g_ref[...]) * (1.0 + sc) + sh).astype(o_ref.dtype)

    @pl.when(i < n_ptiles)
    def _():
        o_ref[...] = body(xp_ref[...], scp_ref[...], shp_ref[...])

    @pl.when(i == n_ptiles)
    def _():
        o_ref[0:n_s, :] = body(xs_ref[...], scs_ref[...], shs_ref[...])


def _hmod(xp, xs, g, scp, shp, scs, shs, seq, tm=512):
    tp, d = xp.shape
    ts = xs.shape[0]
    npt = tp // tm
    per = seq // tm
    last = npt - 1
    pmod = pl.BlockSpec((None, 1, d), lambda i: (jnp.minimum(i, last) // per, 0, 0))
    full = lambda i: (0, 0)
    return pl.pallas_call(
        functools.partial(_hmod_kernel, npt, ts),
        grid=(npt + 1,),
        in_specs=[pl.BlockSpec((tm, d), lambda i: (jnp.minimum(i, last), 0)),
                  pl.BlockSpec((ts, d), full),
                  pl.BlockSpec((1, d), full),
                  pmod, pmod,
                  pl.BlockSpec((ts, d), full),
                  pl.BlockSpec((ts, d), full)],
        out_specs=pl.BlockSpec((tm, d), lambda i: (i, 0)),
        out_shape=jax.ShapeDtypeStruct((tp + ts, d), BF16),
        compiler_params=_cparams("arbitrary"),
        name="hmod",
    )(xp, xs, g.reshape(1, d), scp, shp, scs, shs)


def _rows_call(kernel, h, n_tail, tm, weights, vecs, out_dtypes, name):
    t, d = h.shape
    n_full = (t - n_tail) // tm

    def wrapped(*refs):
        i = pl.program_id(0)

        @pl.when(i < n_full)
        def _():
            kernel(tm, *refs)

        @pl.when(i == n_full)
        def _():
            kernel(n_tail, *refs)

    in_specs = [pl.BlockSpec((tm, d), lambda i: (i, 0))]
    in_specs += [pl.BlockSpec(bs, im, pipeline_mode=pl.Buffered(1)) for (_, bs, im) in weights]
    in_specs += [pl.BlockSpec(v.shape, lambda i: (0, 0)) for v in vecs]
    return pl.pallas_call(
        wrapped,
        grid=(n_full + 1,),
        in_specs=in_specs,
        out_specs=[pl.BlockSpec((tm, d), lambda i: (i, 0)) for _ in out_dtypes],
        out_shape=[jax.ShapeDtypeStruct((t, d), dt) for dt in out_dtypes],
        compiler_params=_cparams("arbitrary"),
        name=name,
    )(h, *[w for (w, _, _) in weights], *vecs)


def _seg_u_kernel(rows, h_ref, w_ref, o_ref):
    z = _bdot(h_ref[0:rows, :], w_ref[...])
    o_ref[0:rows, :] = jax.nn.gelu(z).astype(o_ref.dtype)


def _seg_v_kernel(rows, h_ref, w_ref, g_ref, b_ref, o32_ref, o16_ref):
    z = _bdot(h_ref[0:rows, :], w_ref[...])
    v = _layer_norm(jax.nn.gelu(z), g_ref[...], b_ref[...])
    o32_ref[0:rows, :] = v
    o16_ref[0:rows, :] = v.astype(o16_ref.dtype)


def _seg_glu_kernel(rows, h_ref, wa_ref, wb_ref, o_ref):
    h = h_ref[0:rows, :]
    o_ref[0:rows, :] = _bdot(h, wa_ref[...]) * jax.nn.sigmoid(_bdot(h, wb_ref[...]))


def _seg_gate_kernel(rows, h_ref, wa_ref, wb_ref, oa_ref, ob_ref):
    h = h_ref[0:rows, :]
    oa_ref[0:rows, :] = jax.nn.sigmoid(_bdot(h, wa_ref[...])).astype(oa_ref.dtype)
    ob_ref[0:rows, :] = jax.nn.sigmoid(_bdot(h, wb_ref[...])).astype(ob_ref.dtype)


def _col_block(w, d, seg):
    return (w, (w.shape[0], d), lambda i, seg=seg: (0, seg))


CONV_ROWS = 64
CONV_LANES = 256


def _mixp_kernel(tm, halo, per, u_ref, v_ref, glu_ref, prev_ref, ws_ref, bst_ref, cw_ref, cb_ref,
                 ng_ref, nb_ref, ya_ref, yb_ref, xp_ref, cv_ref):
    i = pl.program_id(0)
    d = u_ref.shape[1]
    n_chunks = tm // CHUNK
    gdim = d // A_GROUPS

    r_io = lax.broadcasted_iota(I32, (CHUNK, CHUNK), 0)
    c_io = lax.broadcasted_iota(I32, (CHUNK, CHUNK), 1)
    tril = c_io <= r_io
    for g in range(A_GROUPS):
        cs = slice(g * gdim, (g + 1) * gdim)
        wg = jnp.where(tril, ws_ref[g], 0.0).astype(BF16)
        vg = jnp.concatenate([v_ref[c * CHUNK:(c + 1) * CHUNK, cs] for c in range(n_chunks)], axis=1)
        mix = _bdot(wg, vg)
        bcol = bst_ref[:, g:g + 1]
        for c in range(n_chunks):
            rs = slice(c * CHUNK, (c + 1) * CHUNK)
            ya_ref[rs, cs] = (u_ref[rs, cs].astype(F32)
                              * (mix[:, c * gdim:(c + 1) * gdim] + bcol)).astype(ya_ref.dtype)

    xp_ref[0:halo, :] = jnp.where(i % per == 0, 0.0, prev_ref[...])
    xp_ref[halo:halo + tm, :] = glu_ref[...]
    first = halo - (CONV_W - 1)
    n_cb = d // CONV_LANES

    def chunk(idx, carry):
        r0 = pl.multiple_of((idx // n_cb) * CONV_ROWS, CONV_ROWS)
        c0 = pl.multiple_of((idx % n_cb) * CONV_LANES, CONV_LANES)
        acc = jnp.zeros((CONV_ROWS, CONV_LANES), F32)
        window = xp_ref[pl.ds(r0, CONV_ROWS + halo), pl.ds(c0, CONV_LANES)]
        for j in range(SUBLANES):
            taps = [m for m in range(first, first + CONV_W) if m % SUBLANES == j]
            span = taps[-1] - j + CONV_ROWS
            s_j = window[j:j + span, :]
            for m in taps:
                w_row = cw_ref[pl.ds(m - first, 1), pl.ds(c0, CONV_LANES)]
                acc = acc + w_row * s_j[m - j:m - j + CONV_ROWS, :]
        cv_ref[pl.ds(r0, CONV_ROWS), pl.ds(c0, CONV_LANES)] = acc
        return carry

    lax.fori_loop(0, (tm // CONV_ROWS) * n_cb, chunk, 0)

    y = _layer_norm(cv_ref[...] + cb_ref[...], ng_ref[...], nb_ref[...])
    yb_ref[...] = (y * jax.nn.sigmoid(y)).astype(yb_ref.dtype)


def _mixp(u, v16, glu, w_s, b_s_t, conv_w, conv_b, ng, nb, n_prompt, seq, tm=512, halo=32):
    d = u.shape[1]
    npt = n_prompt // tm
    per = seq // tm
    hb = tm // halo
    row = lambda i: (i, 0)
    return pl.pallas_call(
        functools.partial(_mixp_kernel, tm, halo, per),
        grid=(npt,),
        in_specs=[pl.BlockSpec((tm, d), row),
                  pl.BlockSpec((tm, d), row),
                  pl.BlockSpec((tm, d), row),
                  pl.BlockSpec((halo, d), lambda i: (jnp.maximum(i * hb - 1, 0), 0)),
                  _const_spec(w_s.shape), _const_spec(b_s_t.shape), _const_spec(conv_w.shape),
                  _const_spec(conv_b.shape), _const_spec(ng.shape), _const_spec(nb.shape)],
        out_specs=[pl.BlockSpec((tm, d), row), pl.BlockSpec((tm, d), row)],
        out_shape=[jax.ShapeDtypeStruct((n_prompt, d), BF16), jax.ShapeDtypeStruct((n_prompt, d), BF16)],
        scratch_shapes=[pltpu.VMEM((halo + tm, d), F32), pltpu.VMEM((tm, d), F32)],
        compiler_params=_cparams("arbitrary"),
        name="mixp",
    )(u, v16, glu, glu, w_s, b_s_t, conv_w, conv_b, ng, nb)


def _mixs_kernel(u_ref, v_ref, glu_ref, st_ref, ws0_ref, bs0_ref, cw_ref, cb_ref, ng_ref, nb_ref,
                 ya_ref, yb_ref):
    n_past = st_ref.shape[1]
    ya_ref[...] = (u_ref[...].astype(F32) * (ws0_ref[...] * v_ref[...] + bs0_ref[...])).astype(ya_ref.dtype)
    acc = cw_ref[n_past:n_past + 1, :] * glu_ref[...] + cb_ref[...]
    acc = acc + jnp.sum(st_ref[...] * cw_ref[0:n_past, :][None], axis=1)
    y = _layer_norm(acc, ng_ref[...], nb_ref[...])
    yb_ref[...] = (y * jax.nn.sigmoid(y)).astype(yb_ref.dtype)


def _mixs(u, v32, glu, state, ws0, bs0, conv_w, conv_b, ng, nb, n_prompt, tb=32):
    ts, n_past, d = state.shape
    off = n_prompt // tb
    tail = lambda j: (off + j, 0)
    vec = lambda j: (0, 0)
    return pl.pallas_call(
        _mixs_kernel,
        grid=(ts // tb,),
        in_specs=[pl.BlockSpec((tb, d), tail), pl.BlockSpec((tb, d), tail), pl.BlockSpec((tb, d), tail),
                  pl.BlockSpec((tb, n_past, d), lambda j: (j, 0, 0)),
                  pl.BlockSpec((1, d), vec), pl.BlockSpec((1, d), vec),
                  pl.BlockSpec(conv_w.shape, vec), pl.BlockSpec((1, d), vec),
                  pl.BlockSpec((1, d), vec), pl.BlockSpec((1, d), vec)],
        out_specs=[pl.BlockSpec((tb, d), lambda j: (j, 0)), pl.BlockSpec((tb, d), lambda j: (j, 0))],
        out_shape=[jax.ShapeDtypeStruct((ts, d), BF16), jax.ShapeDtypeStruct((ts, d), BF16)],
        compiler_params=_cparams("arbitrary"),
        name="mixs",
    )(u, v32, glu, state, ws0, bs0, conv_w, conv_b, ng, nb)


def _proj_kernel(n_ptiles, tm, n_s,
                 yap_ref, ybp_ref, yas_ref, ybs_ref, ga_ref, gb_ref, xp_ref, xs_ref,
                 wa_ref, wb_ref, wo_ref, wrh_ref, wrl_ref, g2_ref,
                 gtp_ref, scp_ref, shp_ref, gts_ref, scs_ref, shs_ref,
                 x1_ref, h2_ref, lg_ref):
    i = pl.program_id(0)

    def body(rows, ya, yb, x, gate, sc, sh):
        pa = _bdot(ya, wa_ref[...])
        pb = _bdot(yb, wb_ref[...])
        merged = ga_ref[0:rows, :].astype(F32) * pa + gb_ref[0:rows, :].astype(F32) * pb
        out = _bdot(merged.astype(BF16), wo_ref[...])
        x1 = x + gate * out
        h2 = (_rms(x1) * g2_ref[...]) * (1.0 + sc) + sh
        hi = h2.astype(BF16)
        lo = (h2 - hi.astype(F32)).astype(BF16)
        logits = _bdot(hi, wrh_ref[...]) + (_bdot(hi, wrl_ref[...]) + _bdot(lo, wrh_ref[...]))
        x1_ref[0:rows, :] = x1
        h2_ref[0:rows, :] = h2
        lg_ref[0:rows, :] = logits

    @pl.when(i < n_ptiles)
    def _():
        body(tm, yap_ref[...], ybp_ref[...], xp_ref[...], gtp_ref[...], scp_ref[...], shp_ref[...])

    @pl.when(i == n_ptiles)
    def _():
        body(n_s, yas_ref[...], ybs_ref[...], xs_ref[...], gts_ref[...], scs_ref[...], shs_ref[...])


def _proj(yap, ybp, yas, ybs, ga, gb, xp, xs, wa, wb, wo, wrh, wrl, g2,
          gtp, scp, shp, gts, scs, shs, seq, tm=256):
    tp, d = xp.shape
    ts = xs.shape[0]
    t = tp + ts
    ne = wrh.shape[1]
    npt = tp // tm
    per = seq // tm
    last = npt - 1
    prow = lambda i: (jnp.minimum(i, last), 0)
    row = lambda i: (i, 0)
    full = lambda i: (0, 0)
    pmod = pl.BlockSpec((None, 1, d), lambda i: (jnp.minimum(i, last) // per, 0, 0))
    smod = pl.BlockSpec((ts, d), full)
    return pl.pallas_call(
        functools.partial(_proj_kernel, npt, tm, ts),
        grid=(npt + 1,),
        in_specs=[pl.BlockSpec((tm, d), prow), pl.BlockSpec((tm, d), prow),
                  pl.BlockSpec((ts, d), full), pl.BlockSpec((ts, d), full),
                  pl.BlockSpec((tm, d), row), pl.BlockSpec((tm, d), row),
                  pl.BlockSpec((tm, d), prow), pl.BlockSpec((ts, d), full),
                  _const_spec(wa.shape), _const_spec(wb.shape), _const_spec(wo.shape),
                  _const_spec(wrh.shape), _const_spec(wrl.shape), _const_spec((1, d)),
                  pmod, pmod, pmod, smod, smod, smod],
        out_specs=[pl.BlockSpec((tm, d), row), pl.BlockSpec((tm, d), row), pl.BlockSpec((tm, ne), row)],
        out_shape=[jax.ShapeDtypeStruct((t, d), F32), jax.ShapeDtypeStruct((t, d), F32),
                   jax.ShapeDtypeStruct((t, ne), F32)],
        compiler_params=_cparams("arbitrary"),
        name="proj",
    )(yap, ybp, yas, ybs, ga, gb, xp, xs, wa, wb, wo, wrh, wrl, g2.reshape(1, d),
      gtp, scp, shp, gts, scs, shs)


def _pack_cols(cols, lane):
    out = jnp.zeros(lane.shape, cols[0].dtype)
    for k, c in enumerate(cols):
        out = jnp.where(lane == k, c, out)
    return out


def _route_kernel(lg_ref, bias_ref, tri_ref, idx_ref, pos_ref, wt_ref, cnt_ref, carry_ref):
    i = pl.program_id(0)

    @pl.when(i == 0)
    def _():
        carry_ref[...] = jnp.zeros_like(carry_ref)

    scores = jax.nn.sigmoid(lg_ref[...])
    sel = scores + bias_ref[...]
    lane = lax.broadcasted_iota(I32, sel.shape, 1)
    lane_f = lane.astype(F32)
    grp = jnp.right_shift(lane, GROUP_SIZE.bit_length() - 1)

    gs_cols = []
    gs_full = jnp.zeros(sel.shape, F32)
    for g in range(N_EXPERT_GROUPS):
        in_g = grp == g
        xg = jnp.where(in_g, sel, NEG_INF)
        m1 = _lane_max(xg)
        n1 = _lane_sum(jnp.where(xg == m1, 1.0, 0.0))
        m2 = jnp.where(n1 >= 2.0, m1, _lane_max(jnp.where(xg < m1, xg, NEG_INF)))
        gs = m1 + m2
        gs_cols.append(gs)
        gs_full = jnp.where(in_g, gs, gs_full)

    beaten = jnp.zeros(sel.shape, F32)
    for g in range(N_EXPERT_GROUPS):
        better = (gs_cols[g] > gs_full) | ((gs_cols[g] == gs_full) & (g < grp))
        beaten = beaten + jnp.where(better, 1.0, 0.0)
    cur = jnp.where(beaten < float(TOPK_GROUPS), sel, NEG_INF)

    idx_cols, s_cols, hots = [], [], []
    picked = jnp.zeros(sel.shape, F32)
    for _ in range(TOP_K):
        m = _lane_max(cur)
        idx = _lane_min(jnp.where(cur == m, lane_f, float(N_EXPERTS)))
        hot = lane_f == idx
        idx_cols.append(idx)
        s_cols.append(_lane_sum(jnp.where(hot, scores, 0.0)))
        hots.append(hot)
        picked = jnp.where(hot, 1.0, picked)
        cur = jnp.where(hot, NEG_INF, cur)

    den = s_cols[0]
    for s in s_cols[1:]:
        den = den + s
    w_cols = [s / den * ROUTED_SCALE for s in s_cols]

    pos_full = _bdot(tri_ref[...], picked.astype(BF16)) + carry_ref[...]
    pos_cols = [_lane_sum(jnp.where(h, pos_full, 0.0)) for h in hots]
    carry_ref[...] = carry_ref[...] + jnp.sum(picked, axis=0, keepdims=True)
    cnt_ref[...] = carry_ref[...]

    lane_o = lax.broadcasted_iota(I32, idx_ref.shape, 1)
    idx_ref[...] = _pack_cols(idx_cols, lane_o).astype(I32)
    pos_ref[...] = _pack_cols(pos_cols, lane_o).astype(I32)
    wt_ref[...] = _pack_cols(w_cols, lane_o)


def _route(logits, bias, tm=128):
    t, ne = logits.shape
    tri = jnp.tril(jnp.ones((tm, tm), BF16), -1)
    row = lambda i: (i, 0)
    return pl.pallas_call(
        _route_kernel,
        grid=(t // tm,),
        in_specs=[pl.BlockSpec((tm, ne), row), pl.BlockSpec((1, ne), lambda i: (0, 0)),
                  pl.BlockSpec((tm, tm), lambda i: (0, 0))],
        out_specs=[pl.BlockSpec((tm, LANES), row), pl.BlockSpec((tm, LANES), row),
                   pl.BlockSpec((tm, LANES), row), pl.BlockSpec((1, ne), lambda i: (0, 0))],
        out_shape=[jax.ShapeDtypeStruct((t, LANES), I32), jax.ShapeDtypeStruct((t, LANES), I32),
                   jax.ShapeDtypeStruct((t, LANES), F32), jax.ShapeDtypeStruct((1, ne), F32)],
        scratch_shapes=[pltpu.VMEM((1, ne), F32)],
        compiler_params=_cparams("arbitrary"),
        name="route",
    )(logits, bias.reshape(1, ne), tri)


def _slots_kernel(n_blocks, cnt_ref, idx_ref, pos_ref, triu_ref, slot_ref, be_ref, nv_ref, ps_ref):
    i = pl.program_id(0)
    ne = cnt_ref.shape[1]

    @pl.when(i == 0)
    def _():
        cnt = cnt_ref[...]
        nblk = jnp.floor((cnt + float(MOE_BLOCK - 1)) * (1.0 / MOE_BLOCK))
        pend = _bdot(jnp.broadcast_to(nblk, (SUBLANES, ne)).astype(BF16), triu_ref[...])[0:1, :]
        pstart = pend - nblk
        ps_ref[...] = pstart
        b_io = lax.broadcasted_iota(I32, (n_blocks, ne), 0).astype(F32)
        e_io = lax.broadcasted_iota(I32, (n_blocks, ne), 1).astype(F32)
        be = jnp.minimum(_lane_sum(jnp.where(pend <= b_io, 1.0, 0.0)), float(ne - 1))
        hot = e_io == be
        cnt_e = _lane_sum(jnp.where(hot, cnt, 0.0))
        ps_e = _lane_sum(jnp.where(hot, pstart, 0.0))
        b_col = b_io[:, 0:1]
        nv = jnp.clip(cnt_e - float(MOE_BLOCK) * (b_col - ps_e), 0.0, float(MOE_BLOCK))
        be_ref[...] = jnp.broadcast_to(be, be_ref.shape).astype(I32)
        nv_ref[...] = jnp.broadcast_to(nv, nv_ref.shape).astype(I32)

    lane_o = lax.broadcasted_iota(I32, idx_ref.shape, 1)
    lane_e = lax.broadcasted_iota(I32, (idx_ref.shape[0], ne), 1)
    idx = idx_ref[...].astype(F32)
    pos = pos_ref[...].astype(F32)
    pstart = ps_ref[...]
    lane_ef = lane_e.astype(F32)
    cols = []
    for k in range(TOP_K):
        e_k = _lane_sum(jnp.where(lane_o == k, idx, 0.0))
        p_k = _lane_sum(jnp.where(lane_o == k, pos, 0.0))
        ps_k = _lane_sum(jnp.where(lane_ef == e_k, pstart, 0.0))
        cols.append(ps_k * float(MOE_BLOCK) + p_k)
    slot_ref[...] = _pack_cols(cols, lane_o).astype(I32)


def _slots(cnt, idx, pos, n_blocks, tm=128):
    t = idx.shape[0]
    ne = cnt.shape[1]
    triu = jnp.triu(jnp.ones((ne, ne), BF16))
    row = lambda i: (i, 0)
    full = lambda i: (0, 0)
    return pl.pallas_call(
        functools.partial(_slots_kernel, n_blocks),
        grid=(t // tm,),
        in_specs=[pl.BlockSpec((1, ne), full), pl.BlockSpec((tm, LANES), row),
                  pl.BlockSpec((tm, LANES), row), pl.BlockSpec((ne, ne), full)],
        out_specs=[pl.BlockSpec((tm, LANES), row), pl.BlockSpec((n_blocks, LANES), full),
                   pl.BlockSpec((n_blocks, LANES), full)],
        out_shape=[jax.ShapeDtypeStruct((t, LANES), I32), jax.ShapeDtypeStruct((n_blocks, LANES), I32),
                   jax.ShapeDtypeStruct((n_blocks, LANES), I32)],
        scratch_shapes=[pltpu.VMEM((1, ne), F32)],
        compiler_params=_cparams("arbitrary"),
        name="slots",
    )(cnt, idx, pos, triu)


def _invert_kernel(per_step, slot_ref, out_ref):
    i = pl.program_id(0)

    @pl.when(i == 0)
    def _():
        def fill(j, c):
            out_ref[j] = 0
            return c
        lax.fori_loop(0, out_ref.shape[0], fill, 0, unroll=8)

    base = i * per_step

    def put(j, c):
        out_ref[slot_ref[j]] = jnp.right_shift(base + j, TOP_K.bit_length() - 1)
        return c
    lax.fori_loop(0, per_step, put, 0, unroll=8)


def _invert(slot_flat, n_slots, steps=5):
    n = slot_flat.shape[0]
    per = n // steps
    assert per * steps == n and per % 1024 == 0, "rank-1 SMEM blocks must be multiples of 1024"
    return pl.pallas_call(
        functools.partial(_invert_kernel, per),
        grid=(steps,),
        in_specs=[pl.BlockSpec((per,), lambda i: (i,), memory_space=pltpu.SMEM)],
        out_specs=pl.BlockSpec((n_slots,), lambda i: (0,), memory_space=pltpu.SMEM),
        out_shape=jax.ShapeDtypeStruct((n_slots,), I32),
        compiler_params=_cparams("arbitrary"),
        name="invert",
    )(slot_flat)


def _experts_kernel(n_blocks, be_ref, nv_ref, ids_ref, nxt_ref, h_hbm, wg_ref, wu_ref, wd_ref,
                    o_ref, xbuf, sem):
    b = pl.program_id(0)
    slot = b % 2

    def gather(ids, dst_slot):
        for r in range(MOE_BLOCK):
            pltpu.make_async_copy(h_hbm.at[pl.ds(ids[0, 0, r], 1)],
                                  xbuf.at[dst_slot, pl.ds(r, 1)], sem.at[dst_slot]).start()

    @pl.when((b == 0) & (nv_ref[0] > 0))
    def _():
        gather(ids_ref, 0)

    nb = jnp.minimum(b + 1, n_blocks - 1)

    @pl.when((b + 1 < n_blocks) & (nv_ref[nb] > 0))
    def _():
        gather(nxt_ref, 1 - slot)

    @pl.when(nv_ref[b] > 0)
    def _():
        pltpu.make_async_copy(h_hbm.at[pl.ds(0, MOE_BLOCK)], xbuf.at[slot], sem.at[slot]).wait()
        x = xbuf[slot].astype(BF16)
        g = _bdot(x, wg_ref[...].astype(BF16))
        u = _bdot(x, wu_ref[...].astype(BF16))
        act = (g * jax.nn.sigmoid(g) * u).astype(BF16)
        o_ref[...] = _bdot(act, wd_ref[...].astype(BF16))


def _experts(be, nv, slot_t, h2, wg, wu, wd):
    n_blocks = be.shape[0]
    t, d = h2.shape
    ne, _, de = wg.shape
    ids = slot_t.reshape(n_blocks, 1, MOE_BLOCK)
    grid_spec = pltpu.PrefetchScalarGridSpec(
        num_scalar_prefetch=2,
        grid=(n_blocks,),
        in_specs=[
            pl.BlockSpec((1, 1, MOE_BLOCK), lambda b, be, nv: (b, 0, 0), memory_space=pltpu.SMEM),
            pl.BlockSpec((1, 1, MOE_BLOCK), lambda b, be, nv: (jnp.minimum(b + 1, n_blocks - 1), 0, 0),
                         memory_space=pltpu.SMEM),
            pl.BlockSpec(memory_space=pl.ANY),
            pl.BlockSpec((None, d, de), lambda b, be, nv: (be[b], 0, 0)),
            pl.BlockSpec((None, d, de), lambda b, be, nv: (be[b], 0, 0)),
            pl.BlockSpec((None, de, d), lambda b, be, nv: (be[b], 0, 0)),
        ],
        out_specs=pl.BlockSpec((MOE_BLOCK, d), lambda b, be, nv: (jnp.where(nv[b] > 0, b, n_blocks), 0)),
        scratch_shapes=[pltpu.VMEM((2, MOE_BLOCK, d), F32), pltpu.SemaphoreType.DMA((2,))],
    )
    return pl.pallas_call(
        functools.partial(_experts_kernel, n_blocks),
        grid_spec=grid_spec,
        out_shape=jax.ShapeDtypeStruct(((n_blocks + 1) * MOE_BLOCK, d), F32),
        compiler_params=_cparams("arbitrary"),
        name="experts",
    )(be, nv, ids, ids, h2, wg, wu, wd)


def _combine_kernel(n_tiles, n_ptiles, tm, tab_ref, nxt_ref, ys_hbm, wt_ref, h2_ref, x1_ref,
                    wgs_ref, wus_ref, wds_ref, gf_ref, gtp_ref, gts_ref,
                    yp_ref, ysm_ref, ybuf, sem):
    i = pl.program_id(0)
    slot = i % 2

    def gather(tab, dst_slot):
        def row(r, c):
            for k in range(TOP_K):
                pltpu.make_async_copy(ys_hbm.at[pl.ds(tab[0, 0, r * TOP_K + k], 1)],
                                      ybuf.at[dst_slot, k, pl.ds(r, 1)], sem.at[dst_slot]).start()
            return c
        lax.fori_loop(0, tm, row, 0, unroll=4)

    @pl.when(i == 0)
    def _():
        gather(tab_ref, 0)

    @pl.when(i + 1 < n_tiles)
    def _():
        gather(nxt_ref, 1 - slot)

    h = h2_ref[...].astype(BF16)
    g = _bdot(h, wgs_ref[...])
    u = _bdot(h, wus_ref[...])
    f = _bdot((g * jax.nn.sigmoid(g) * u).astype(BF16), wds_ref[...])

    for k in range(TOP_K):
        pltpu.make_async_copy(ys_hbm.at[pl.ds(0, tm)], ybuf.at[slot, k], sem.at[slot]).wait()
    wt = wt_ref[...]
    routed = wt[:, 0:1] * ybuf[slot, 0]
    for k in range(1, TOP_K):
        routed = routed + wt[:, k:k + 1] * ybuf[slot, k]
    f = routed + f

    @pl.when(i < n_ptiles)
    def _():
        yp_ref[...] = _rms(x1_ref[...] + gtp_ref[...] * f) * gf_ref[...]

    @pl.when(i == n_ptiles)
    def _():
        ysm_ref[...] = _rms(x1_ref[...] + gts_ref[...] * f) * gf_ref[...]


def _combine(slot_tab, y_sorted, wts, h2, x1, wgs, wus, wds, gf, gtp, gts, n_prompt, seq, tm=128):
    t, d = h2.shape
    ts = t - n_prompt
    n_tiles = t // tm
    npt = n_prompt // tm
    per = seq // tm
    tab = slot_tab.reshape(n_tiles, 1, tm * TOP_K)
    row = lambda i: (i, 0)
    full = lambda i: (0, 0)
    return pl.pallas_call(
        functools.partial(_combine_kernel, n_tiles, npt, tm),
        grid=(n_tiles,),
        in_specs=[pl.BlockSpec((1, 1, tm * TOP_K), lambda i: (i, 0, 0), memory_space=pltpu.SMEM),
                  pl.BlockSpec((1, 1, tm * TOP_K), lambda i: (jnp.minimum(i + 1, n_tiles - 1), 0, 0),
                               memory_space=pltpu.SMEM),
                  pl.BlockSpec(memory_space=pl.ANY),
                  pl.BlockSpec((tm, LANES), row), pl.BlockSpec((tm, d), row), pl.BlockSpec((tm, d), row),
                  _const_spec(wgs.shape), _const_spec(wus.shape), _const_spec(wds.shape),
                  pl.BlockSpec((1, d), full),
                  pl.BlockSpec((None, 1, d), lambda i: (jnp.minimum(i, npt - 1) // per, 0, 0)),
                  pl.BlockSpec((ts, d), full)],
        out_specs=[pl.BlockSpec((tm, d), lambda i: (jnp.minimum(i, npt - 1), 0)),
                   pl.BlockSpec((ts, d), full)],
        out_shape=[jax.ShapeDtypeStruct((n_prompt, d), F32), jax.ShapeDtypeStruct((ts, d), F32)],
        scratch_shapes=[pltpu.VMEM((2, TOP_K, tm, d), F32), pltpu.SemaphoreType.DMA((2,))],
        compiler_params=_cparams("arbitrary"),
        name="combine",
    )(tab, tab, y_sorted, wts, h2, x1, wgs, wus, wds, gf.reshape(1, d), gtp, gts)


def kernel(x_prompt, x_sample, state_conv, c_prompt, c_sample, w_ada, b_ada, g_norm1, w_in, a_norm_g, a_norm_b, w_s, b_s, conv_w, conv_b, b_norm_g, b_norm_b, w_a_out, w_b_out, w_o, g_norm2, w_router, router_bias, w_gate_e, w_up_e, w_down_e, w_gate_s, w_up_s, w_down_s, g_final):
    depth = w_ada.shape[0]
    assert depth == 1, "single-layer trunk"
    nb, seq, d = x_prompt.shape
    ts, dec_seq, _ = x_sample.shape
    assert dec_seq == 1
    tp = nb * seq
    t = tp + ts
    l = 0

    xp = x_prompt.reshape(tp, d)
    xs = x_sample.reshape(ts, d)

    pad = 16
    c_all = jnp.concatenate([c_prompt, jnp.zeros((pad - nb, d), F32), c_sample], axis=0)
    mod = _ada(c_all, w_ada[l], b_ada[l]).reshape(pad + ts, 6, d)
    mp = [mod[:nb, k].reshape(nb, 1, d) for k in range(6)]
    ms = [mod[pad:, k] for k in range(6)]

    h = _hmod(xp, xs, g_norm1[l], mp[1], mp[0], ms[1], ms[0], seq)
    w_in16 = w_in[l].astype(BF16)
    blk = lambda s: _col_block(w_in16, d, s)
    vec = lambda a: a.reshape(1, d)
    (u,) = _rows_call(_seg_u_kernel, h, ts, 512, [blk(0)], [], [BF16], "seg_u")
    v32, v16 = _rows_call(_seg_v_kernel, h, ts, 512, [blk(1)], [vec(a_norm_g[l]), vec(a_norm_b[l])],
                          [F32, BF16], "seg_v")
    (glu,) = _rows_call(_seg_glu_kernel, h, ts, 512, [blk(2), blk(3)], [], [F32], "seg_glu")
    ga, gb = _rows_call(_seg_gate_kernel, h, ts, 512, [blk(4), blk(5)], [], [BF16, BF16], "seg_gate")

    gdim = d // A_GROUPS
    cb, ng, nbb = vec(conv_b[l]), vec(b_norm_g[l]), vec(b_norm_b[l])
    yap, ybp = _mixp(u, v16, glu, w_s[l], b_s[l].T, conv_w[l], cb, ng, nbb, tp, seq)
    yas, ybs = _mixs(u, v32, glu, state_conv[l], vec(jnp.repeat(w_s[l][:, 0, 0], gdim)),
                     vec(jnp.repeat(b_s[l][:, 0], gdim)), conv_w[l], cb, ng, nbb, tp)

    wr = w_router[l]
    wrh = wr.astype(BF16)
    wrl = (wr - wrh.astype(F32)).astype(BF16)
    x1, h2, logits = _proj(yap, ybp, yas, ybs, ga, gb, xp, xs,
                           w_a_out[l].astype(BF16), w_b_out[l].astype(BF16), w_o[l].astype(BF16),
                           wrh, wrl, g_norm2[l], mp[2], mp[4], mp[3], ms[2], ms[4], ms[3], seq)

    n_blocks = -(-(t * TOP_K) // MOE_BLOCK) + N_EXPERTS
    idx, pos, wts, cnt = _route(logits, router_bias[l])
    slot, be, nv = _slots(cnt, idx, pos, n_blocks)
    slot_tab = slot[:, :TOP_K].reshape(t * TOP_K)
    slot_t = _invert(slot_tab, n_blocks * MOE_BLOCK)
    y_sorted = _experts(be[:, 0], nv[:, 0], slot_t, h2, w_gate_e[l], w_up_e[l], w_down_e[l])
    yp, ysm = _combine(slot_tab, y_sorted, wts, h2, x1,
                       w_gate_s[l].astype(BF16), w_up_s[l].astype(BF16), w_down_s[l].astype(BF16),
                       g_final, mp[5], ms[5], tp, seq)

    n_past = state_conv.shape[2]
    glu_p = glu[:tp].reshape(nb, seq, d)
    conv_p = glu_p[:, seq - n_past:][None]
    conv_s = jnp.concatenate([state_conv[l][:, 1:], glu[tp:][:, None]], axis=1)[None]
    chunkv_p = v32[:tp].reshape(nb, seq, d)[:, seq - CHUNK:][None]
    chunkv_s = v32[tp:].reshape(1, ts, 1, d)
    return (yp.reshape(nb, seq, d), ysm.reshape(ts, 1, d), conv_p, conv_s, chunkv_p, chunkv_s)
```

```python
import functools

import jax
import jax.numpy as jnp
from jax import lax
from jax.experimental import pallas as pl
from jax.experimental.pallas import tpu as pltpu

F32 = jnp.float32
BF16 = jnp.bfloat16
I32 = jnp.int32

EPS = 1e-6
CHUNK = 128
A_GROUPS = 8
CONV_W = 31
N_EXPERTS = 256
TOP_K = 8
N_EXPERT_GROUPS = 8
GROUP_SIZE = N_EXPERTS // N_EXPERT_GROUPS
TOPK_GROUPS = 4
ROUTED_SCALE = 2.5
MOE_BLOCK = 128
LANES = 128
SUBLANES = 8
VMEM_LIMIT = 58 * 1024 * 1024

NEG_INF = float("-inf")


def _cparams(*sem):
    return pltpu.CompilerParams(dimension_semantics=sem, vmem_limit_bytes=VMEM_LIMIT)


def _const_spec(shape):
    nd = len(shape)
    return pl.BlockSpec(shape, lambda *_: (0,) * nd, pipeline_mode=pl.Buffered(1))


def _rms(x):
    return x * lax.rsqrt(jnp.mean(x * x, axis=-1, keepdims=True) + EPS)


def _layer_norm(x, g, b):
    mu = jnp.mean(x, axis=-1, keepdims=True)
    xc = x - mu
    var = jnp.mean(xc * xc, axis=-1, keepdims=True)
    return xc * lax.rsqrt(var + EPS) * g + b


def _bdot(a, b):
    return jnp.dot(a, b, preferred_element_type=F32)


U32 = jnp.uint32
HI_MASK = 0xFFFF0000


def _store_packed(x, ref, n_rows):
    half = x.shape[1] // 2
    assert half == SUBLANES * LANES
    for c in range(SUBLANES):
        lo = x[:, c * LANES:(c + 1) * LANES].astype(BF16).astype(F32)
        hi = x[:, half + c * LANES:half + (c + 1) * LANES].astype(BF16).astype(F32)
        word = (lax.bitcast_convert_type(hi, U32) & U32(HI_MASK)) | (lax.bitcast_convert_type(lo, U32) >> 16)
        ref[pl.ds(c, n_rows, stride=SUBLANES), :] = word


def _load_packed(ref, n_rows):
    lo, hi = [], []
    for c in range(SUBLANES):
        word = ref[pl.ds(c, n_rows, stride=SUBLANES), :]
        lo.append(lax.bitcast_convert_type(word << 16, F32))
        hi.append(lax.bitcast_convert_type(word & U32(HI_MASK), F32))
    return jnp.concatenate(lo + hi, axis=1)


def _lane_sum(x):
    return jnp.sum(x, axis=-1, keepdims=True)


def _lane_max(x):
    return jnp.max(x, axis=-1, keepdims=True)


def _lane_min(x):
    return jnp.min(x, axis=-1, keepdims=True)


def _ada_kernel(c_ref, w_ref, b_ref, o_ref):
    c = c_ref[...]
    s = (c * jax.nn.sigmoid(c)).astype(BF16)
    o_ref[...] = _bdot(s, w_ref[...].astype(BF16)) + b_ref[...]


def _ada(c_all, w_ada, b_ada, tn=1024):
    m, d = c_all.shape
    n = w_ada.shape[1]
    return pl.pallas_call(
        _ada_kernel,
        grid=(n // tn,),
        in_specs=[pl.BlockSpec((m, d), lambda j: (0, 0)),
                  pl.BlockSpec((d, tn), lambda j: (0, j)),
                  pl.BlockSpec((1, tn), lambda j: (0, j))],
        out_specs=pl.BlockSpec((m, tn), lambda j: (0, j)),
        out_shape=jax.ShapeDtypeStruct((m, n), F32),
        compiler_params=_cparams("arbitrary"),
        name="ada",
    )(c_all, w_ada, b_ada.reshape(1, n))


def _hmod_kernel(n_ptiles, n_s, xp_ref, xs_ref, g_ref, scp_ref, shp_ref, scs_ref, shs_ref, o_ref):
    i = pl.program_id(0)

    def body(x, sc, sh):
        return ((_rms(x) * g_ref[...]) * (1.0 + sc) + sh).astype(o_ref.dtype)

    @pl.when(i < n_ptiles)
    def _():
        o_ref[...] = body(xp_ref[...], scp_ref[...], shp_ref[...])

    @pl.when(i == n_ptiles)
    def _():
        o_ref[0:n_s, :] = body(xs_ref[...], scs_ref[...], shs_ref[...])


def _hmod(xp, xs, g, scp, shp, scs, shs, seq, tm=512):
    tp, d = xp.shape
    ts = xs.shape[0]
    npt = tp // tm
    per = seq // tm
    last = npt - 1
    pmod = pl.BlockSpec((None, 1, d), lambda i: (jnp.minimum(i, last) // per, 0, 0))
    full = lambda i: (0, 0)
    return pl.pallas_call(
        functools.partial(_hmod_kernel, npt, ts),
        grid=(npt + 1,),
        in_specs=[pl.BlockSpec((tm, d), lambda i: (jnp.minimum(i, last), 0)),
                  pl.BlockSpec((ts, d), full),
                  pl.BlockSpec((1, d), full),
                  pmod, pmod,
                  pl.BlockSpec((ts, d), full),
                  pl.BlockSpec((ts, d), full)],
        out_specs=pl.BlockSpec((tm, d), lambda i: (i, 0)),
        out_shape=jax.ShapeDtypeStruct((tp + ts, d), BF16),
        compiler_params=_cparams("arbitrary"),
        name="hmod",
    )(xp, xs, g.reshape(1, d), scp, shp, scs, shs)


def _rows_call(kernel, h, n_tail, tm, weights, vecs, out_dtypes, name):
    t, d = h.shape
    n_full = (t - n_tail) // tm

    def wrapped(*refs):
        i = pl.program_id(0)

        @pl.when(i < n_full)
        def _():
            kernel(tm, *refs)

        @pl.when(i == n_full)
        def _():
            kernel(n_tail, *refs)

    in_specs = [pl.BlockSpec((tm, d), lambda i: (i, 0))]
    in_specs += [pl.BlockSpec(bs, im, pipeline_mode=pl.Buffered(1)) for (_, bs, im) in weights]
    in_specs += [pl.BlockSpec(v.shape, lambda i: (0, 0)) for v in vecs]
    return pl.pallas_call(
        wrapped,
        grid=(n_full + 1,),
        in_specs=in_specs,
        out_specs=[pl.BlockSpec((tm, d), lambda i: (i, 0)) for _ in out_dtypes],
        out_shape=[jax.ShapeDtypeStruct((t, d), dt) for dt in out_dtypes],
        compiler_params=_cparams("arbitrary"),
        name=name,
    )(h, *[w for (w, _, _) in weights], *vecs)


def _seg_u_kernel(rows, h_ref, w_ref, o_ref):
    z = _bdot(h_ref[0:rows, :], w_ref[...])
    o_ref[0:rows, :] = jax.nn.gelu(z).astype(o_ref.dtype)


def _seg_v_kernel(rows, h_ref, w_ref, g_ref, b_ref, o32_ref, o16_ref):
    z = _bdot(h_ref[0:rows, :], w_ref[...])
    v = _layer_norm(jax.nn.gelu(z), g_ref[...], b_ref[...])
    o32_ref[0:rows, :] = v
    o16_ref[0:rows, :] = v.astype(o16_ref.dtype)


def _seg_glu_kernel(rows, h_ref, wa_ref, wb_ref, o_ref):
    h = h_ref[0:rows, :]
    o_ref[0:rows, :] = _bdot(h, wa_ref[...]) * jax.nn.sigmoid(_bdot(h, wb_ref[...]))


def _seg_gate_kernel(rows, h_ref, wa_ref, wb_ref, oa_ref, ob_ref):
    h = h_ref[0:rows, :]
    oa_ref[0:rows, :] = jax.nn.sigmoid(_bdot(h, wa_ref[...])).astype(oa_ref.dtype)
    ob_ref[0:rows, :] = jax.nn.sigmoid(_bdot(h, wb_ref[...])).astype(ob_ref.dtype)


def _col_block(w, d, seg):
    return (w, (w.shape[0], d), lambda i, seg=seg: (0, seg))


CONV_ROWS = 128
CONV_LANES = 128


def _mixp_kernel(tm, halo, per, u_ref, v_ref, glu_ref, prev_ref, ws_ref, bst_ref, cw_ref, cb_ref,
                 ng_ref, nb_ref, ya_ref, yb_ref, xp_ref, cv_ref):
    i = pl.program_id(0)
    d = u_ref.shape[1]
    n_chunks = tm // CHUNK
    gdim = d // A_GROUPS

    r_io = lax.broadcasted_iota(I32, (CHUNK, CHUNK), 0)
    c_io = lax.broadcasted_iota(I32, (CHUNK, CHUNK), 1)
    tril = c_io <= r_io
    for g in range(A_GROUPS):
        cs = slice(g * gdim, (g + 1) * gdim)
        wg = jnp.where(tril, ws_ref[g], 0.0).astype(BF16)
        vg = jnp.concatenate([v_ref[c * CHUNK:(c + 1) * CHUNK, cs] for c in range(n_chunks)], axis=1)
        mix = _bdot(wg, vg)
        bcol = bst_ref[:, g:g + 1]
        for c in range(n_chunks):
            rs = slice(c * CHUNK, (c + 1) * CHUNK)
            ya_ref[rs, cs] = (u_ref[rs, cs].astype(F32)
                              * (mix[:, c * gdim:(c + 1) * gdim] + bcol)).astype(ya_ref.dtype)

    xp_ref[0:halo, :] = jnp.where(i % per == 0, 0.0, prev_ref[...])
    xp_ref[halo:halo + tm, :] = glu_ref[...]
    first = halo - (CONV_W - 1)
    n_cb = d // CONV_LANES

    def chunk(idx, carry):
        r0 = pl.multiple_of((idx // n_cb) * CONV_ROWS, CONV_ROWS)
        c0 = pl.multiple_of((idx % n_cb) * CONV_LANES, CONV_LANES)
        acc = jnp.zeros((CONV_ROWS, CONV_LANES), F32)
        n_win = CONV_ROWS + halo
        window = xp_ref[pl.ds(r0, n_win), pl.ds(c0, CONV_LANES)]
        for j in range(SUBLANES):
            taps = [m for m in range(first, first + CONV_W) if m % SUBLANES == j]
            s_j = window if j == 0 else pltpu.roll(window, n_win - j, 0)
            for m in taps:
                w_row = cw_ref[pl.ds(m - first, 1), pl.ds(c0, CONV_LANES)]
                acc = acc + w_row * s_j[m - j:m - j + CONV_ROWS, :]
        cv_ref[pl.ds(r0, CONV_ROWS), pl.ds(c0, CONV_LANES)] = acc
        return carry

    lax.fori_loop(0, (tm // CONV_ROWS) * n_cb, chunk, 0)

    y = _layer_norm(cv_ref[...] + cb_ref[...], ng_ref[...], nb_ref[...])
    yb_ref[...] = (y * jax.nn.sigmoid(y)).astype(yb_ref.dtype)


def _mixp(u, v16, glu, w_s, b_s_t, conv_w, conv_b, ng, nb, n_prompt, seq, tm=512, halo=32):
    d = u.shape[1]
    npt = n_prompt // tm
    per = seq // tm
    hb = tm // halo
    row = lambda i: (i, 0)
    return pl.pallas_call(
        functools.partial(_mixp_kernel, tm, halo, per),
        grid=(npt,),
        in_specs=[pl.BlockSpec((tm, d), row),
                  pl.BlockSpec((tm, d), row),
                  pl.BlockSpec((tm, d), row),
                  pl.BlockSpec((halo, d), lambda i: (jnp.maximum(i * hb - 1, 0), 0)),
                  _const_spec(w_s.shape), _const_spec(b_s_t.shape), _const_spec(conv_w.shape),
                  _const_spec(conv_b.shape), _const_spec(ng.shape), _const_spec(nb.shape)],
        out_specs=[pl.BlockSpec((tm, d), row), pl.BlockSpec((tm, d), row)],
        out_shape=[jax.ShapeDtypeStruct((n_prompt, d), BF16), jax.ShapeDtypeStruct((n_prompt, d), BF16)],
        scratch_shapes=[pltpu.VMEM((halo + tm, d), F32), pltpu.VMEM((tm, d), F32)],
        compiler_params=_cparams("arbitrary"),
        name="mixp",
    )(u, v16, glu, glu, w_s, b_s_t, conv_w, conv_b, ng, nb)


def _mixs_kernel(u_ref, v_ref, glu_ref, st_ref, ws0_ref, bs0_ref, cw_ref, cb_ref, ng_ref, nb_ref,
                 ya_ref, yb_ref):
    n_past = st_ref.shape[1]
    ya_ref[...] = (u_ref[...].astype(F32) * (ws0_ref[...] * v_ref[...] + bs0_ref[...])).astype(ya_ref.dtype)
    acc = cw_ref[n_past:n_past + 1, :] * glu_ref[...] + cb_ref[...]
    acc = acc + jnp.sum(st_ref[...] * cw_ref[0:n_past, :][None], axis=1)
    y = _layer_norm(acc, ng_ref[...], nb_ref[...])
    yb_ref[...] = (y * jax.nn.sigmoid(y)).astype(yb_ref.dtype)


def _mixs(u, v32, glu, state, ws0, bs0, conv_w, conv_b, ng, nb, n_prompt, tb=32):
    ts, n_past, d = state.shape
    off = n_prompt // tb
    tail = lambda j: (off + j, 0)
    vec = lambda j: (0, 0)
    return pl.pallas_call(
        _mixs_kernel,
        grid=(ts // tb,),
        in_specs=[pl.BlockSpec((tb, d), tail), pl.BlockSpec((tb, d), tail), pl.BlockSpec((tb, d), tail),
                  pl.BlockSpec((tb, n_past, d), lambda j: (j, 0, 0)),
                  pl.BlockSpec((1, d), vec), pl.BlockSpec((1, d), vec),
                  pl.BlockSpec(conv_w.shape, vec), pl.BlockSpec((1, d), vec),
                  pl.BlockSpec((1, d), vec), pl.BlockSpec((1, d), vec)],
        out_specs=[pl.BlockSpec((tb, d), lambda j: (j, 0)), pl.BlockSpec((tb, d), lambda j: (j, 0))],
        out_shape=[jax.ShapeDtypeStruct((ts, d), BF16), jax.ShapeDtypeStruct((ts, d), BF16)],
        compiler_params=_cparams("arbitrary"),
        name="mixs",
    )(u, v32, glu, state, ws0, bs0, conv_w, conv_b, ng, nb)


def _proj_kernel(n_ptiles, tm, n_s,
                 yap_ref, ybp_ref, yas_ref, ybs_ref, ga_ref, gb_ref, xp_ref, xs_ref,
                 wa_ref, wb_ref, wo_ref, wrh_ref, wrl_ref, g2_ref,
                 gtp_ref, scp_ref, shp_ref, gts_ref, scs_ref, shs_ref,
                 x1_ref, h2_ref, h2p_ref, lg_ref):
    i = pl.program_id(0)

    def body(rows, ya, yb, x, gate, sc, sh):
        pa = _bdot(ya, wa_ref[...])
        pb = _bdot(yb, wb_ref[...])
        merged = ga_ref[0:rows, :].astype(F32) * pa + gb_ref[0:rows, :].astype(F32) * pb
        out = _bdot(merged.astype(BF16), wo_ref[...])
        x1 = x + gate * out
        h2 = (_rms(x1) * g2_ref[...]) * (1.0 + sc) + sh
        hi = h2.astype(BF16)
        lo = (h2 - hi.astype(F32)).astype(BF16)
        logits = _bdot(hi, wrh_ref[...]) + (_bdot(hi, wrl_ref[...]) + _bdot(lo, wrh_ref[...]))
        x1_ref[0:rows, :] = x1
        h2_ref[0:rows, :] = hi
        _store_packed(h2, h2p_ref, rows)
        lg_ref[0:rows, :] = logits

    @pl.when(i < n_ptiles)
    def _():
        body(tm, yap_ref[...], ybp_ref[...], xp_ref[...], gtp_ref[...], scp_ref[...], shp_ref[...])

    @pl.when(i == n_ptiles)
    def _():
        body(n_s, yas_ref[...], ybs_ref[...], xs_ref[...], gts_ref[...], scs_ref[...], shs_ref[...])


def _proj(yap, ybp, yas, ybs, ga, gb, xp, xs, wa, wb, wo, wrh, wrl, g2,
          gtp, scp, shp, gts, scs, shs, seq, tm=256):
    tp, d = xp.shape
    ts = xs.shape[0]
    t = tp + ts
    ne = wrh.shape[1]
    npt = tp // tm
    per = seq // tm
    last = npt - 1
    prow = lambda i: (jnp.minimum(i, last), 0)
    row = lambda i: (i, 0)
    full = lambda i: (0, 0)
    pmod = pl.BlockSpec((None, 1, d), lambda i: (jnp.minimum(i, last) // per, 0, 0))
    smod = pl.BlockSpec((ts, d), full)
    return pl.pallas_call(
        functools.partial(_proj_kernel, npt, tm, ts),
        grid=(npt + 1,),
        in_specs=[pl.BlockSpec((tm, d), prow), pl.BlockSpec((tm, d), prow),
                  pl.BlockSpec((ts, d), full), pl.BlockSpec((ts, d), full),
                  pl.BlockSpec((tm, d), row), pl.BlockSpec((tm, d), row),
                  pl.BlockSpec((tm, d), prow), pl.BlockSpec((ts, d), full),
                  _const_spec(wa.shape), _const_spec(wb.shape), _const_spec(wo.shape),
                  _const_spec(wrh.shape), _const_spec(wrl.shape), _const_spec((1, d)),
                  pmod, pmod, pmod, smod, smod, smod],
        out_specs=[pl.BlockSpec((tm, d), row), pl.BlockSpec((tm, d), row),
                   pl.BlockSpec((tm * SUBLANES, LANES), row), pl.BlockSpec((tm, ne), row)],
        out_shape=[jax.ShapeDtypeStruct((t, d), F32), jax.ShapeDtypeStruct((t, d), BF16),
                   jax.ShapeDtypeStruct((t * SUBLANES, LANES), U32), jax.ShapeDtypeStruct((t, ne), F32)],
        compiler_params=_cparams("arbitrary"),
        name="proj",
    )(yap, ybp, yas, ybs, ga, gb, xp, xs, wa, wb, wo, wrh, wrl, g2.reshape(1, d),
      gtp, scp, shp, gts, scs, shs)


def _pack_cols(cols, lane):
    out = jnp.zeros(lane.shape, cols[0].dtype)
    for k, c in enumerate(cols):
        out = jnp.where(lane == k, c, out)
    return out


def _route_kernel(lg_ref, bias_ref, tri_ref, idx_ref, pos_ref, wt_ref, cnt_ref, carry_ref):
    i = pl.program_id(0)

    @pl.when(i == 0)
    def _():
        carry_ref[...] = jnp.zeros_like(carry_ref)

    scores = jax.nn.sigmoid(lg_ref[...])
    sel = scores + bias_ref[...]
    lane = lax.broadcasted_iota(I32, sel.shape, 1)
    lane_f = lane.astype(F32)
    grp = jnp.right_shift(lane, GROUP_SIZE.bit_length() - 1)

    gs_cols = []
    gs_full = jnp.zeros(sel.shape, F32)
    for g in range(N_EXPERT_GROUPS):
        in_g = grp == g
        xg = jnp.where(in_g, sel, NEG_INF)
        m1 = _lane_max(xg)
        n1 = _lane_sum(jnp.where(xg == m1, 1.0, 0.0))
        m2 = jnp.where(n1 >= 2.0, m1, _lane_max(jnp.where(xg < m1, xg, NEG_INF)))
        gs = m1 + m2
        gs_cols.append(gs)
        gs_full = jnp.where(in_g, gs, gs_full)

    beaten = jnp.zeros(sel.shape, F32)
    for g in range(N_EXPERT_GROUPS):
        better = (gs_cols[g] > gs_full) | ((gs_cols[g] == gs_full) & (g < grp))
        beaten = beaten + jnp.where(better, 1.0, 0.0)
    cur = jnp.where(beaten < float(TOPK_GROUPS), sel, NEG_INF)

    idx_cols, s_cols, hots = [], [], []
    picked = jnp.zeros(sel.shape, F32)
    for _ in range(TOP_K):
        m = _lane_max(cur)
        idx = _lane_min(jnp.where(cur == m, lane_f, float(N_EXPERTS)))
        hot = lane_f == idx
        idx_cols.append(idx)
        s_cols.append(_lane_sum(jnp.where(hot, scores, 0.0)))
        hots.append(hot)
        picked = jnp.where(hot, 1.0, picked)
        cur = jnp.where(hot, NEG_INF, cur)

    den = s_cols[0]
    for s in s_cols[1:]:
        den = den + s
    w_cols = [s / den * ROUTED_SCALE for s in s_cols]

    pos_full = _bdot(tri_ref[...], picked.astype(BF16)) + carry_ref[...]
    pos_cols = [_lane_sum(jnp.where(h, pos_full, 0.0)) for h in hots]
    carry_ref[...] = carry_ref[...] + jnp.sum(picked, axis=0, keepdims=True)
    cnt_ref[...] = carry_ref[...]

    lane_o = lax.broadcasted_iota(I32, idx_ref.shape, 1)
    idx_ref[...] = _pack_cols(idx_cols, lane_o).astype(I32)
    pos_ref[...] = _pack_cols(pos_cols, lane_o).astype(I32)
    wt_ref[...] = _pack_cols(w_cols, lane_o)


def _route(logits, bias, tm=128):
    t, ne = logits.shape
    tri = jnp.tril(jnp.ones((tm, tm), BF16), -1)
    row = lambda i: (i, 0)
    return pl.pallas_call(
        _route_kernel,
        grid=(t // tm,),
        in_specs=[pl.BlockSpec((tm, ne), row), pl.BlockSpec((1, ne), lambda i: (0, 0)),
                  pl.BlockSpec((tm, tm), lambda i: (0, 0))],
        out_specs=[pl.BlockSpec((tm, LANES), row), pl.BlockSpec((tm, LANES), row),
                   pl.BlockSpec((tm, LANES), row), pl.BlockSpec((1, ne), lambda i: (0, 0))],
        out_shape=[jax.ShapeDtypeStruct((t, LANES), I32), jax.ShapeDtypeStruct((t, LANES), I32),
                   jax.ShapeDtypeStruct((t, LANES), F32), jax.ShapeDtypeStruct((1, ne), F32)],
        scratch_shapes=[pltpu.VMEM((1, ne), F32)],
        compiler_params=_cparams("arbitrary"),
        name="route",
    )(logits, bias.reshape(1, ne), tri)


def _slots_kernel(cnt_ref, idx_ref, pos_ref, triu_ref, slot_ref, bs_ref, bn_ref, ps_ref):
    i = pl.program_id(0)
    ne = cnt_ref.shape[1]

    @pl.when(i == 0)
    def _():
        cnt = cnt_ref[...]
        nblk = jnp.floor((cnt + float(MOE_BLOCK - 1)) * (1.0 / MOE_BLOCK))
        pend = _bdot(jnp.broadcast_to(nblk, (SUBLANES, ne)).astype(BF16), triu_ref[...])[0:1, :]
        pstart = pend - nblk
        ps_ref[...] = pstart
        bs_ref[...] = pstart.astype(I32)
        bn_ref[...] = nblk.astype(I32)

    lane_o = lax.broadcasted_iota(I32, idx_ref.shape, 1)
    lane_e = lax.broadcasted_iota(I32, (idx_ref.shape[0], ne), 1)
    idx = idx_ref[...].astype(F32)
    pos = pos_ref[...].astype(F32)
    pstart = ps_ref[...]
    lane_ef = lane_e.astype(F32)
    cols = []
    for k in range(TOP_K):
        e_k = _lane_sum(jnp.where(lane_o == k, idx, 0.0))
        p_k = _lane_sum(jnp.where(lane_o == k, pos, 0.0))
        ps_k = _lane_sum(jnp.where(lane_ef == e_k, pstart, 0.0))
        cols.append(ps_k * float(MOE_BLOCK) + p_k)
    slot_ref[...] = _pack_cols(cols, lane_o).astype(I32)


def _slots(cnt, idx, pos, tm=128):
    t = idx.shape[0]
    ne = cnt.shape[1]
    triu = jnp.triu(jnp.ones((ne, ne), BF16))
    row = lambda i: (i, 0)
    full = lambda i: (0, 0)
    return pl.pallas_call(
        _slots_kernel,
        grid=(t // tm,),
        in_specs=[pl.BlockSpec((1, ne), full), pl.BlockSpec((tm, LANES), row),
                  pl.BlockSpec((tm, LANES), row), pl.BlockSpec((ne, ne), full)],
        out_specs=[pl.BlockSpec((tm, LANES), row), pl.BlockSpec((1, ne), full),
                   pl.BlockSpec((1, ne), full)],
        out_shape=[jax.ShapeDtypeStruct((t, LANES), I32), jax.ShapeDtypeStruct((1, ne), I32),
                   jax.ShapeDtypeStruct((1, ne), I32)],
        scratch_shapes=[pltpu.VMEM((1, ne), F32)],
        compiler_params=_cparams("arbitrary"),
        name="slots",
    )(cnt, idx, pos, triu)


def _invert_kernel(per_step, slot_ref, fill_hbm, out_ref):
    i = pl.program_id(0)

    @pl.when(i == 0)
    def _():
        pltpu.sync_copy(fill_hbm, out_ref)

    base = i * per_step

    def put(j, c):
        out_ref[slot_ref[j]] = base + j
        return c
    lax.fori_loop(0, per_step, put, 0, unroll=16)


def _invert(slot_flat, n_slots, fill_value, steps=5):
    n = slot_flat.shape[0]
    per = n // steps
    assert per * steps == n and per % 1024 == 0, "rank-1 SMEM blocks must be multiples of 1024"
    return pl.pallas_call(
        functools.partial(_invert_kernel, per),
        grid=(steps,),
        in_specs=[pl.BlockSpec((per,), lambda i: (i,), memory_space=pltpu.SMEM),
                  pl.BlockSpec(memory_space=pl.ANY)],
        out_specs=pl.BlockSpec((n_slots,), lambda i: (0,), memory_space=pltpu.SMEM),
        out_shape=jax.ShapeDtypeStruct((n_slots,), I32),
        compiler_params=_cparams("arbitrary"),
        name="invert",
    )(slot_flat, jnp.full((n_slots,), fill_value, I32))


IDS_RING = 4
TRASH_ROWS = 2 * MOE_BLOCK


def _experts_kernel(n_tab, t_tok, bs_ref, bn_ref, tab_hbm, h_hbm, wg_ref, wu_ref, wd_ref, y_hbm,
                    xbuf, ybuf, wgb, wub, wdb, ids, gsem, ssem, isem):
    e = pl.program_id(0)
    trash = TOP_K * t_tok
    k_shift = TOP_K.bit_length() - 1

    def ids_copy(g):
        q = g & (IDS_RING - 1)
        row = jnp.minimum(g, n_tab - 1)
        return pltpu.make_async_copy(tab_hbm.at[pl.ds(row, 1)], ids.at[pl.ds(q, 1)], isem.at[q])

    def issue_gather(g):
        s, q = g & 1, g & (IDS_RING - 1)
        for r in range(MOE_BLOCK):
            tok = jnp.minimum(jnp.right_shift(ids[q, r], k_shift), t_tok - 1)
            src = h_hbm.at[pl.ds(pl.multiple_of(tok * SUBLANES, SUBLANES), SUBLANES)]
            pltpu.make_async_copy(src, xbuf.at[s, pl.ds(r * SUBLANES, SUBLANES)], gsem.at[s]).start()

    def issue_scatter(g):
        s, q = g & 1, g & (IDS_RING - 1)
        for r in range(MOE_BLOCK):
            a = ids[q, r]
            dst = jnp.where(a >= trash, trash + s * MOE_BLOCK + r,
                            (a & (TOP_K - 1)) * t_tok + jnp.right_shift(a, k_shift))
            out = y_hbm.at[pl.ds(pl.multiple_of(dst * SUBLANES, SUBLANES), SUBLANES)]
            pltpu.make_async_copy(ybuf.at[s, pl.ds(r * SUBLANES, SUBLANES)], out, ssem.at[s]).start()

    blk_rows = MOE_BLOCK * SUBLANES

    def wait_gather(s):
        pltpu.make_async_copy(h_hbm.at[pl.ds(0, blk_rows)], xbuf.at[s], gsem.at[s]).wait()

    def wait_scatter(s):
        pltpu.make_async_copy(ybuf.at[s], y_hbm.at[pl.ds(0, blk_rows)], ssem.at[s]).wait()

    @pl.when(e == 0)
    def _():
        ids_copy(0).start()
        ids_copy(0).wait()
        issue_gather(0)
        ids_copy(1).start()

    @pl.when(bn_ref[e] > 0)
    def _():
        wgb[...] = wg_ref[...].astype(BF16)
        wub[...] = wu_ref[...].astype(BF16)
        wdb[...] = wd_ref[...].astype(BF16)

    def block(j, carry):
        g = bs_ref[e] + j
        s = g & 1

        @pl.when(g >= 2)
        def _():
            wait_scatter(s)

        ids_copy(g + 1).wait()
        issue_gather(g + 1)
        ids_copy(g + 2).start()
        wait_gather(s)
        x = _load_packed(xbuf.at[s], MOE_BLOCK).astype(BF16)
        gate = _bdot(x, wgb[...])
        up = _bdot(x, wub[...])
        act = (gate * jax.nn.sigmoid(gate) * up).astype(BF16)
        _store_packed(_bdot(act, wdb[...]), ybuf.at[s], MOE_BLOCK)
        issue_scatter(g)
        return carry

    lax.fori_loop(0, bn_ref[e], block, 0)

    @pl.when(e == pl.num_programs(0) - 1)
    def _():
        n_used = bs_ref[e] + bn_ref[e]
        wait_gather(n_used & 1)
        ids_copy(n_used + 1).wait()
        wait_scatter(0)
        wait_scatter(1)
        ybuf[...] = jnp.zeros(ybuf.shape, U32)
        for s in range(2):
            tail = pltpu.make_async_copy(
                ybuf.at[s], y_hbm.at[pl.ds((trash + s * MOE_BLOCK) * SUBLANES, blk_rows)], ssem.at[s])
            tail.start()
            tail.wait()


def _experts(blk_start, blk_count, table, h2p, wg, wu, wd):
    n_tab = table.shape[0]
    t = h2p.shape[0] // SUBLANES
    ne, d, de = wg.shape
    blk_rows = MOE_BLOCK * SUBLANES
    grid_spec = pltpu.PrefetchScalarGridSpec(
        num_scalar_prefetch=2,
        grid=(ne,),
        in_specs=[
            pl.BlockSpec(memory_space=pl.ANY),
            pl.BlockSpec(memory_space=pl.ANY),
            pl.BlockSpec((None, d, de), lambda e, bs, bn: (e, 0, 0)),
            pl.BlockSpec((None, d, de), lambda e, bs, bn: (e, 0, 0)),
            pl.BlockSpec((None, de, d), lambda e, bs, bn: (e, 0, 0)),
        ],
        out_specs=pl.BlockSpec(memory_space=pl.ANY),
        scratch_shapes=[pltpu.VMEM((2, blk_rows, LANES), U32), pltpu.VMEM((2, blk_rows, LANES), U32),
                        pltpu.VMEM((d, de), BF16), pltpu.VMEM((d, de), BF16), pltpu.VMEM((de, d), BF16),
                        pltpu.SMEM((IDS_RING, MOE_BLOCK), I32),
                        pltpu.SemaphoreType.DMA((2,)), pltpu.SemaphoreType.DMA((2,)),
                        pltpu.SemaphoreType.DMA((IDS_RING,))],
    )
    return pl.pallas_call(
        functools.partial(_experts_kernel, n_tab, t),
        grid_spec=grid_spec,
        out_shape=jax.ShapeDtypeStruct(((TOP_K * t + TRASH_ROWS) * SUBLANES, LANES), U32),
        compiler_params=_cparams("arbitrary"),
        name="experts",
    )(blk_start, blk_count, table, h2p, wg, wu, wd)


def _combine_kernel(n_ptiles, *refs):
    y_refs = refs[:TOP_K]
    (wt_ref, h2_ref, x1_ref, wgs_ref, wus_ref, wds_ref, gf_ref, gtp_ref, gts_ref,
     yp_ref, ysm_ref) = refs[TOP_K:]
    i = pl.program_id(0)

    tm = h2_ref.shape[0]
    h = h2_ref[...]
    g = _bdot(h, wgs_ref[...])
    u = _bdot(h, wus_ref[...])
    f = _bdot((g * jax.nn.sigmoid(g) * u).astype(BF16), wds_ref[...])

    wt = wt_ref[...]
    routed = wt[:, 0:1] * _load_packed(y_refs[0], tm)
    for k in range(1, TOP_K):
        routed = routed + wt[:, k:k + 1] * _load_packed(y_refs[k], tm)
    f = routed + f

    @pl.when(i < n_ptiles)
    def _():
        yp_ref[...] = _rms(x1_ref[...] + gtp_ref[...] * f) * gf_ref[...]

    @pl.when(i == n_ptiles)
    def _():
        ysm_ref[...] = _rms(x1_ref[...] + gts_ref[...] * f) * gf_ref[...]


def _combine(y_planes, wts, h2, x1, wgs, wus, wds, gf, gtp, gts, n_prompt, seq, tm=128):
    t, d = h2.shape
    ts = t - n_prompt
    n_tiles = t // tm
    npt = n_prompt // tm
    per = seq // tm
    row = lambda i: (i, 0)
    full = lambda i: (0, 0)
    return pl.pallas_call(
        functools.partial(_combine_kernel, npt),
        grid=(n_tiles,),
        in_specs=[pl.BlockSpec((tm * SUBLANES, LANES), lambda i, k=k: (k * n_tiles + i, 0))
                  for k in range(TOP_K)] + [
                  pl.BlockSpec((tm, LANES), row), pl.BlockSpec((tm, d), row), pl.BlockSpec((tm, d), row),
                  _const_spec(wgs.shape), _const_spec(wus.shape), _const_spec(wds.shape),
                  pl.BlockSpec((1, d), full),
                  pl.BlockSpec((None, 1, d), lambda i: (jnp.minimum(i, npt - 1) // per, 0, 0)),
                  pl.BlockSpec((ts, d), full)],
        out_specs=[pl.BlockSpec((tm, d), lambda i: (jnp.minimum(i, npt - 1), 0)),
                   pl.BlockSpec((ts, d), full)],
        out_shape=[jax.ShapeDtypeStruct((n_prompt, d), F32), jax.ShapeDtypeStruct((ts, d), F32)],
        compiler_params=_cparams("arbitrary"),
        name="combine",
    )(*([y_planes] * TOP_K), wts, h2, x1, wgs, wus, wds, gf.reshape(1, d), gtp, gts)


def kernel(x_prompt, x_sample, state_conv, c_prompt, c_sample, w_ada, b_ada, g_norm1, w_in, a_norm_g, a_norm_b, w_s, b_s, conv_w, conv_b, b_norm_g, b_norm_b, w_a_out, w_b_out, w_o, g_norm2, w_router, router_bias, w_gate_e, w_up_e, w_down_e, w_gate_s, w_up_s, w_down_s, g_final):
    depth = w_ada.shape[0]
    assert depth == 1, "single-layer trunk"
    nb, seq, d = x_prompt.shape
    ts, dec_seq, _ = x_sample.shape
    assert dec_seq == 1
    tp = nb * seq
    t = tp + ts
    l = 0

    xp = x_prompt.reshape(tp, d)
    xs = x_sample.reshape(ts, d)

    pad = 16
    c_all = jnp.concatenate([c_prompt, jnp.zeros((pad - nb, d), F32), c_sample], axis=0)
    mod = _ada(c_all, w_ada[l], b_ada[l]).reshape(pad + ts, 6, d)
    mp = [mod[:nb, k].reshape(nb, 1, d) for k in range(6)]
    ms = [mod[pad:, k] for k in range(6)]

    h = _hmod(xp, xs, g_norm1[l], mp[1], mp[0], ms[1], ms[0], seq)
    w_in16 = w_in[l].astype(BF16)
    blk = lambda s: _col_block(w_in16, d, s)
    vec = lambda a: a.reshape(1, d)
    (u,) = _rows_call(_seg_u_kernel, h, ts, 512, [blk(0)], [], [BF16], "seg_u")
    v32, v16 = _rows_call(_seg_v_kernel, h, ts, 512, [blk(1)], [vec(a_norm_g[l]), vec(a_norm_b[l])],
                          [F32, BF16], "seg_v")
    (glu,) = _rows_call(_seg_glu_kernel, h, ts, 512, [blk(2), blk(3)], [], [F32], "seg_glu")
    ga, gb = _rows_call(_seg_gate_kernel, h, ts, 512, [blk(4), blk(5)], [], [BF16, BF16], "seg_gate")

    gdim = d // A_GROUPS
    cb, ng, nbb = vec(conv_b[l]), vec(b_norm_g[l]), vec(b_norm_b[l])
    yap, ybp = _mixp(u, v16, glu, w_s[l], b_s[l].T, conv_w[l], cb, ng, nbb, tp, seq)
    yas, ybs = _mixs(u, v32, glu, state_conv[l], vec(jnp.repeat(w_s[l][:, 0, 0], gdim)),
                     vec(jnp.repeat(b_s[l][:, 0], gdim)), conv_w[l], cb, ng, nbb, tp)

    wr = w_router[l]
    wrh = wr.astype(BF16)
    wrl = (wr - wrh.astype(F32)).astype(BF16)
    x1, h2, h2p, logits = _proj(yap, ybp, yas, ybs, ga, gb, xp, xs,
                           w_a_out[l].astype(BF16), w_b_out[l].astype(BF16), w_o[l].astype(BF16),
                           wrh, wrl, g_norm2[l], mp[2], mp[4], mp[3], ms[2], ms[4], ms[3], seq)

    n_blocks = -(-(t * TOP_K) // MOE_BLOCK) + N_EXPERTS
    idx, pos, wts, cnt = _route(logits, router_bias[l])
    slot, blk_start, blk_count = _slots(cnt, idx, pos)
    slot_tab = slot[:, :TOP_K].reshape(t * TOP_K)
    table = _invert(slot_tab, n_blocks * MOE_BLOCK, t * TOP_K).reshape(n_blocks, MOE_BLOCK)
    y_planes = _experts(blk_start[0], blk_count[0], table, h2p, w_gate_e[l], w_up_e[l], w_down_e[l])
    yp, ysm = _combine(y_planes, wts, h2, x1,
                       w_gate_s[l].astype(BF16), w_up_s[l].astype(BF16), w_down_s[l].astype(BF16),
                       g_final, mp[5], ms[5], tp, seq)

    n_past = state_conv.shape[2]
    glu_p = glu[:tp].reshape(nb, seq, d)
    conv_p = glu_p[:, seq - n_past:][None]
    conv_s = jnp.concatenate([state_conv[l][:, 1:], glu[tp:][:, None]], axis=1)[None]
    chunkv_p = v32[:tp].reshape(nb, seq, d)[:, seq - CHUNK:][None]
    chunkv_s = v32[tp:].reshape(1, ts, 1, d)
    return (yp.reshape(nb, seq, d), ysm.reshape(ts, 1, d), conv_p, conv_s, chunkv_p, chunkv_s)
```

```python
import functools

import jax
import jax.numpy as jnp
from jax import lax
from jax.experimental import pallas as pl
from jax.experimental.pallas import tpu as pltpu

F32 = jnp.float32
BF16 = jnp.bfloat16
I32 = jnp.int32

EPS = 1e-6
CHUNK = 128
A_GROUPS = 8
CONV_W = 31
N_EXPERTS = 256
TOP_K = 8
N_EXPERT_GROUPS = 8
GROUP_SIZE = N_EXPERTS // N_EXPERT_GROUPS
TOPK_GROUPS = 4
ROUTED_SCALE = 2.5
MOE_BLOCK = 128
LANES = 128
SUBLANES = 8
VMEM_LIMIT = 58 * 1024 * 1024

NEG_INF = float("-inf")


def _cparams(*sem):
    return pltpu.CompilerParams(dimension_semantics=sem, vmem_limit_bytes=VMEM_LIMIT)


def _const_spec(shape):
    nd = len(shape)
    return pl.BlockSpec(shape, lambda *_: (0,) * nd, pipeline_mode=pl.Buffered(1))


def _rms(x):
    return x * lax.rsqrt(jnp.mean(x * x, axis=-1, keepdims=True) + EPS)


def _layer_norm(x, g, b):
    mu = jnp.mean(x, axis=-1, keepdims=True)
    xc = x - mu
    var = jnp.mean(xc * xc, axis=-1, keepdims=True)
    return xc * lax.rsqrt(var + EPS) * g + b


def _bdot(a, b):
    return jnp.dot(a, b, preferred_element_type=F32)


U32 = jnp.uint32
HI_MASK = 0xFFFF0000


def _store_packed_cols(x_lo, x_hi, ref, c0, n_rows):
    for i in range(x_lo.shape[1] // LANES):
        lo = x_lo[:, i * LANES:(i + 1) * LANES].astype(BF16).astype(F32)
        hi = x_hi[:, i * LANES:(i + 1) * LANES].astype(BF16).astype(F32)
        word = (lax.bitcast_convert_type(hi, U32) & U32(HI_MASK)) | (lax.bitcast_convert_type(lo, U32) >> 16)
        ref[pl.ds(c0 + i, n_rows, stride=SUBLANES), :] = word


def _store_packed(x, ref, n_rows):
    half = x.shape[1] // 2
    assert half == SUBLANES * LANES
    _store_packed_cols(x[:, :half], x[:, half:], ref, 0, n_rows)


def _load_packed(ref, n_rows):
    lo, hi = [], []
    for c in range(SUBLANES):
        word = ref[pl.ds(c, n_rows, stride=SUBLANES), :]
        lo.append(lax.bitcast_convert_type(word << 16, F32))
        hi.append(lax.bitcast_convert_type(word & U32(HI_MASK), F32))
    return jnp.concatenate(lo + hi, axis=1)


def _lane_sum(x):
    return jnp.sum(x, axis=-1, keepdims=True)


def _lane_max(x):
    return jnp.max(x, axis=-1, keepdims=True)


def _lane_min(x):
    return jnp.min(x, axis=-1, keepdims=True)


def _ada_kernel(c_ref, w_ref, b_ref, o_ref):
    c = c_ref[...]
    s = (c * jax.nn.sigmoid(c)).astype(BF16)
    o_ref[...] = _bdot(s, w_ref[...].astype(BF16)) + b_ref[...]


def _ada(c_all, w_ada, b_ada, tn=1024):
    m, d = c_all.shape
    n = w_ada.shape[1]
    return pl.pallas_call(
        _ada_kernel,
        grid=(n // tn,),
        in_specs=[pl.BlockSpec((m, d), lambda j: (0, 0)),
                  pl.BlockSpec((d, tn), lambda j: (0, j)),
                  pl.BlockSpec((1, tn), lambda j: (0, j))],
        out_specs=pl.BlockSpec((m, tn), lambda j: (0, j)),
        out_shape=jax.ShapeDtypeStruct((m, n), F32),
        compiler_params=_cparams("arbitrary"),
        name="ada",
    )(c_all, w_ada, b_ada.reshape(1, n))


def _hmod_kernel(n_ptiles, n_s, xp_ref, xs_ref, g_ref, scp_ref, shp_ref, scs_ref, shs_ref, o_ref):
    i = pl.program_id(0)

    def body(x, sc, sh):
        return ((_rms(x) * g_ref[...]) * (1.0 + sc) + sh).astype(o_ref.dtype)

    @pl.when(i < n_ptiles)
    def _():
        o_ref[...] = body(xp_ref[...], scp_ref[...], shp_ref[...])

    @pl.when(i == n_ptiles)
    def _():
        o_ref[0:n_s, :] = body(xs_ref[...], scs_ref[...], shs_ref[...])


def _hmod(xp, xs, g, scp, shp, scs, shs, seq, tm=512):
    tp, d = xp.shape
    ts = xs.shape[0]
    npt = tp // tm
    per = seq // tm
    last = npt - 1
    pmod = pl.BlockSpec((None, 1, d), lambda i: (jnp.minimum(i, last) // per, 0, 0))
    full = lambda i: (0, 0)
    return pl.pallas_call(
        functools.partial(_hmod_kernel, npt, ts),
        grid=(npt + 1,),
        in_specs=[pl.BlockSpec((tm, d), lambda i: (jnp.minimum(i, last), 0)),
                  pl.BlockSpec((ts, d), full),
                  pl.BlockSpec((1, d), full),
                  pmod, pmod,
                  pl.BlockSpec((ts, d), full),
                  pl.BlockSpec((ts, d), full)],
        out_specs=pl.BlockSpec((tm, d), lambda i: (i, 0)),
        out_shape=jax.ShapeDtypeStruct((tp + ts, d), BF16),
        compiler_params=_cparams("arbitrary"),
        name="hmod",
    )(xp, xs, g.reshape(1, d), scp, shp, scs, shs)


def _rows_call(kernel, h, n_tail, tm, weights, vecs, out_dtypes, name):
    t, d = h.shape
    n_full = (t - n_tail) // tm

    def wrapped(*refs):
        i = pl.program_id(0)

        @pl.when(i < n_full)
        def _():
            kernel(tm, *refs)

        @pl.when(i == n_full)
        def _():
            kernel(n_tail, *refs)

    in_specs = [pl.BlockSpec((tm, d), lambda i: (i, 0))]
    in_specs += [pl.BlockSpec(bs, im, pipeline_mode=pl.Buffered(1)) for (_, bs, im) in weights]
    in_specs += [pl.BlockSpec(v.shape, lambda i: (0, 0)) for v in vecs]
    return pl.pallas_call(
        wrapped,
        grid=(n_full + 1,),
        in_specs=in_specs,
        out_specs=[pl.BlockSpec((tm, d), lambda i: (i, 0)) for _ in out_dtypes],
        out_shape=[jax.ShapeDtypeStruct((t, d), dt) for dt in out_dtypes],
        compiler_params=_cparams("arbitrary"),
        name=name,
    )(h, *[w for (w, _, _) in weights], *vecs)


def _seg_u_kernel(rows, h_ref, w_ref, o_ref):
    z = _bdot(h_ref[0:rows, :], w_ref[...])
    o_ref[0:rows, :] = jax.nn.gelu(z).astype(o_ref.dtype)


def _seg_v_kernel(rows, h_ref, w_ref, g_ref, b_ref, o32_ref, o16_ref):
    z = _bdot(h_ref[0:rows, :], w_ref[...])
    v = _layer_norm(jax.nn.gelu(z), g_ref[...], b_ref[...])
    o32_ref[0:rows, :] = v
    o16_ref[0:rows, :] = v.astype(o16_ref.dtype)


def _seg_glu_kernel(rows, h_ref, wa_ref, wb_ref, o_ref):
    h = h_ref[0:rows, :]
    o_ref[0:rows, :] = _bdot(h, wa_ref[...]) * jax.nn.sigmoid(_bdot(h, wb_ref[...]))


def _seg_gate_kernel(rows, h_ref, wa_ref, wb_ref, oa_ref, ob_ref):
    h = h_ref[0:rows, :]
    oa_ref[0:rows, :] = jax.nn.sigmoid(_bdot(h, wa_ref[...])).astype(oa_ref.dtype)
    ob_ref[0:rows, :] = jax.nn.sigmoid(_bdot(h, wb_ref[...])).astype(ob_ref.dtype)


def _col_block(w, d, seg):
    return (w, (w.shape[0], d), lambda i, seg=seg: (0, seg))


CONV_ROWS = 128
CONV_LANES = 128


def _mixp_kernel(tm, halo, per, u_ref, v_ref, glu_ref, prev_ref, ws_ref, bst_ref, cw_ref, cb_ref,
                 ng_ref, nb_ref, ya_ref, yb_ref, xp_ref, cv_ref):
    i = pl.program_id(0)
    d = u_ref.shape[1]
    n_chunks = tm // CHUNK
    gdim = d // A_GROUPS

    r_io = lax.broadcasted_iota(I32, (CHUNK, CHUNK), 0)
    c_io = lax.broadcasted_iota(I32, (CHUNK, CHUNK), 1)
    tril = c_io <= r_io
    for g in range(A_GROUPS):
        cs = slice(g * gdim, (g + 1) * gdim)
        wg = jnp.where(tril, ws_ref[g], 0.0).astype(BF16)
        vg = jnp.concatenate([v_ref[c * CHUNK:(c + 1) * CHUNK, cs] for c in range(n_chunks)], axis=1)
        mix = _bdot(wg, vg)
        bcol = bst_ref[:, g:g + 1]
        for c in range(n_chunks):
            rs = slice(c * CHUNK, (c + 1) * CHUNK)
            ya_ref[rs, cs] = (u_ref[rs, cs].astype(F32)
                              * (mix[:, c * gdim:(c + 1) * gdim] + bcol)).astype(ya_ref.dtype)

    xp_ref[0:halo, :] = jnp.where(i % per == 0, 0.0, prev_ref[...])
    xp_ref[halo:halo + tm, :] = glu_ref[...]
    first = halo - (CONV_W - 1)
    n_cb = d // CONV_LANES

    def chunk(idx, carry):
        r0 = pl.multiple_of((idx // n_cb) * CONV_ROWS, CONV_ROWS)
        c0 = pl.multiple_of((idx % n_cb) * CONV_LANES, CONV_LANES)
        acc = jnp.zeros((CONV_ROWS, CONV_LANES), F32)
        n_win = CONV_ROWS + halo
        window = xp_ref[pl.ds(r0, n_win), pl.ds(c0, CONV_LANES)]
        for j in range(SUBLANES):
            taps = [m for m in range(first, first + CONV_W) if m % SUBLANES == j]
            s_j = window if j == 0 else pltpu.roll(window, n_win - j, 0)
            for m in taps:
                w_row = cw_ref[pl.ds(m - first, 1), pl.ds(c0, CONV_LANES)]
                acc = acc + w_row * s_j[m - j:m - j + CONV_ROWS, :]
        cv_ref[pl.ds(r0, CONV_ROWS), pl.ds(c0, CONV_LANES)] = acc
        return carry

    lax.fori_loop(0, (tm // CONV_ROWS) * n_cb, chunk, 0)

    y = _layer_norm(cv_ref[...] + cb_ref[...], ng_ref[...], nb_ref[...])
    yb_ref[...] = (y * jax.nn.sigmoid(y)).astype(yb_ref.dtype)


def _mixp(u, v16, glu, w_s, b_s_t, conv_w, conv_b, ng, nb, n_prompt, seq, tm=512, halo=32):
    d = u.shape[1]
    npt = n_prompt // tm
    per = seq // tm
    hb = tm // halo
    row = lambda i: (i, 0)
    return pl.pallas_call(
        functools.partial(_mixp_kernel, tm, halo, per),
        grid=(npt,),
        in_specs=[pl.BlockSpec((tm, d), row),
                  pl.BlockSpec((tm, d), row),
                  pl.BlockSpec((tm, d), row),
                  pl.BlockSpec((halo, d), lambda i: (jnp.maximum(i * hb - 1, 0), 0)),
                  _const_spec(w_s.shape), _const_spec(b_s_t.shape), _const_spec(conv_w.shape),
                  _const_spec(conv_b.shape), _const_spec(ng.shape), _const_spec(nb.shape)],
        out_specs=[pl.BlockSpec((tm, d), row), pl.BlockSpec((tm, d), row)],
        out_shape=[jax.ShapeDtypeStruct((n_prompt, d), BF16), jax.ShapeDtypeStruct((n_prompt, d), BF16)],
        scratch_shapes=[pltpu.VMEM((halo + tm, d), F32), pltpu.VMEM((tm, d), F32)],
        compiler_params=_cparams("arbitrary"),
        name="mixp",
    )(u, v16, glu, glu, w_s, b_s_t, conv_w, conv_b, ng, nb)


def _mixs_kernel(u_ref, v_ref, glu_ref, st_ref, ws0_ref, bs0_ref, cw_ref, cb_ref, ng_ref, nb_ref,
                 ya_ref, yb_ref):
    n_past = st_ref.shape[1]
    ya_ref[...] = (u_ref[...].astype(F32) * (ws0_ref[...] * v_ref[...] + bs0_ref[...])).astype(ya_ref.dtype)
    acc = cw_ref[n_past:n_past + 1, :] * glu_ref[...] + cb_ref[...]
    acc = acc + jnp.sum(st_ref[...] * cw_ref[0:n_past, :][None], axis=1)
    y = _layer_norm(acc, ng_ref[...], nb_ref[...])
    yb_ref[...] = (y * jax.nn.sigmoid(y)).astype(yb_ref.dtype)


def _mixs(u, v32, glu, state, ws0, bs0, conv_w, conv_b, ng, nb, n_prompt, tb=32):
    ts, n_past, d = state.shape
    off = n_prompt // tb
    tail = lambda j: (off + j, 0)
    vec = lambda j: (0, 0)
    return pl.pallas_call(
        _mixs_kernel,
        grid=(ts // tb,),
        in_specs=[pl.BlockSpec((tb, d), tail), pl.BlockSpec((tb, d), tail), pl.BlockSpec((tb, d), tail),
                  pl.BlockSpec((tb, n_past, d), lambda j: (j, 0, 0)),
                  pl.BlockSpec((1, d), vec), pl.BlockSpec((1, d), vec),
                  pl.BlockSpec(conv_w.shape, vec), pl.BlockSpec((1, d), vec),
                  pl.BlockSpec((1, d), vec), pl.BlockSpec((1, d), vec)],
        out_specs=[pl.BlockSpec((tb, d), lambda j: (j, 0)), pl.BlockSpec((tb, d), lambda j: (j, 0))],
        out_shape=[jax.ShapeDtypeStruct((ts, d), BF16), jax.ShapeDtypeStruct((ts, d), BF16)],
        compiler_params=_cparams("arbitrary"),
        name="mixs",
    )(u, v32, glu, state, ws0, bs0, conv_w, conv_b, ng, nb)


def _proj_kernel(n_ptiles, tm, n_s,
                 yap_ref, ybp_ref, yas_ref, ybs_ref, ga_ref, gb_ref, xp_ref, xs_ref,
                 wa_ref, wb_ref, wo_ref, wrh_ref, wrl_ref, g2_ref,
                 gtp_ref, scp_ref, shp_ref, gts_ref, scs_ref, shs_ref,
                 x1_ref, h2_ref, h2p_ref, lg_ref):
    i = pl.program_id(0)

    def body(rows, ya, yb, x, gate, sc, sh):
        pa = _bdot(ya, wa_ref[...])
        pb = _bdot(yb, wb_ref[...])
        merged = ga_ref[0:rows, :].astype(F32) * pa + gb_ref[0:rows, :].astype(F32) * pb
        out = _bdot(merged.astype(BF16), wo_ref[...])
        x1 = x + gate * out
        h2 = (_rms(x1) * g2_ref[...]) * (1.0 + sc) + sh
        hi = h2.astype(BF16)
        lo = (h2 - hi.astype(F32)).astype(BF16)
        logits = _bdot(hi, wrh_ref[...]) + (_bdot(hi, wrl_ref[...]) + _bdot(lo, wrh_ref[...]))
        x1_ref[0:rows, :] = x1
        h2_ref[0:rows, :] = hi
        _store_packed(h2, h2p_ref, rows)
        lg_ref[0:rows, :] = logits

    @pl.when(i < n_ptiles)
    def _():
        body(tm, yap_ref[...], ybp_ref[...], xp_ref[...], gtp_ref[...], scp_ref[...], shp_ref[...])

    @pl.when(i == n_ptiles)
    def _():
        body(n_s, yas_ref[...], ybs_ref[...], xs_ref[...], gts_ref[...], scs_ref[...], shs_ref[...])


def _proj(yap, ybp, yas, ybs, ga, gb, xp, xs, wa, wb, wo, wrh, wrl, g2,
          gtp, scp, shp, gts, scs, shs, seq, tm=256):
    tp, d = xp.shape
    ts = xs.shape[0]
    t = tp + ts
    ne = wrh.shape[1]
    npt = tp // tm
    per = seq // tm
    last = npt - 1
    prow = lambda i: (jnp.minimum(i, last), 0)
    row = lambda i: (i, 0)
    full = lambda i: (0, 0)
    pmod = pl.BlockSpec((None, 1, d), lambda i: (jnp.minimum(i, last) // per, 0, 0))
    smod = pl.BlockSpec((ts, d), full)
    return pl.pallas_call(
        functools.partial(_proj_kernel, npt, tm, ts),
        grid=(npt + 1,),
        in_specs=[pl.BlockSpec((tm, d), prow), pl.BlockSpec((tm, d), prow),
                  pl.BlockSpec((ts, d), full), pl.BlockSpec((ts, d), full),
                  pl.BlockSpec((tm, d), row), pl.BlockSpec((tm, d), row),
                  pl.BlockSpec((tm, d), prow), pl.BlockSpec((ts, d), full),
                  _const_spec(wa.shape), _const_spec(wb.shape), _const_spec(wo.shape),
                  _const_spec(wrh.shape), _const_spec(wrl.shape), _const_spec((1, d)),
                  pmod, pmod, pmod, smod, smod, smod],
        out_specs=[pl.BlockSpec((tm, d), row), pl.BlockSpec((tm, d), row),
                   pl.BlockSpec((tm * SUBLANES, LANES), row), pl.BlockSpec((tm, ne), row)],
        out_shape=[jax.ShapeDtypeStruct((t, d), F32), jax.ShapeDtypeStruct((t, d), BF16),
                   jax.ShapeDtypeStruct((t * SUBLANES, LANES), U32), jax.ShapeDtypeStruct((t, ne), F32)],
        compiler_params=_cparams("arbitrary"),
        name="proj",
    )(yap, ybp, yas, ybs, ga, gb, xp, xs, wa, wb, wo, wrh, wrl, g2.reshape(1, d),
      gtp, scp, shp, gts, scs, shs)


def _pack_cols(cols, lane):
    out = jnp.zeros(lane.shape, cols[0].dtype)
    for k, c in enumerate(cols):
        out = jnp.where(lane == k, c, out)
    return out


def _route_kernel(lg_ref, bias_ref, tri_ref, idx_ref, pos_ref, wt_ref, cnt_ref, carry_ref):
    i = pl.program_id(0)

    @pl.when(i == 0)
    def _():
        carry_ref[...] = jnp.zeros_like(carry_ref)

    scores = jax.nn.sigmoid(lg_ref[...])
    sel = scores + bias_ref[...]
    lane = lax.broadcasted_iota(I32, sel.shape, 1)
    lane_f = lane.astype(F32)
    grp = jnp.right_shift(lane, GROUP_SIZE.bit_length() - 1)

    gs_cols = []
    gs_full = jnp.zeros(sel.shape, F32)
    for g in range(N_EXPERT_GROUPS):
        in_g = grp == g
        xg = jnp.where(in_g, sel, NEG_INF)
        m1 = _lane_max(xg)
        n1 = _lane_sum(jnp.where(xg == m1, 1.0, 0.0))
        m2 = jnp.where(n1 >= 2.0, m1, _lane_max(jnp.where(xg < m1, xg, NEG_INF)))
        gs = m1 + m2
        gs_cols.append(gs)
        gs_full = jnp.where(in_g, gs, gs_full)

    beaten = jnp.zeros(sel.shape, F32)
    for g in range(N_EXPERT_GROUPS):
        better = (gs_cols[g] > gs_full) | ((gs_cols[g] == gs_full) & (g < grp))
        beaten = beaten + jnp.where(better, 1.0, 0.0)
    cur = jnp.where(beaten < float(TOPK_GROUPS), sel, NEG_INF)

    idx_cols, s_cols, hots = [], [], []
    picked = jnp.zeros(sel.shape, F32)
    for _ in range(TOP_K):
        m = _lane_max(cur)
        idx = _lane_min(jnp.where(cur == m, lane_f, float(N_EXPERTS)))
        hot = lane_f == idx
        idx_cols.append(idx)
        s_cols.append(_lane_sum(jnp.where(hot, scores, 0.0)))
        hots.append(hot)
        picked = jnp.where(hot, 1.0, picked)
        cur = jnp.where(hot, NEG_INF, cur)

    den = s_cols[0]
    for s in s_cols[1:]:
        den = den + s
    w_cols = [s / den * ROUTED_SCALE for s in s_cols]

    pos_full = _bdot(tri_ref[...], picked.astype(BF16)) + carry_ref[...]
    pos_cols = [_lane_sum(jnp.where(h, pos_full, 0.0)) for h in hots]
    carry_ref[...] = carry_ref[...] + jnp.sum(picked, axis=0, keepdims=True)
    cnt_ref[...] = carry_ref[...]

    lane_o = lax.broadcasted_iota(I32, idx_ref.shape, 1)
    idx_ref[...] = _pack_cols(idx_cols, lane_o).astype(I32)
    pos_ref[...] = _pack_cols(pos_cols, lane_o).astype(I32)
    wt_ref[...] = _pack_cols(w_cols, lane_o)


def _route(logits, bias, tm=640):
    t, ne = logits.shape
    tri = jnp.tril(jnp.ones((tm, tm), BF16), -1)
    row = lambda i: (i, 0)
    return pl.pallas_call(
        _route_kernel,
        grid=(t // tm,),
        in_specs=[pl.BlockSpec((tm, ne), row), pl.BlockSpec((1, ne), lambda i: (0, 0)),
                  pl.BlockSpec((tm, tm), lambda i: (0, 0))],
        out_specs=[pl.BlockSpec((tm, LANES), row), pl.BlockSpec((tm, LANES), row),
                   pl.BlockSpec((tm, LANES), row), pl.BlockSpec((1, ne), lambda i: (0, 0))],
        out_shape=[jax.ShapeDtypeStruct((t, LANES), I32), jax.ShapeDtypeStruct((t, LANES), I32),
                   jax.ShapeDtypeStruct((t, LANES), F32), jax.ShapeDtypeStruct((1, ne), F32)],
        scratch_shapes=[pltpu.VMEM((1, ne), F32)],
        compiler_params=_cparams("arbitrary"),
        name="route",
    )(logits, bias.reshape(1, ne), tri)


def _slots_kernel(cnt_ref, idx_ref, pos_ref, triu_ref, slot_ref, bs_ref, bn_ref, ps_ref):
    i = pl.program_id(0)
    ne = cnt_ref.shape[1]

    @pl.when(i == 0)
    def _():
        cnt = cnt_ref[...]
        nblk = jnp.floor((cnt + float(MOE_BLOCK - 1)) * (1.0 / MOE_BLOCK))
        pend = _bdot(jnp.broadcast_to(nblk, (SUBLANES, ne)).astype(BF16), triu_ref[...])[0:1, :]
        pstart = pend - nblk
        ps_ref[...] = pstart
        bs_ref[...] = pstart.astype(I32)
        bn_ref[...] = nblk.astype(I32)

    lane_o = lax.broadcasted_iota(I32, idx_ref.shape, 1)
    lane_e = lax.broadcasted_iota(I32, (idx_ref.shape[0], ne), 1)
    idx = idx_ref[...].astype(F32)
    pos = pos_ref[...].astype(F32)
    pstart = ps_ref[...]
    lane_ef = lane_e.astype(F32)
    cols = []
    for k in range(TOP_K):
        e_k = _lane_sum(jnp.where(lane_o == k, idx, 0.0))
        p_k = _lane_sum(jnp.where(lane_o == k, pos, 0.0))
        ps_k = _lane_sum(jnp.where(lane_ef == e_k, pstart, 0.0))
        cols.append(ps_k * float(MOE_BLOCK) + p_k)
    slot_ref[...] = _pack_cols(cols, lane_o).astype(I32)


def _slots(cnt, idx, pos, tm=640):
    t = idx.shape[0]
    ne = cnt.shape[1]
    triu = jnp.triu(jnp.ones((ne, ne), BF16))
    row = lambda i: (i, 0)
    full = lambda i: (0, 0)
    return pl.pallas_call(
        _slots_kernel,
        grid=(t // tm,),
        in_specs=[pl.BlockSpec((1, ne), full), pl.BlockSpec((tm, LANES), row),
                  pl.BlockSpec((tm, LANES), row), pl.BlockSpec((ne, ne), full)],
        out_specs=[pl.BlockSpec((tm, LANES), row), pl.BlockSpec((1, ne), full),
                   pl.BlockSpec((1, ne), full)],
        out_shape=[jax.ShapeDtypeStruct((t, LANES), I32), jax.ShapeDtypeStruct((1, ne), I32),
                   jax.ShapeDtypeStruct((1, ne), I32)],
        scratch_shapes=[pltpu.VMEM((1, ne), F32)],
        compiler_params=_cparams("arbitrary"),
        name="slots",
    )(cnt, idx, pos, triu)


def _invert_kernel(per_step, slot_ref, fill_hbm, out_ref):
    i = pl.program_id(0)

    @pl.when(i == 0)
    def _():
        pltpu.sync_copy(fill_hbm, out_ref)

    base = i * per_step

    def put(j, c):
        out_ref[slot_ref[j]] = base + j
        return c
    lax.fori_loop(0, per_step, put, 0, unroll=16)


def _invert(slot_flat, n_slots, fill_value, steps=5):
    n = slot_flat.shape[0]
    per = n // steps
    assert per * steps == n and per % 1024 == 0, "rank-1 SMEM blocks must be multiples of 1024"
    return pl.pallas_call(
        functools.partial(_invert_kernel, per),
        grid=(steps,),
        in_specs=[pl.BlockSpec((per,), lambda i: (i,), memory_space=pltpu.SMEM),
                  pl.BlockSpec(memory_space=pl.ANY)],
        out_specs=pl.BlockSpec((n_slots,), lambda i: (0,), memory_space=pltpu.SMEM),
        out_shape=jax.ShapeDtypeStruct((n_slots,), I32),
        compiler_params=_cparams("arbitrary"),
        name="invert",
    )(slot_flat, jnp.full((n_slots,), fill_value, I32))


BUF_RING = 4
IDS_RING = 8
GATHER_AHEAD = 2
TRASH_ROWS = BUF_RING * MOE_BLOCK
WEIGHT_DMA_PRIORITY = 1
N_HALF = 256


def _experts_kernel(n_tab, t_tok, bs_ref, bn_ref, tab_hbm, h_hbm, wg_hbm, wu_hbm, wd_hbm, y_hbm,
                    xbuf, ybuf, wgf, wuf, wdf, wgb, wub, wdb, ids, gsem, ssem, isem, wsem):
    e = pl.program_id(0)
    n_exp = pl.num_programs(0)
    trash = TOP_K * t_tok
    k_shift = TOP_K.bit_length() - 1
    blk_rows = MOE_BLOCK * SUBLANES

    def weight_copies(ex):
        s = ex & 1
        return [pltpu.make_async_copy(src.at[ex], dst.at[s], wsem.at[s, i])
                for i, (src, dst) in enumerate(((wg_hbm, wgf), (wu_hbm, wuf), (wd_hbm, wdf)))]

    def ids_copy(g):
        q = g & (IDS_RING - 1)
        row = jnp.minimum(g, n_tab - 1)
        return pltpu.make_async_copy(tab_hbm.at[pl.ds(row, 1)], ids.at[pl.ds(q, 1)], isem.at[q])

    def gather_starts(g):
        s, q = g & (BUF_RING - 1), g & (IDS_RING - 1)

        def one(r):
            tok = jnp.minimum(jnp.right_shift(ids[q, r], k_shift), t_tok - 1)
            src = h_hbm.at[pl.ds(pl.multiple_of(tok * SUBLANES, SUBLANES), SUBLANES)]
            pltpu.make_async_copy(src, xbuf.at[s, pl.ds(r * SUBLANES, SUBLANES)], gsem.at[s]).start()
        return [functools.partial(one, r) for r in range(MOE_BLOCK)]

    def scatter_starts(g):
        s, q = g & (BUF_RING - 1), g & (IDS_RING - 1)

        def one(r):
            a = ids[q, r]
            dst = jnp.where(a >= trash, trash + s * MOE_BLOCK + r,
                            (a & (TOP_K - 1)) * t_tok + jnp.right_shift(a, k_shift))
            out = y_hbm.at[pl.ds(pl.multiple_of(dst * SUBLANES, SUBLANES), SUBLANES)]
            pltpu.make_async_copy(ybuf.at[s, pl.ds(r * SUBLANES, SUBLANES)], out, ssem.at[s]).start()
        return [functools.partial(one, r) for r in range(MOE_BLOCK)]

    def wait_gather(s):
        pltpu.make_async_copy(h_hbm.at[pl.ds(0, blk_rows)], xbuf.at[s], gsem.at[s]).wait()

    def wait_scatter(s):
        pltpu.make_async_copy(ybuf.at[s], y_hbm.at[pl.ds(0, blk_rows)], ssem.at[s]).wait()

    @pl.when(e == 0)
    def _():
        for c in weight_copies(e):
            c.start(priority=WEIGHT_DMA_PRIORITY)
        for g0 in range(GATHER_AHEAD + 1):
            ids_copy(g0).start()
        for g0 in range(GATHER_AHEAD):
            ids_copy(g0).wait()
            for start in gather_starts(g0):
                start()
        q_fake = IDS_RING - 1
        for r in range(MOE_BLOCK):
            ids[q_fake, r] = trash
        ybuf[BUF_RING - 1] = jnp.zeros((blk_rows, LANES), U32)

    @pl.when(e + 1 < n_exp)
    def _():
        for c in weight_copies(e + 1):
            c.start(priority=WEIGHT_DMA_PRIORITY)

    for c in weight_copies(e):
        c.wait()

    @pl.when(bn_ref[e] > 0)
    def _():
        s = e & 1
        wgb[...] = wgf[s].astype(BF16)
        wub[...] = wuf[s].astype(BF16)
        wdb[...] = wdf[s].astype(BF16)

    d_exp = wgb.shape[1]
    n_gate = d_exp // N_HALF
    n_pair = wdb.shape[1] // (2 * N_HALF)
    n_phase = 2 * n_gate + 1

    def block(j, carry):
        g = bs_ref[e] + j
        s = g & (BUF_RING - 1)

        @pl.when(g >= BUF_RING - 1)
        def _():
            wait_scatter(s)

        ids_copy(g + GATHER_AHEAD).wait()
        wait_gather(s)

        starts = [f for pair in zip(gather_starts(g + GATHER_AHEAD), scatter_starts(g - 1)) for f in pair]
        bounds = [len(starts) * p // n_phase for p in range(n_phase + 1)]
        phase = iter(range(n_phase))

        def issue():
            p = next(phase)
            for f in starts[bounds[p]:bounds[p + 1]]:
                f()

        x = _load_packed(xbuf.at[s], MOE_BLOCK).astype(BF16)
        ids_copy(g + GATHER_AHEAD + 1).start()
        acts = []
        for c in range(n_gate):
            cols = slice(c * N_HALF, (c + 1) * N_HALF)
            issue()
            gate = _bdot(x, wgb[:, cols])
            issue()
            up = _bdot(x, wub[:, cols])
            acts.append((gate * jax.nn.sigmoid(gate) * up).astype(BF16))
        act = jnp.concatenate(acts, axis=1)
        half = n_pair * N_HALF
        issue()
        for c in range(n_pair):
            y_lo = _bdot(act, wdb[:, c * N_HALF:(c + 1) * N_HALF])
            y_hi = _bdot(act, wdb[:, half + c * N_HALF:half + (c + 1) * N_HALF])
            _store_packed_cols(y_lo, y_hi, ybuf.at[s], c * (N_HALF // LANES), MOE_BLOCK)
        return carry

    lax.fori_loop(0, bn_ref[e], block, 0)

    @pl.when(e == n_exp - 1)
    def _():
        n_used = bs_ref[e] + bn_ref[e]
        for a in range(GATHER_AHEAD):
            wait_gather((n_used + a) & (BUF_RING - 1))
        ids_copy(n_used + GATHER_AHEAD).wait()
        for start in scatter_starts(n_used - 1):
            start()
        for s in range(BUF_RING):
            wait_scatter(s)
        ybuf[...] = jnp.zeros(ybuf.shape, U32)
        for s in range(BUF_RING):
            tail = pltpu.make_async_copy(
                ybuf.at[s], y_hbm.at[pl.ds((trash + s * MOE_BLOCK) * SUBLANES, blk_rows)], ssem.at[s])
            tail.start()
            tail.wait()


def _experts(blk_start, blk_count, table, h2p, wg, wu, wd):
    n_tab = table.shape[0]
    t = h2p.shape[0] // SUBLANES
    ne, d, de = wg.shape
    blk_rows = MOE_BLOCK * SUBLANES
    grid_spec = pltpu.PrefetchScalarGridSpec(
        num_scalar_prefetch=2,
        grid=(ne,),
        in_specs=[pl.BlockSpec(memory_space=pl.ANY)] * 5,
        out_specs=pl.BlockSpec(memory_space=pl.ANY),
        scratch_shapes=[pltpu.VMEM((BUF_RING, blk_rows, LANES), U32), pltpu.VMEM((BUF_RING, blk_rows, LANES), U32),
                        pltpu.VMEM((2, d, de), F32), pltpu.VMEM((2, d, de), F32), pltpu.VMEM((2, de, d), F32),
                        pltpu.VMEM((d, de), BF16), pltpu.VMEM((d, de), BF16), pltpu.VMEM((de, d), BF16),
                        pltpu.SMEM((IDS_RING, MOE_BLOCK), I32),
                        pltpu.SemaphoreType.DMA((BUF_RING,)), pltpu.SemaphoreType.DMA((BUF_RING,)),
                        pltpu.SemaphoreType.DMA((IDS_RING,)), pltpu.SemaphoreType.DMA((2, 3))],
    )
    return pl.pallas_call(
        functools.partial(_experts_kernel, n_tab, t),
        grid_spec=grid_spec,
        out_shape=jax.ShapeDtypeStruct(((TOP_K * t + TRASH_ROWS) * SUBLANES, LANES), U32),
        compiler_params=_cparams("arbitrary"),
        name="experts",
    )(blk_start, blk_count, table, h2p, wg, wu, wd)


def _combine_kernel(n_ptiles, *refs):
    y_refs = refs[:TOP_K]
    (wt_ref, h2_ref, x1_ref, wgs_ref, wus_ref, wds_ref, gf_ref, gtp_ref, gts_ref,
     yp_ref, ysm_ref) = refs[TOP_K:]
    i = pl.program_id(0)

    tm = h2_ref.shape[0]
    h = h2_ref[...]
    g = _bdot(h, wgs_ref[...])
    u = _bdot(h, wus_ref[...])
    f = _bdot((g * jax.nn.sigmoid(g) * u).astype(BF16), wds_ref[...])

    wt = wt_ref[...]
    routed = wt[:, 0:1] * _load_packed(y_refs[0], tm)
    for k in range(1, TOP_K):
        routed = routed + wt[:, k:k + 1] * _load_packed(y_refs[k], tm)
    f = routed + f

    @pl.when(i < n_ptiles)
    def _():
        yp_ref[...] = _rms(x1_ref[...] + gtp_ref[...] * f) * gf_ref[...]

    @pl.when(i == n_ptiles)
    def _():
        ysm_ref[...] = _rms(x1_ref[...] + gts_ref[...] * f) * gf_ref[...]


def _combine(y_planes, wts, h2, x1, wgs, wus, wds, gf, gtp, gts, n_prompt, seq, tm=128):
    t, d = h2.shape
    ts = t - n_prompt
    n_tiles = t // tm
    npt = n_prompt // tm
    per = seq // tm
    row = lambda i: (i, 0)
    full = lambda i: (0, 0)
    return pl.pallas_call(
        functools.partial(_combine_kernel, npt),
        grid=(n_tiles,),
        in_specs=[pl.BlockSpec((tm * SUBLANES, LANES), lambda i, k=k: (k * n_tiles + i, 0))
                  for k in range(TOP_K)] + [
                  pl.BlockSpec((tm, LANES), row), pl.BlockSpec((tm, d), row), pl.BlockSpec((tm, d), row),
                  _const_spec(wgs.shape), _const_spec(wus.shape), _const_spec(wds.shape),
                  pl.BlockSpec((1, d), full),
                  pl.BlockSpec((None, 1, d), lambda i: (jnp.minimum(i, npt - 1) // per, 0, 0)),
                  pl.BlockSpec((ts, d), full)],
        out_specs=[pl.BlockSpec((tm, d), lambda i: (jnp.minimum(i, npt - 1), 0)),
                   pl.BlockSpec((ts, d), full)],
        out_shape=[jax.ShapeDtypeStruct((n_prompt, d), F32), jax.ShapeDtypeStruct((ts, d), F32)],
        compiler_params=_cparams("arbitrary"),
        name="combine",
    )(*([y_planes] * TOP_K), wts, h2, x1, wgs, wus, wds, gf.reshape(1, d), gtp, gts)


def kernel(x_prompt, x_sample, state_conv, c_prompt, c_sample, w_ada, b_ada, g_norm1, w_in, a_norm_g, a_norm_b, w_s, b_s, conv_w, conv_b, b_norm_g, b_norm_b, w_a_out, w_b_out, w_o, g_norm2, w_router, router_bias, w_gate_e, w_up_e, w_down_e, w_gate_s, w_up_s, w_down_s, g_final):
    depth = w_ada.shape[0]
    assert depth == 1, "single-layer trunk"
    nb, seq, d = x_prompt.shape
    ts, dec_seq, _ = x_sample.shape
    assert dec_seq == 1
    tp = nb * seq
    t = tp + ts
    l = 0

    xp = x_prompt.reshape(tp, d)
    xs = x_sample.reshape(ts, d)

    pad = 16
    c_all = jnp.concatenate([c_prompt, jnp.zeros((pad - nb, d), F32), c_sample], axis=0)
    mod = _ada(c_all, w_ada[l], b_ada[l]).reshape(pad + ts, 6, d)
    mp = [mod[:nb, k].reshape(nb, 1, d) for k in range(6)]
    ms = [mod[pad:, k] for k in range(6)]

    h = _hmod(xp, xs, g_norm1[l], mp[1], mp[0], ms[1], ms[0], seq)
    w_in16 = w_in[l].astype(BF16)
    blk = lambda s: _col_block(w_in16, d, s)
    vec = lambda a: a.reshape(1, d)
    (u,) = _rows_call(_seg_u_kernel, h, ts, 512, [blk(0)], [], [BF16], "seg_u")
    v32, v16 = _rows_call(_seg_v_kernel, h, ts, 512, [blk(1)], [vec(a_norm_g[l]), vec(a_norm_b[l])],
                          [F32, BF16], "seg_v")
    (glu,) = _rows_call(_seg_glu_kernel, h, ts, 512, [blk(2), blk(3)], [], [F32], "seg_glu")
    ga, gb = _rows_call(_seg_gate_kernel, h, ts, 512, [blk(4), blk(5)], [], [BF16, BF16], "seg_gate")

    gdim = d // A_GROUPS
    cb, ng, nbb = vec(conv_b[l]), vec(b_norm_g[l]), vec(b_norm_b[l])
    yap, ybp = _mixp(u, v16, glu, w_s[l], b_s[l].T, conv_w[l], cb, ng, nbb, tp, seq)
    yas, ybs = _mixs(u, v32, glu, state_conv[l], vec(jnp.repeat(w_s[l][:, 0, 0], gdim)),
                     vec(jnp.repeat(b_s[l][:, 0], gdim)), conv_w[l], cb, ng, nbb, tp)

    wr = w_router[l]
    wrh = wr.astype(BF16)
    wrl = (wr - wrh.astype(F32)).astype(BF16)
    x1, h2, h2p, logits = _proj(yap, ybp, yas, ybs, ga, gb, xp, xs,
                           w_a_out[l].astype(BF16), w_b_out[l].astype(BF16), w_o[l].astype(BF16),
                           wrh, wrl, g_norm2[l], mp[2], mp[4], mp[3], ms[2], ms[4], ms[3], seq)

    n_blocks = -(-(t * TOP_K) // MOE_BLOCK) + N_EXPERTS
    idx, pos, wts, cnt = _route(logits, router_bias[l])
    slot, blk_start, blk_count = _slots(cnt, idx, pos)
    slot_tab = slot[:, :TOP_K].reshape(t * TOP_K)
    table = _invert(slot_tab, n_blocks * MOE_BLOCK, t * TOP_K).reshape(n_blocks, MOE_BLOCK)
    y_planes = _experts(blk_start[0], blk_count[0], table, h2p, w_gate_e[l], w_up_e[l], w_down_e[l])
    yp, ysm = _combine(y_planes, wts, h2, x1,
                       w_gate_s[l].astype(BF16), w_up_s[l].astype(BF16), w_down_s[l].astype(BF16),
                       g_final, mp[5], ms[5], tp, seq)

    n_past = state_conv.shape[2]
    last_rows = lambda a, n: jnp.stack([a[(b + 1) * seq - n:(b + 1) * seq] for b in range(nb)])[None]
    conv_p = last_rows(glu, n_past)
    conv_s = jnp.concatenate([state_conv[l][:, 1:], glu[tp:][:, None]], axis=1)[None]
    chunkv_p = last_rows(v32, CHUNK)
    chunkv_s = v32[tp:].reshape(1, ts, 1, d)
    return (yp.reshape(nb, seq, d), ysm.reshape(ts, 1, d), conv_p, conv_s, chunkv_p, chunkv_s)
```

```python
import functools

import jax
import jax.numpy as jnp
from jax import lax
from jax.experimental import pallas as pl
from jax.experimental.pallas import tpu as pltpu

F32 = jnp.float32
BF16 = jnp.bfloat16
I32 = jnp.int32

EPS = 1e-6
CHUNK = 128
A_GROUPS = 8
CONV_W = 31
N_EXPERTS = 256
TOP_K = 8
N_EXPERT_GROUPS = 8
GROUP_SIZE = N_EXPERTS // N_EXPERT_GROUPS
TOPK_GROUPS = 4
ROUTED_SCALE = 2.5
MOE_BLOCK = 128
LANES = 128
SUBLANES = 8
VMEM_LIMIT = 58 * 1024 * 1024

NEG_INF = float("-inf")


def _cparams(*sem):
    return pltpu.CompilerParams(dimension_semantics=sem, vmem_limit_bytes=VMEM_LIMIT)


def _const_spec(shape):
    nd = len(shape)
    return pl.BlockSpec(shape, lambda *_: (0,) * nd, pipeline_mode=pl.Buffered(1))


def _rms(x):
    return x * lax.rsqrt(jnp.mean(x * x, axis=-1, keepdims=True) + EPS)


def _layer_norm(x, g, b):
    mu = jnp.mean(x, axis=-1, keepdims=True)
    xc = x - mu
    var = jnp.mean(xc * xc, axis=-1, keepdims=True)
    return xc * lax.rsqrt(var + EPS) * g + b


def _bdot(a, b):
    return jnp.dot(a, b, preferred_element_type=F32)


U32 = jnp.uint32
HI_MASK = 0xFFFF0000


def _store_packed_cols(x_lo, x_hi, ref, c0, n_rows):
    for i in range(x_lo.shape[1] // LANES):
        lo = x_lo[:, i * LANES:(i + 1) * LANES].astype(BF16).astype(F32)
        hi = x_hi[:, i * LANES:(i + 1) * LANES].astype(BF16).astype(F32)
        word = (lax.bitcast_convert_type(hi, U32) & U32(HI_MASK)) | (lax.bitcast_convert_type(lo, U32) >> 16)
        ref[pl.ds(c0 + i, n_rows, stride=SUBLANES), :] = word


def _store_packed(x, ref, n_rows):
    half = x.shape[1] // 2
    assert half == SUBLANES * LANES
    _store_packed_cols(x[:, :half], x[:, half:], ref, 0, n_rows)


def _load_packed(ref, n_rows):
    lo, hi = [], []
    for c in range(SUBLANES):
        word = ref[pl.ds(c, n_rows, stride=SUBLANES), :]
        lo.append(lax.bitcast_convert_type(word << 16, F32))
        hi.append(lax.bitcast_convert_type(word & U32(HI_MASK), F32))
    return jnp.concatenate(lo + hi, axis=1)


def _lane_sum(x):
    return jnp.sum(x, axis=-1, keepdims=True)


def _lane_max(x):
    return jnp.max(x, axis=-1, keepdims=True)


def _lane_min(x):
    return jnp.min(x, axis=-1, keepdims=True)


def _ada_kernel(c_ref, w_ref, b_ref, o_ref):
    c = c_ref[...]
    s = (c * jax.nn.sigmoid(c)).astype(BF16)
    o_ref[...] = _bdot(s, w_ref[...].astype(BF16)) + b_ref[...]


def _ada(c_all, w_ada, b_ada, tn=1024):
    m, d = c_all.shape
    n = w_ada.shape[1]
    return pl.pallas_call(
        _ada_kernel,
        grid=(n // tn,),
        in_specs=[pl.BlockSpec((m, d), lambda j: (0, 0)),
                  pl.BlockSpec((d, tn), lambda j: (0, j)),
                  pl.BlockSpec((1, tn), lambda j: (0, j))],
        out_specs=pl.BlockSpec((m, tn), lambda j: (0, j)),
        out_shape=jax.ShapeDtypeStruct((m, n), F32),
        compiler_params=_cparams("arbitrary"),
        name="ada",
    )(c_all, w_ada, b_ada.reshape(1, n))


def _hmod_kernel(n_ptiles, n_s, xp_ref, xs_ref, g_ref, scp_ref, shp_ref, scs_ref, shs_ref, o_ref):
    i = pl.program_id(0)

    def body(x, sc, sh):
        return ((_rms(x) * g_ref[...]) * (1.0 + sc) + sh).astype(o_ref.dtype)

    @pl.when(i < n_ptiles)
    def _():
        o_ref[...] = body(xp_ref[...], scp_ref[...], shp_ref[...])

    @pl.when(i == n_ptiles)
    def _():
        o_ref[0:n_s, :] = body(xs_ref[...], scs_ref[...], shs_ref[...])


def _hmod(xp, xs, g, scp, shp, scs, shs, seq, tm=512):
    tp, d = xp.shape
    ts = xs.shape[0]
    npt = tp // tm
    per = seq // tm
    last = npt - 1
    pmod = pl.BlockSpec((None, 1, d), lambda i: (jnp.minimum(i, last) // per, 0, 0))
    full = lambda i: (0, 0)
    return pl.pallas_call(
        functools.partial(_hmod_kernel, npt, ts),
        grid=(npt + 1,),
        in_specs=[pl.BlockSpec((tm, d), lambda i: (jnp.minimum(i, last), 0)),
                  pl.BlockSpec((ts, d), full),
                  pl.BlockSpec((1, d), full),
                  pmod, pmod,
                  pl.BlockSpec((ts, d), full),
                  pl.BlockSpec((ts, d), full)],
        out_specs=pl.BlockSpec((tm, d), lambda i: (i, 0)),
        out_shape=jax.ShapeDtypeStruct((tp + ts, d), BF16),
        compiler_params=_cparams("arbitrary"),
        name="hmod",
    )(xp, xs, g.reshape(1, d), scp, shp, scs, shs)


def _rows_call(kernel, h, n_tail, tm, weights, vecs, out_dtypes, name):
    t, d = h.shape
    n_full = (t - n_tail) // tm

    def wrapped(*refs):
        i = pl.program_id(0)

        @pl.when(i < n_full)
        def _():
            kernel(tm, *refs)

        @pl.when(i == n_full)
        def _():
            kernel(n_tail, *refs)

    in_specs = [pl.BlockSpec((tm, d), lambda i: (i, 0))]
    in_specs += [pl.BlockSpec(bs, im, pipeline_mode=pl.Buffered(1)) for (_, bs, im) in weights]
    in_specs += [pl.BlockSpec(v.shape, lambda i: (0, 0)) for v in vecs]
    return pl.pallas_call(
        wrapped,
        grid=(n_full + 1,),
        in_specs=in_specs,
        out_specs=[pl.BlockSpec((tm, d), lambda i: (i, 0)) for _ in out_dtypes],
        out_shape=[jax.ShapeDtypeStruct((t, d), dt) for dt in out_dtypes],
        compiler_params=_cparams("arbitrary"),
        name=name,
    )(h, *[w for (w, _, _) in weights], *vecs)


def _seg_u_kernel(rows, h_ref, w_ref, o_ref):
    z = _bdot(h_ref[0:rows, :], w_ref[...])
    o_ref[0:rows, :] = jax.nn.gelu(z).astype(o_ref.dtype)


def _seg_v_kernel(rows, h_ref, w_ref, g_ref, b_ref, o32_ref, o16_ref):
    z = _bdot(h_ref[0:rows, :], w_ref[...])
    v = _layer_norm(jax.nn.gelu(z), g_ref[...], b_ref[...])
    o32_ref[0:rows, :] = v
    o16_ref[0:rows, :] = v.astype(o16_ref.dtype)


def _seg_glu_kernel(rows, h_ref, wa_ref, wb_ref, o_ref):
    h = h_ref[0:rows, :]
    o_ref[0:rows, :] = _bdot(h, wa_ref[...]) * jax.nn.sigmoid(_bdot(h, wb_ref[...]))


def _seg_gate_kernel(rows, h_ref, wa_ref, wb_ref, oa_ref, ob_ref):
    h = h_ref[0:rows, :]
    oa_ref[0:rows, :] = jax.nn.sigmoid(_bdot(h, wa_ref[...])).astype(oa_ref.dtype)
    ob_ref[0:rows, :] = jax.nn.sigmoid(_bdot(h, wb_ref[...])).astype(ob_ref.dtype)


def _col_block(w, d, seg):
    return (w, (w.shape[0], d), lambda i, seg=seg: (0, seg))


CONV_ROWS = 128
CONV_LANES = 128


def _mixp_kernel(tm, halo, per, u_ref, v_ref, glu_ref, prev_ref, ws_ref, bst_ref, cw_ref, cb_ref,
                 ng_ref, nb_ref, ya_ref, yb_ref, xp_ref, cv_ref):
    i = pl.program_id(0)
    d = u_ref.shape[1]
    n_chunks = tm // CHUNK
    gdim = d // A_GROUPS

    r_io = lax.broadcasted_iota(I32, (CHUNK, CHUNK), 0)
    c_io = lax.broadcasted_iota(I32, (CHUNK, CHUNK), 1)
    tril = c_io <= r_io
    for g in range(A_GROUPS):
        cs = slice(g * gdim, (g + 1) * gdim)
        wg = jnp.where(tril, ws_ref[g], 0.0).astype(BF16)
        vg = jnp.concatenate([v_ref[c * CHUNK:(c + 1) * CHUNK, cs] for c in range(n_chunks)], axis=1)
        mix = _bdot(wg, vg)
        bcol = bst_ref[:, g:g + 1]
        for c in range(n_chunks):
            rs = slice(c * CHUNK, (c + 1) * CHUNK)
            ya_ref[rs, cs] = (u_ref[rs, cs].astype(F32)
                              * (mix[:, c * gdim:(c + 1) * gdim] + bcol)).astype(ya_ref.dtype)

    xp_ref[0:halo, :] = jnp.where(i % per == 0, 0.0, prev_ref[...])
    xp_ref[halo:halo + tm, :] = glu_ref[...]
    first = halo - (CONV_W - 1)
    n_cb = d // CONV_LANES

    def chunk(idx, carry):
        r0 = pl.multiple_of((idx // n_cb) * CONV_ROWS, CONV_ROWS)
        c0 = pl.multiple_of((idx % n_cb) * CONV_LANES, CONV_LANES)
        acc = jnp.zeros((CONV_ROWS, CONV_LANES), F32)
        n_win = CONV_ROWS + halo
        window = xp_ref[pl.ds(r0, n_win), pl.ds(c0, CONV_LANES)]
        for j in range(SUBLANES):
            taps = [m for m in range(first, first + CONV_W) if m % SUBLANES == j]
            s_j = window if j == 0 else pltpu.roll(window, n_win - j, 0)
            for m in taps:
                w_row = cw_ref[pl.ds(m - first, 1), pl.ds(c0, CONV_LANES)]
                acc = acc + w_row * s_j[m - j:m - j + CONV_ROWS, :]
        cv_ref[pl.ds(r0, CONV_ROWS), pl.ds(c0, CONV_LANES)] = acc
        return carry

    lax.fori_loop(0, (tm // CONV_ROWS) * n_cb, chunk, 0)

    y = _layer_norm(cv_ref[...] + cb_ref[...], ng_ref[...], nb_ref[...])
    yb_ref[...] = (y * jax.nn.sigmoid(y)).astype(yb_ref.dtype)


def _mixp(u, v16, glu, w_s, b_s_t, conv_w, conv_b, ng, nb, n_prompt, seq, tm=512, halo=32):
    d = u.shape[1]
    npt = n_prompt // tm
    per = seq // tm
    hb = tm // halo
    row = lambda i: (i, 0)
    return pl.pallas_call(
        functools.partial(_mixp_kernel, tm, halo, per),
        grid=(npt,),
        in_specs=[pl.BlockSpec((tm, d), row),
                  pl.BlockSpec((tm, d), row),
                  pl.BlockSpec((tm, d), row),
                  pl.BlockSpec((halo, d), lambda i: (jnp.maximum(i * hb - 1, 0), 0)),
                  _const_spec(w_s.shape), _const_spec(b_s_t.shape), _const_spec(conv_w.shape),
                  _const_spec(conv_b.shape), _const_spec(ng.shape), _const_spec(nb.shape)],
        out_specs=[pl.BlockSpec((tm, d), row), pl.BlockSpec((tm, d), row)],
        out_shape=[jax.ShapeDtypeStruct((n_prompt, d), BF16), jax.ShapeDtypeStruct((n_prompt, d), BF16)],
        scratch_shapes=[pltpu.VMEM((halo + tm, d), F32), pltpu.VMEM((tm, d), F32)],
        compiler_params=_cparams("arbitrary"),
        name="mixp",
    )(u, v16, glu, glu, w_s, b_s_t, conv_w, conv_b, ng, nb)


def _mixs_kernel(u_ref, v_ref, glu_ref, st_ref, ws0_ref, bs0_ref, cw_ref, cb_ref, ng_ref, nb_ref,
                 ya_ref, yb_ref):
    n_past = st_ref.shape[1]
    ya_ref[...] = (u_ref[...].astype(F32) * (ws0_ref[...] * v_ref[...] + bs0_ref[...])).astype(ya_ref.dtype)
    acc = cw_ref[n_past:n_past + 1, :] * glu_ref[...] + cb_ref[...]
    acc = acc + jnp.sum(st_ref[...] * cw_ref[0:n_past, :][None], axis=1)
    y = _layer_norm(acc, ng_ref[...], nb_ref[...])
    yb_ref[...] = (y * jax.nn.sigmoid(y)).astype(yb_ref.dtype)


def _mixs(u, v32, glu, state, ws0, bs0, conv_w, conv_b, ng, nb, n_prompt, tb=32):
    ts, n_past, d = state.shape
    off = n_prompt // tb
    tail = lambda j: (off + j, 0)
    vec = lambda j: (0, 0)
    return pl.pallas_call(
        _mixs_kernel,
        grid=(ts // tb,),
        in_specs=[pl.BlockSpec((tb, d), tail), pl.BlockSpec((tb, d), tail), pl.BlockSpec((tb, d), tail),
                  pl.BlockSpec((tb, n_past, d), lambda j: (j, 0, 0)),
                  pl.BlockSpec((1, d), vec), pl.BlockSpec((1, d), vec),
                  pl.BlockSpec(conv_w.shape, vec), pl.BlockSpec((1, d), vec),
                  pl.BlockSpec((1, d), vec), pl.BlockSpec((1, d), vec)],
        out_specs=[pl.BlockSpec((tb, d), lambda j: (j, 0)), pl.BlockSpec((tb, d), lambda j: (j, 0))],
        out_shape=[jax.ShapeDtypeStruct((ts, d), BF16), jax.ShapeDtypeStruct((ts, d), BF16)],
        compiler_params=_cparams("arbitrary"),
        name="mixs",
    )(u, v32, glu, state, ws0, bs0, conv_w, conv_b, ng, nb)


def _proj_kernel(n_ptiles, tm, n_s,
                 yap_ref, ybp_ref, yas_ref, ybs_ref, ga_ref, gb_ref, xp_ref, xs_ref,
                 wa_ref, wb_ref, wo_ref, wrh_ref, wrl_ref, g2_ref,
                 gtp_ref, scp_ref, shp_ref, gts_ref, scs_ref, shs_ref,
                 x1_ref, h2_ref, h2p_ref, lg_ref):
    i = pl.program_id(0)

    def body(rows, ya, yb, x, gate, sc, sh):
        pa = _bdot(ya, wa_ref[...])
        pb = _bdot(yb, wb_ref[...])
        merged = ga_ref[0:rows, :].astype(F32) * pa + gb_ref[0:rows, :].astype(F32) * pb
        out = _bdot(merged.astype(BF16), wo_ref[...])
        x1 = x + gate * out
        h2 = (_rms(x1) * g2_ref[...]) * (1.0 + sc) + sh
        hi = h2.astype(BF16)
        lo = (h2 - hi.astype(F32)).astype(BF16)
        logits = _bdot(hi, wrh_ref[...]) + (_bdot(hi, wrl_ref[...]) + _bdot(lo, wrh_ref[...]))
        x1_ref[0:rows, :] = x1
        h2_ref[0:rows, :] = hi
        _store_packed(h2, h2p_ref, rows)
        lg_ref[0:rows, :] = logits

    @pl.when(i < n_ptiles)
    def _():
        body(tm, yap_ref[...], ybp_ref[...], xp_ref[...], gtp_ref[...], scp_ref[...], shp_ref[...])

    @pl.when(i == n_ptiles)
    def _():
        body(n_s, yas_ref[...], ybs_ref[...], xs_ref[...], gts_ref[...], scs_ref[...], shs_ref[...])


def _proj(yap, ybp, yas, ybs, ga, gb, xp, xs, wa, wb, wo, wrh, wrl, g2,
          gtp, scp, shp, gts, scs, shs, seq, tm=256):
    tp, d = xp.shape
    ts = xs.shape[0]
    t = tp + ts
    ne = wrh.shape[1]
    npt = tp // tm
    per = seq // tm
    last = npt - 1
    prow = lambda i: (jnp.minimum(i, last), 0)
    row = lambda i: (i, 0)
    full = lambda i: (0, 0)
    pmod = pl.BlockSpec((None, 1, d), lambda i: (jnp.minimum(i, last) // per, 0, 0))
    smod = pl.BlockSpec((ts, d), full)
    return pl.pallas_call(
        functools.partial(_proj_kernel, npt, tm, ts),
        grid=(npt + 1,),
        in_specs=[pl.BlockSpec((tm, d), prow), pl.BlockSpec((tm, d), prow),
                  pl.BlockSpec((ts, d), full), pl.BlockSpec((ts, d), full),
                  pl.BlockSpec((tm, d), row), pl.BlockSpec((tm, d), row),
                  pl.BlockSpec((tm, d), prow), pl.BlockSpec((ts, d), full),
                  _const_spec(wa.shape), _const_spec(wb.shape), _const_spec(wo.shape),
                  _const_spec(wrh.shape), _const_spec(wrl.shape), _const_spec((1, d)),
                  pmod, pmod, pmod, smod, smod, smod],
        out_specs=[pl.BlockSpec((tm, d), row), pl.BlockSpec((tm, d), row),
                   pl.BlockSpec((tm * SUBLANES, LANES), row), pl.BlockSpec((tm, ne), row)],
        out_shape=[jax.ShapeDtypeStruct((t, d), F32), jax.ShapeDtypeStruct((t, d), BF16),
                   jax.ShapeDtypeStruct((t * SUBLANES, LANES), U32), jax.ShapeDtypeStruct((t, ne), F32)],
        compiler_params=_cparams("arbitrary"),
        name="proj",
    )(yap, ybp, yas, ybs, ga, gb, xp, xs, wa, wb, wo, wrh, wrl, g2.reshape(1, d),
      gtp, scp, shp, gts, scs, shs)


def _pack_cols(cols, lane):
    out = jnp.zeros(lane.shape, cols[0].dtype)
    for k, c in enumerate(cols):
        out = jnp.where(lane == k, c, out)
    return out


def _route_kernel(lg_ref, bias_ref, tri_ref, idx_ref, pos_ref, wt_ref, cnt_ref, carry_ref):
    i = pl.program_id(0)

    @pl.when(i == 0)
    def _():
        carry_ref[...] = jnp.zeros_like(carry_ref)

    scores = jax.nn.sigmoid(lg_ref[...])
    sel = scores + bias_ref[...]
    lane = lax.broadcasted_iota(I32, sel.shape, 1)
    lane_f = lane.astype(F32)
    grp = jnp.right_shift(lane, GROUP_SIZE.bit_length() - 1)

    gs_cols = []
    gs_full = jnp.zeros(sel.shape, F32)
    for g in range(N_EXPERT_GROUPS):
        in_g = grp == g
        xg = jnp.where(in_g, sel, NEG_INF)
        m1 = _lane_max(xg)
        n1 = _lane_sum(jnp.where(xg == m1, 1.0, 0.0))
        m2 = jnp.where(n1 >= 2.0, m1, _lane_max(jnp.where(xg < m1, xg, NEG_INF)))
        gs = m1 + m2
        gs_cols.append(gs)
        gs_full = jnp.where(in_g, gs, gs_full)

    beaten = jnp.zeros(sel.shape, F32)
    for g in range(N_EXPERT_GROUPS):
        better = (gs_cols[g] > gs_full) | ((gs_cols[g] == gs_full) & (g < grp))
        beaten = beaten + jnp.where(better, 1.0, 0.0)
    cur = jnp.where(beaten < float(TOPK_GROUPS), sel, NEG_INF)

    idx_cols, s_cols, hots = [], [], []
    picked = jnp.zeros(sel.shape, F32)
    for _ in range(TOP_K):
        m = _lane_max(cur)
        idx = _lane_min(jnp.where(cur == m, lane_f, float(N_EXPERTS)))
        hot = lane_f == idx
        idx_cols.append(idx)
        s_cols.append(_lane_sum(jnp.where(hot, scores, 0.0)))
        hots.append(hot)
        picked = jnp.where(hot, 1.0, picked)
        cur = jnp.where(hot, NEG_INF, cur)

    den = s_cols[0]
    for s in s_cols[1:]:
        den = den + s
    w_cols = [s / den * ROUTED_SCALE for s in s_cols]

    pos_full = _bdot(tri_ref[...], picked.astype(BF16)) + carry_ref[...]
    pos_cols = [_lane_sum(jnp.where(h, pos_full, 0.0)) for h in hots]
    carry_ref[...] = carry_ref[...] + jnp.sum(picked, axis=0, keepdims=True)
    cnt_ref[...] = carry_ref[...]

    lane_o = lax.broadcasted_iota(I32, idx_ref.shape, 1)
    idx_ref[...] = _pack_cols(idx_cols, lane_o).astype(I32)
    pos_ref[...] = _pack_cols(pos_cols, lane_o).astype(I32)
    wt_ref[...] = _pack_cols(w_cols, lane_o)


def _route(logits, bias, tm=640):
    t, ne = logits.shape
    tri = jnp.tril(jnp.ones((tm, tm), BF16), -1)
    row = lambda i: (i, 0)
    return pl.pallas_call(
        _route_kernel,
        grid=(t // tm,),
        in_specs=[pl.BlockSpec((tm, ne), row), pl.BlockSpec((1, ne), lambda i: (0, 0)),
                  pl.BlockSpec((tm, tm), lambda i: (0, 0))],
        out_specs=[pl.BlockSpec((tm, LANES), row), pl.BlockSpec((tm, LANES), row),
                   pl.BlockSpec((tm, LANES), row), pl.BlockSpec((1, ne), lambda i: (0, 0))],
        out_shape=[jax.ShapeDtypeStruct((t, LANES), I32), jax.ShapeDtypeStruct((t, LANES), I32),
                   jax.ShapeDtypeStruct((t, LANES), F32), jax.ShapeDtypeStruct((1, ne), F32)],
        scratch_shapes=[pltpu.VMEM((1, ne), F32)],
        compiler_params=_cparams("arbitrary"),
        name="route",
    )(logits, bias.reshape(1, ne), tri)


def _slots_kernel(cnt_ref, idx_ref, pos_ref, triu_ref, slot_ref, bs_ref, bn_ref, ps_ref):
    i = pl.program_id(0)
    ne = cnt_ref.shape[1]

    @pl.when(i == 0)
    def _():
        cnt = cnt_ref[...]
        nblk = jnp.floor((cnt + float(MOE_BLOCK - 1)) * (1.0 / MOE_BLOCK))
        pend = _bdot(jnp.broadcast_to(nblk, (SUBLANES, ne)).astype(BF16), triu_ref[...])[0:1, :]
        pstart = pend - nblk
        ps_ref[...] = pstart
        bs_ref[...] = pstart.astype(I32)
        bn_ref[...] = nblk.astype(I32)

    lane_o = lax.broadcasted_iota(I32, idx_ref.shape, 1)
    lane_e = lax.broadcasted_iota(I32, (idx_ref.shape[0], ne), 1)
    idx = idx_ref[...].astype(F32)
    pos = pos_ref[...].astype(F32)
    pstart = ps_ref[...]
    lane_ef = lane_e.astype(F32)
    cols = []
    for k in range(TOP_K):
        e_k = _lane_sum(jnp.where(lane_o == k, idx, 0.0))
        p_k = _lane_sum(jnp.where(lane_o == k, pos, 0.0))
        ps_k = _lane_sum(jnp.where(lane_ef == e_k, pstart, 0.0))
        cols.append(ps_k * float(MOE_BLOCK) + p_k)
    slot_ref[...] = _pack_cols(cols, lane_o).astype(I32)


def _slots(cnt, idx, pos, tm=640):
    t = idx.shape[0]
    ne = cnt.shape[1]
    triu = jnp.triu(jnp.ones((ne, ne), BF16))
    row = lambda i: (i, 0)
    full = lambda i: (0, 0)
    return pl.pallas_call(
        _slots_kernel,
        grid=(t // tm,),
        in_specs=[pl.BlockSpec((1, ne), full), pl.BlockSpec((tm, LANES), row),
                  pl.BlockSpec((tm, LANES), row), pl.BlockSpec((ne, ne), full)],
        out_specs=[pl.BlockSpec((tm, LANES), row), pl.BlockSpec((1, ne), full),
                   pl.BlockSpec((1, ne), full)],
        out_shape=[jax.ShapeDtypeStruct((t, LANES), I32), jax.ShapeDtypeStruct((1, ne), I32),
                   jax.ShapeDtypeStruct((1, ne), I32)],
        scratch_shapes=[pltpu.VMEM((1, ne), F32)],
        compiler_params=_cparams("arbitrary"),
        name="slots",
    )(cnt, idx, pos, triu)


def _invert_kernel(per_step, slot_ref, fill_hbm, out_ref):
    i = pl.program_id(0)

    @pl.when(i == 0)
    def _():
        pltpu.sync_copy(fill_hbm, out_ref)

    base = i * per_step

    def put(j, c):
        out_ref[slot_ref[j]] = base + j
        return c
    lax.fori_loop(0, per_step, put, 0, unroll=16)


def _invert(slot_flat, n_slots, fill_value, steps=5):
    n = slot_flat.shape[0]
    per = n // steps
    assert per * steps == n and per % 1024 == 0, "rank-1 SMEM blocks must be multiples of 1024"
    return pl.pallas_call(
        functools.partial(_invert_kernel, per),
        grid=(steps,),
        in_specs=[pl.BlockSpec((per,), lambda i: (i,), memory_space=pltpu.SMEM),
                  pl.BlockSpec(memory_space=pl.ANY)],
        out_specs=pl.BlockSpec((n_slots,), lambda i: (0,), memory_space=pltpu.SMEM),
        out_shape=jax.ShapeDtypeStruct((n_slots,), I32),
        compiler_params=_cparams("arbitrary"),
        name="invert",
    )(slot_flat, jnp.full((n_slots,), fill_value, I32))


BUF_RING = 4
IDS_RING = 8
GATHER_AHEAD = 2
TRASH_ROWS = BUF_RING * MOE_BLOCK
WEIGHT_DMA_PRIORITY = 1
N_HALF = 256


def _dispatch_kernel(n_tiles, tm, tab_ref, h_hbm, x_hbm, buf, rsem, wsem):
    i = pl.program_id(0)
    rows = tm * SUBLANES
    n_buf = buf.shape[0]

    def read(j):
        s = lax.rem(j, n_buf)
        return pltpu.make_async_copy(h_hbm.at[pl.ds(pl.multiple_of(j * rows, rows), rows)], buf.at[s],
                                     rsem.at[s])

    def wait_writes(s):
        for _ in range(TOP_K):
            pltpu.make_async_copy(buf.at[s], x_hbm.at[pl.ds(0, rows)], wsem.at[s]).wait()

    @pl.when(i == 0)
    def _():
        read(i).start()

    @pl.when(i >= n_buf - 1)
    def _():
        wait_writes(lax.rem(i + 1, n_buf))

    @pl.when(i + 1 < n_tiles)
    def _():
        read(i + 1).start()

    read(i).wait()
    s = lax.rem(i, n_buf)

    def row(r, carry):
        src = buf.at[s, pl.ds(pl.multiple_of(r * SUBLANES, SUBLANES), SUBLANES)]
        for k in range(TOP_K):
            dst = pl.multiple_of(tab_ref[0, 0, r * TOP_K + k] * SUBLANES, SUBLANES)
            pltpu.make_async_copy(src, x_hbm.at[pl.ds(dst, SUBLANES)], wsem.at[s]).start()
        return carry
    lax.fori_loop(0, tm, row, 0, unroll=4)

    @pl.when(i == n_tiles - 1)
    def _():
        for j in range(max(n_tiles - (n_buf - 1), 0), n_tiles):
            wait_writes(j % n_buf)


def _dispatch(slot_tab, h2p, n_slots, tm=128):
    t = h2p.shape[0] // SUBLANES
    n_tiles = t // tm
    tab = slot_tab.reshape(n_tiles, 1, tm * TOP_K)
    return pl.pallas_call(
        functools.partial(_dispatch_kernel, n_tiles, tm),
        grid=(n_tiles,),
        in_specs=[pl.BlockSpec((1, 1, tm * TOP_K), lambda i: (i, 0, 0), memory_space=pltpu.SMEM),
                  pl.BlockSpec(memory_space=pl.ANY)],
        out_specs=pl.BlockSpec(memory_space=pl.ANY),
        out_shape=jax.ShapeDtypeStruct((n_slots * SUBLANES, LANES), U32),
        scratch_shapes=[pltpu.VMEM((3, tm * SUBLANES, LANES), U32), pltpu.SemaphoreType.DMA((3,)),
                        pltpu.SemaphoreType.DMA((3,))],
        compiler_params=_cparams("arbitrary"),
        name="dispatch",
    )(tab, h2p)


def _experts_kernel(n_tab, t_tok, bs_ref, bn_ref, tab_hbm, x_hbm, wg_hbm, wu_hbm, wd_hbm, y_hbm,
                    xbuf, ybuf, wgf, wuf, wdf, wgb, wub, wdb, ids, gsem, ssem, isem, wsem):
    e = pl.program_id(0)
    n_exp = pl.num_programs(0)
    trash = TOP_K * t_tok
    k_shift = TOP_K.bit_length() - 1
    blk_rows = MOE_BLOCK * SUBLANES

    def weight_copies(ex):
        s = ex & 1
        return [pltpu.make_async_copy(src.at[ex], dst.at[s], wsem.at[s, i])
                for i, (src, dst) in enumerate(((wg_hbm, wgf), (wu_hbm, wuf), (wd_hbm, wdf)))]

    def ids_copy(g):
        q = g & (IDS_RING - 1)
        row = jnp.minimum(g, n_tab - 1)
        return pltpu.make_async_copy(tab_hbm.at[pl.ds(row, 1)], ids.at[pl.ds(q, 1)], isem.at[q])

    def gather_starts(g):
        s = g & (BUF_RING - 1)

        def one():
            row = pl.multiple_of(jnp.minimum(g, n_tab - 1) * blk_rows, blk_rows)
            pltpu.make_async_copy(x_hbm.at[pl.ds(row, blk_rows)], xbuf.at[s], gsem.at[s]).start()
        return [one]

    def scatter_starts(g):
        s, q = g & (BUF_RING - 1), g & (IDS_RING - 1)

        def one(r):
            a = ids[q, r]
            dst = jnp.where(a >= trash, trash + s * MOE_BLOCK + r,
                            (a & (TOP_K - 1)) * t_tok + jnp.right_shift(a, k_shift))
            out = y_hbm.at[pl.ds(pl.multiple_of(dst * SUBLANES, SUBLANES), SUBLANES)]
            pltpu.make_async_copy(ybuf.at[s, pl.ds(r * SUBLANES, SUBLANES)], out, ssem.at[s]).start()
        return [functools.partial(one, r) for r in range(MOE_BLOCK)]

    def wait_gather(s):
        pltpu.make_async_copy(x_hbm.at[pl.ds(0, blk_rows)], xbuf.at[s], gsem.at[s]).wait()

    def wait_scatter(s):
        pltpu.make_async_copy(ybuf.at[s], y_hbm.at[pl.ds(0, blk_rows)], ssem.at[s]).wait()

    @pl.when(e == 0)
    def _():
        for c in weight_copies(e):
            c.start(priority=WEIGHT_DMA_PRIORITY)
        for g0 in range(GATHER_AHEAD + 1):
            ids_copy(g0).start()
        for g0 in range(GATHER_AHEAD):
            ids_copy(g0).wait()
            for start in gather_starts(g0):
                start()
        q_fake = IDS_RING - 1
        for r in range(MOE_BLOCK):
            ids[q_fake, r] = trash
        ybuf[BUF_RING - 1] = jnp.zeros((blk_rows, LANES), U32)

    @pl.when(e + 1 < n_exp)
    def _():
        for c in weight_copies(e + 1):
            c.start(priority=WEIGHT_DMA_PRIORITY)

    for c in weight_copies(e):
        c.wait()

    @pl.when(bn_ref[e] > 0)
    def _():
        s = e & 1
        wgb[...] = wgf[s].astype(BF16)
        wub[...] = wuf[s].astype(BF16)
        wdb[...] = wdf[s].astype(BF16)

    d_exp = wgb.shape[1]
    n_gate = d_exp // N_HALF
    n_pair = wdb.shape[1] // (2 * N_HALF)
    n_phase = 2 * n_gate + 1

    def block(j, carry):
        g = bs_ref[e] + j
        s = g & (BUF_RING - 1)

        @pl.when(g >= BUF_RING - 1)
        def _():
            wait_scatter(s)

        ids_copy(g + GATHER_AHEAD).wait()
        wait_gather(s)

        starts = gather_starts(g + GATHER_AHEAD) + scatter_starts(g - 1)
        bounds = [len(starts) * p // n_phase for p in range(n_phase + 1)]
        phase = iter(range(n_phase))

        def issue():
            p = next(phase)
            for f in starts[bounds[p]:bounds[p + 1]]:
                f()

        x = _load_packed(xbuf.at[s], MOE_BLOCK).astype(BF16)
        ids_copy(g + GATHER_AHEAD + 1).start()
        acts = []
        for c in range(n_gate):
            cols = slice(c * N_HALF, (c + 1) * N_HALF)
            issue()
            gate = _bdot(x, wgb[:, cols])
            issue()
            up = _bdot(x, wub[:, cols])
            acts.append((gate * jax.nn.sigmoid(gate) * up).astype(BF16))
        act = jnp.concatenate(acts, axis=1)
        half = n_pair * N_HALF
        issue()
        for c in range(n_pair):
            y_lo = _bdot(act, wdb[:, c * N_HALF:(c + 1) * N_HALF])
            y_hi = _bdot(act, wdb[:, half + c * N_HALF:half + (c + 1) * N_HALF])
            _store_packed_cols(y_lo, y_hi, ybuf.at[s], c * (N_HALF // LANES), MOE_BLOCK)
        return carry

    lax.fori_loop(0, bn_ref[e], block, 0)

    @pl.when(e == n_exp - 1)
    def _():
        n_used = bs_ref[e] + bn_ref[e]
        for a in range(GATHER_AHEAD):
            wait_gather((n_used + a) & (BUF_RING - 1))
        ids_copy(n_used + GATHER_AHEAD).wait()
        for start in scatter_starts(n_used - 1):
            start()
        for s in range(BUF_RING):
            wait_scatter(s)
        ybuf[...] = jnp.zeros(ybuf.shape, U32)
        for s in range(BUF_RING):
            tail = pltpu.make_async_copy(
                ybuf.at[s], y_hbm.at[pl.ds((trash + s * MOE_BLOCK) * SUBLANES, blk_rows)], ssem.at[s])
            tail.start()
            tail.wait()


def _experts(blk_start, blk_count, table, x_sorted, t, wg, wu, wd):
    n_tab = table.shape[0]
    ne, d, de = wg.shape
    blk_rows = MOE_BLOCK * SUBLANES
    grid_spec = pltpu.PrefetchScalarGridSpec(
        num_scalar_prefetch=2,
        grid=(ne,),
        in_specs=[pl.BlockSpec(memory_space=pl.ANY)] * 5,
        out_specs=pl.BlockSpec(memory_space=pl.ANY),
        scratch_shapes=[pltpu.VMEM((BUF_RING, blk_rows, LANES), U32), pltpu.VMEM((BUF_RING, blk_rows, LANES), U32),
                        pltpu.VMEM((2, d, de), F32), pltpu.VMEM((2, d, de), F32), pltpu.VMEM((2, de, d), F32),
                        pltpu.VMEM((d, de), BF16), pltpu.VMEM((d, de), BF16), pltpu.VMEM((de, d), BF16),
                        pltpu.SMEM((IDS_RING, MOE_BLOCK), I32),
                        pltpu.SemaphoreType.DMA((BUF_RING,)), pltpu.SemaphoreType.DMA((BUF_RING,)),
                        pltpu.SemaphoreType.DMA((IDS_RING,)), pltpu.SemaphoreType.DMA((2, 3))],
    )
    return pl.pallas_call(
        functools.partial(_experts_kernel, n_tab, t),
        grid_spec=grid_spec,
        out_shape=jax.ShapeDtypeStruct(((TOP_K * t + TRASH_ROWS) * SUBLANES, LANES), U32),
        compiler_params=_cparams("arbitrary"),
        name="experts",
    )(blk_start, blk_count, table, x_sorted, wg, wu, wd)


def _combine_kernel(n_ptiles, *refs):
    y_refs = refs[:TOP_K]
    (wt_ref, h2_ref, x1_ref, wgs_ref, wus_ref, wds_ref, gf_ref, gtp_ref, gts_ref,
     yp_ref, ysm_ref) = refs[TOP_K:]
    i = pl.program_id(0)

    tm = h2_ref.shape[0]
    h = h2_ref[...]
    g = _bdot(h, wgs_ref[...])
    u = _bdot(h, wus_ref[...])
    f = _bdot((g * jax.nn.sigmoid(g) * u).astype(BF16), wds_ref[...])

    wt = wt_ref[...]
    routed = wt[:, 0:1] * _load_packed(y_refs[0], tm)
    for k in range(1, TOP_K):
        routed = routed + wt[:, k:k + 1] * _load_packed(y_refs[k], tm)
    f = routed + f

    @pl.when(i < n_ptiles)
    def _():
        yp_ref[...] = _rms(x1_ref[...] + gtp_ref[...] * f) * gf_ref[...]

    @pl.when(i == n_ptiles)
    def _():
        ysm_ref[...] = _rms(x1_ref[...] + gts_ref[...] * f) * gf_ref[...]


def _combine(y_planes, wts, h2, x1, wgs, wus, wds, gf, gtp, gts, n_prompt, seq, tm=128):
    t, d = h2.shape
    ts = t - n_prompt
    n_tiles = t // tm
    npt = n_prompt // tm
    per = seq // tm
    row = lambda i: (i, 0)
    full = lambda i: (0, 0)
    return pl.pallas_call(
        functools.partial(_combine_kernel, npt),
        grid=(n_tiles,),
        in_specs=[pl.BlockSpec((tm * SUBLANES, LANES), lambda i, k=k: (k * n_tiles + i, 0))
                  for k in range(TOP_K)] + [
                  pl.BlockSpec((tm, LANES), row), pl.BlockSpec((tm, d), row), pl.BlockSpec((tm, d), row),
                  _const_spec(wgs.shape), _const_spec(wus.shape), _const_spec(wds.shape),
                  pl.BlockSpec((1, d), full),
                  pl.BlockSpec((None, 1, d), lambda i: (jnp.minimum(i, npt - 1) // per, 0, 0)),
                  pl.BlockSpec((ts, d), full)],
        out_specs=[pl.BlockSpec((tm, d), lambda i: (jnp.minimum(i, npt - 1), 0)),
                   pl.BlockSpec((ts, d), full)],
        out_shape=[jax.ShapeDtypeStruct((n_prompt, d), F32), jax.ShapeDtypeStruct((ts, d), F32)],
        compiler_params=_cparams("arbitrary"),
        name="combine",
    )(*([y_planes] * TOP_K), wts, h2, x1, wgs, wus, wds, gf.reshape(1, d), gtp, gts)


def kernel(x_prompt, x_sample, state_conv, c_prompt, c_sample, w_ada, b_ada, g_norm1, w_in, a_norm_g, a_norm_b, w_s, b_s, conv_w, conv_b, b_norm_g, b_norm_b, w_a_out, w_b_out, w_o, g_norm2, w_router, router_bias, w_gate_e, w_up_e, w_down_e, w_gate_s, w_up_s, w_down_s, g_final):
    depth = w_ada.shape[0]
    assert depth == 1, "single-layer trunk"
    nb, seq, d = x_prompt.shape
    ts, dec_seq, _ = x_sample.shape
    assert dec_seq == 1
    tp = nb * seq
    t = tp + ts
    l = 0

    xp = x_prompt.reshape(tp, d)
    xs = x_sample.reshape(ts, d)

    pad = 16
    c_all = jnp.concatenate([c_prompt, jnp.zeros((pad - nb, d), F32), c_sample], axis=0)
    mod = _ada(c_all, w_ada[l], b_ada[l]).reshape(pad + ts, 6, d)
    mp = [mod[:nb, k].reshape(nb, 1, d) for k in range(6)]
    ms = [mod[pad:, k] for k in range(6)]

    h = _hmod(xp, xs, g_norm1[l], mp[1], mp[0], ms[1], ms[0], seq)
    w_in16 = w_in[l].astype(BF16)
    blk = lambda s: _col_block(w_in16, d, s)
    vec = lambda a: a.reshape(1, d)
    (u,) = _rows_call(_seg_u_kernel, h, ts, 512, [blk(0)], [], [BF16], "seg_u")
    v32, v16 = _rows_call(_seg_v_kernel, h, ts, 512, [blk(1)], [vec(a_norm_g[l]), vec(a_norm_b[l])],
                          [F32, BF16], "seg_v")
    (glu,) = _rows_call(_seg_glu_kernel, h, ts, 512, [blk(2), blk(3)], [], [F32], "seg_glu")
    ga, gb = _rows_call(_seg_gate_kernel, h, ts, 512, [blk(4), blk(5)], [], [BF16, BF16], "seg_gate")

    gdim = d // A_GROUPS
    cb, ng, nbb = vec(conv_b[l]), vec(b_norm_g[l]), vec(b_norm_b[l])
    yap, ybp = _mixp(u, v16, glu, w_s[l], b_s[l].T, conv_w[l], cb, ng, nbb, tp, seq)
    yas, ybs = _mixs(u, v32, glu, state_conv[l], vec(jnp.repeat(w_s[l][:, 0, 0], gdim)),
                     vec(jnp.repeat(b_s[l][:, 0], gdim)), conv_w[l], cb, ng, nbb, tp)

    wr = w_router[l]
    wrh = wr.astype(BF16)
    wrl = (wr - wrh.astype(F32)).astype(BF16)
    x1, h2, h2p, logits = _proj(yap, ybp, yas, ybs, ga, gb, xp, xs,
                           w_a_out[l].astype(BF16), w_b_out[l].astype(BF16), w_o[l].astype(BF16),
                           wrh, wrl, g_norm2[l], mp[2], mp[4], mp[3], ms[2], ms[4], ms[3], seq)

    n_blocks = -(-(t * TOP_K) // MOE_BLOCK) + N_EXPERTS
    idx, pos, wts, cnt = _route(logits, router_bias[l])
    slot, blk_start, blk_count = _slots(cnt, idx, pos)
    slot_tab = slot[:, :TOP_K].reshape(t * TOP_K)
    table = _invert(slot_tab, n_blocks * MOE_BLOCK, t * TOP_K).reshape(n_blocks, MOE_BLOCK)
    x_sorted = _dispatch(slot_tab, h2p, n_blocks * MOE_BLOCK)
    y_planes = _experts(blk_start[0], blk_count[0], table, x_sorted, t, w_gate_e[l], w_up_e[l], w_down_e[l])
    yp, ysm = _combine(y_planes, wts, h2, x1,
                       w_gate_s[l].astype(BF16), w_up_s[l].astype(BF16), w_down_s[l].astype(BF16),
                       g_final, mp[5], ms[5], tp, seq)

    n_past = state_conv.shape[2]
    last_rows = lambda a, n: jnp.stack([a[(b + 1) * seq - n:(b + 1) * seq] for b in range(nb)])[None]
    conv_p = last_rows(glu, n_past)
    conv_s = jnp.concatenate([state_conv[l][:, 1:], glu[tp:][:, None]], axis=1)[None]
    chunkv_p = last_rows(v32, CHUNK)
    chunkv_s = v32[tp:].reshape(1, ts, 1, d)
    return (yp.reshape(nb, seq, d), ysm.reshape(ts, 1, d), conv_p, conv_s, chunkv_p, chunkv_s)
```

```python
import functools

import jax
import jax.numpy as jnp
from jax import lax
from jax.experimental import pallas as pl
from jax.experimental.pallas import tpu as pltpu

F32 = jnp.float32
BF16 = jnp.bfloat16
I32 = jnp.int32

EPS = 1e-6
CHUNK = 128
A_GROUPS = 8
CONV_W = 31
N_EXPERTS = 256
TOP_K = 8
N_EXPERT_GROUPS = 8
GROUP_SIZE = N_EXPERTS // N_EXPERT_GROUPS
TOPK_GROUPS = 4
ROUTED_SCALE = 2.5
MOE_BLOCK = 128
LANES = 128
SUBLANES = 8
VMEM_LIMIT = 58 * 1024 * 1024

NEG_INF = float("-inf")


def _cparams(*sem):
    return pltpu.CompilerParams(dimension_semantics=sem, vmem_limit_bytes=VMEM_LIMIT)


def _const_spec(shape):
    nd = len(shape)
    return pl.BlockSpec(shape, lambda *_: (0,) * nd, pipeline_mode=pl.Buffered(1))


def _rms(x):
    return x * lax.rsqrt(jnp.mean(x * x, axis=-1, keepdims=True) + EPS)


def _layer_norm(x, g, b):
    mu = jnp.mean(x, axis=-1, keepdims=True)
    xc = x - mu
    var = jnp.mean(xc * xc, axis=-1, keepdims=True)
    return xc * lax.rsqrt(var + EPS) * g + b


def _bdot(a, b):
    return jnp.dot(a, b, preferred_element_type=F32)


U32 = jnp.uint32
HI_MASK = 0xFFFF0000


def _store_packed_cols(x_lo, x_hi, ref, c0, n_rows):
    for i in range(x_lo.shape[1] // LANES):
        lo = x_lo[:, i * LANES:(i + 1) * LANES].astype(BF16).astype(F32)
        hi = x_hi[:, i * LANES:(i + 1) * LANES].astype(BF16).astype(F32)
        word = (lax.bitcast_convert_type(hi, U32) & U32(HI_MASK)) | (lax.bitcast_convert_type(lo, U32) >> 16)
        ref[pl.ds(c0 + i, n_rows, stride=SUBLANES), :] = word


def _store_packed(x, ref, n_rows):
    half = x.shape[1] // 2
    assert half == SUBLANES * LANES
    _store_packed_cols(x[:, :half], x[:, half:], ref, 0, n_rows)


def _load_packed(ref, n_rows):
    lo, hi = [], []
    for c in range(SUBLANES):
        word = ref[pl.ds(c, n_rows, stride=SUBLANES), :]
        lo.append(lax.bitcast_convert_type(word << 16, F32))
        hi.append(lax.bitcast_convert_type(word & U32(HI_MASK), F32))
    return jnp.concatenate(lo + hi, axis=1)


def _lane_sum(x):
    return jnp.sum(x, axis=-1, keepdims=True)


def _lane_max(x):
    return jnp.max(x, axis=-1, keepdims=True)


def _lane_min(x):
    return jnp.min(x, axis=-1, keepdims=True)


def _ada_kernel(c_ref, w_ref, b_ref, o_ref):
    c = c_ref[...]
    s = (c * jax.nn.sigmoid(c)).astype(BF16)
    o_ref[...] = _bdot(s, w_ref[...].astype(BF16)) + b_ref[...]


def _ada(c_all, w_ada, b_ada, tn=1024):
    m, d = c_all.shape
    n = w_ada.shape[1]
    return pl.pallas_call(
        _ada_kernel,
        grid=(n // tn,),
        in_specs=[pl.BlockSpec((m, d), lambda j: (0, 0)),
                  pl.BlockSpec((d, tn), lambda j: (0, j)),
                  pl.BlockSpec((1, tn), lambda j: (0, j))],
        out_specs=pl.BlockSpec((m, tn), lambda j: (0, j)),
        out_shape=jax.ShapeDtypeStruct((m, n), F32),
        compiler_params=_cparams("arbitrary"),
        name="ada",
    )(c_all, w_ada, b_ada.reshape(1, n))


def _hmod_kernel(n_ptiles, n_s, xp_ref, xs_ref, g_ref, scp_ref, shp_ref, scs_ref, shs_ref, o_ref):
    i = pl.program_id(0)

    def body(x, sc, sh):
        return ((_rms(x) * g_ref[...]) * (1.0 + sc) + sh).astype(o_ref.dtype)

    @pl.when(i < n_ptiles)
    def _():
        o_ref[...] = body(xp_ref[...], scp_ref[...], shp_ref[...])

    @pl.when(i == n_ptiles)
    def _():
        o_ref[0:n_s, :] = body(xs_ref[...], scs_ref[...], shs_ref[...])


def _hmod(xp, xs, g, scp, shp, scs, shs, seq, tm=512):
    tp, d = xp.shape
    ts = xs.shape[0]
    npt = tp // tm
    per = seq // tm
    last = npt - 1
    pmod = pl.BlockSpec((None, 1, d), lambda i: (jnp.minimum(i, last) // per, 0, 0))
    full = lambda i: (0, 0)
    return pl.pallas_call(
        functools.partial(_hmod_kernel, npt, ts),
        grid=(npt + 1,),
        in_specs=[pl.BlockSpec((tm, d), lambda i: (jnp.minimum(i, last), 0)),
                  pl.BlockSpec((ts, d), full),
                  pl.BlockSpec((1, d), full),
                  pmod, pmod,
                  pl.BlockSpec((ts, d), full),
                  pl.BlockSpec((ts, d), full)],
        out_specs=pl.BlockSpec((tm, d), lambda i: (i, 0)),
        out_shape=jax.ShapeDtypeStruct((tp + ts, d), BF16),
        compiler_params=_cparams("arbitrary"),
        name="hmod",
    )(xp, xs, g.reshape(1, d), scp, shp, scs, shs)


def _rows_call(kernel, h, n_tail, tm, weights, vecs, out_dtypes, name):
    t, d = h.shape
    n_full = (t - n_tail) // tm

    def wrapped(*refs):
        i = pl.program_id(0)

        @pl.when(i < n_full)
        def _():
            kernel(tm, *refs)

        @pl.when(i == n_full)
        def _():
            kernel(n_tail, *refs)

    in_specs = [pl.BlockSpec((tm, d), lambda i: (i, 0))]
    in_specs += [pl.BlockSpec(bs, im, pipeline_mode=pl.Buffered(1)) for (_, bs, im) in weights]
    in_specs += [pl.BlockSpec(v.shape, lambda i: (0, 0)) for v in vecs]
    return pl.pallas_call(
        wrapped,
        grid=(n_full + 1,),
        in_specs=in_specs,
        out_specs=[pl.BlockSpec((tm, d), lambda i: (i, 0)) for _ in out_dtypes],
        out_shape=[jax.ShapeDtypeStruct((t, d), dt) for dt in out_dtypes],
        compiler_params=_cparams("arbitrary"),
        name=name,
    )(h, *[w for (w, _, _) in weights], *vecs)


def _seg_u_kernel(rows, h_ref, w_ref, o_ref):
    z = _bdot(h_ref[0:rows, :], w_ref[...])
    o_ref[0:rows, :] = jax.nn.gelu(z).astype(o_ref.dtype)


def _seg_v_kernel(rows, h_ref, w_ref, g_ref, b_ref, o32_ref, o16_ref):
    z = _bdot(h_ref[0:rows, :], w_ref[...])
    v = _layer_norm(jax.nn.gelu(z), g_ref[...], b_ref[...])
    o32_ref[0:rows, :] = v
    o16_ref[0:rows, :] = v.astype(o16_ref.dtype)


def _seg_glu_kernel(rows, h_ref, wa_ref, wb_ref, o_ref):
    h = h_ref[0:rows, :]
    o_ref[0:rows, :] = _bdot(h, wa_ref[...]) * jax.nn.sigmoid(_bdot(h, wb_ref[...]))


def _seg_gate_kernel(rows, h_ref, wa_ref, wb_ref, oa_ref, ob_ref):
    h = h_ref[0:rows, :]
    oa_ref[0:rows, :] = jax.nn.sigmoid(_bdot(h, wa_ref[...])).astype(oa_ref.dtype)
    ob_ref[0:rows, :] = jax.nn.sigmoid(_bdot(h, wb_ref[...])).astype(ob_ref.dtype)


def _col_block(w, d, seg):
    return (w, (w.shape[0], d), lambda i, seg=seg: (0, seg))


CONV_ROWS = 128
CONV_LANES = 128


def _mixp_kernel(tm, halo, per, u_ref, v_ref, glu_ref, prev_ref, ws_ref, bst_ref, cw_ref, cb_ref,
                 ng_ref, nb_ref, ya_ref, yb_ref, xp_ref, cv_ref):
    i = pl.program_id(0)
    d = u_ref.shape[1]
    n_chunks = tm // CHUNK
    gdim = d // A_GROUPS

    r_io = lax.broadcasted_iota(I32, (CHUNK, CHUNK), 0)
    c_io = lax.broadcasted_iota(I32, (CHUNK, CHUNK), 1)
    tril = c_io <= r_io
    for g in range(A_GROUPS):
        cs = slice(g * gdim, (g + 1) * gdim)
        wg = jnp.where(tril, ws_ref[g], 0.0).astype(BF16)
        vg = jnp.concatenate([v_ref[c * CHUNK:(c + 1) * CHUNK, cs] for c in range(n_chunks)], axis=1)
        mix = _bdot(wg, vg)
        bcol = bst_ref[:, g:g + 1]
        for c in range(n_chunks):
            rs = slice(c * CHUNK, (c + 1) * CHUNK)
            ya_ref[rs, cs] = (u_ref[rs, cs].astype(F32)
                              * (mix[:, c * gdim:(c + 1) * gdim] + bcol)).astype(ya_ref.dtype)

    xp_ref[0:halo, :] = jnp.where(i % per == 0, 0.0, prev_ref[...])
    xp_ref[halo:halo + tm, :] = glu_ref[...]
    first = halo - (CONV_W - 1)
    n_cb = d // CONV_LANES

    def chunk(idx, carry):
        r0 = pl.multiple_of((idx // n_cb) * CONV_ROWS, CONV_ROWS)
        c0 = pl.multiple_of((idx % n_cb) * CONV_LANES, CONV_LANES)
        acc = jnp.zeros((CONV_ROWS, CONV_LANES), F32)
        n_win = CONV_ROWS + halo
        window = xp_ref[pl.ds(r0, n_win), pl.ds(c0, CONV_LANES)]
        for j in range(SUBLANES):
            taps = [m for m in range(first, first + CONV_W) if m % SUBLANES == j]
            s_j = window if j == 0 else pltpu.roll(window, n_win - j, 0)
            for m in taps:
                w_row = cw_ref[pl.ds(m - first, 1), pl.ds(c0, CONV_LANES)]
                acc = acc + w_row * s_j[m - j:m - j + CONV_ROWS, :]
        cv_ref[pl.ds(r0, CONV_ROWS), pl.ds(c0, CONV_LANES)] = acc
        return carry

    lax.fori_loop(0, (tm // CONV_ROWS) * n_cb, chunk, 0)

    y = _layer_norm(cv_ref[...] + cb_ref[...], ng_ref[...], nb_ref[...])
    yb_ref[...] = (y * jax.nn.sigmoid(y)).astype(yb_ref.dtype)


def _mixp(u, v16, glu, w_s, b_s_t, conv_w, conv_b, ng, nb, n_prompt, seq, tm=512, halo=32):
    d = u.shape[1]
    npt = n_prompt // tm
    per = seq // tm
    hb = tm // halo
    row = lambda i: (i, 0)
    return pl.pallas_call(
        functools.partial(_mixp_kernel, tm, halo, per),
        grid=(npt,),
        in_specs=[pl.BlockSpec((tm, d), row),
                  pl.BlockSpec((tm, d), row),
                  pl.BlockSpec((tm, d), row),
                  pl.BlockSpec((halo, d), lambda i: (jnp.maximum(i * hb - 1, 0), 0)),
                  _const_spec(w_s.shape), _const_spec(b_s_t.shape), _const_spec(conv_w.shape),
                  _const_spec(conv_b.shape), _const_spec(ng.shape), _const_spec(nb.shape)],
        out_specs=[pl.BlockSpec((tm, d), row), pl.BlockSpec((tm, d), row)],
        out_shape=[jax.ShapeDtypeStruct((n_prompt, d), BF16), jax.ShapeDtypeStruct((n_prompt, d), BF16)],
        scratch_shapes=[pltpu.VMEM((halo + tm, d), F32), pltpu.VMEM((tm, d), F32)],
        compiler_params=_cparams("arbitrary"),
        name="mixp",
    )(u, v16, glu, glu, w_s, b_s_t, conv_w, conv_b, ng, nb)


def _mixs_kernel(u_ref, v_ref, glu_ref, st_ref, ws0_ref, bs0_ref, cw_ref, cb_ref, ng_ref, nb_ref,
                 ya_ref, yb_ref, st_out_ref):
    n_past = st_ref.shape[1]
    glu = glu_ref[...]
    state = st_ref[...]
    ya_ref[...] = (u_ref[...].astype(F32) * (ws0_ref[...] * v_ref[...] + bs0_ref[...])).astype(ya_ref.dtype)
    acc = cw_ref[n_past:n_past + 1, :] * glu + cb_ref[...]
    acc = acc + jnp.sum(state * cw_ref[0:n_past, :][None], axis=1)
    y = _layer_norm(acc, ng_ref[...], nb_ref[...])
    yb_ref[...] = (y * jax.nn.sigmoid(y)).astype(yb_ref.dtype)
    st_out_ref[:, 0:n_past - 1, :] = state[:, 1:n_past, :]
    st_out_ref[:, n_past - 1:n_past, :] = glu[:, None, :]


def _mixs(u, v32, glu, state, ws0, bs0, conv_w, conv_b, ng, nb, n_prompt, tb=32):
    _, ts, n_past, d = state.shape
    off = n_prompt // tb
    tail = lambda j: (off + j, 0)
    vec = lambda j: (0, 0)
    st_spec = pl.BlockSpec((None, tb, n_past, d), lambda j: (0, j, 0, 0))
    return pl.pallas_call(
        _mixs_kernel,
        grid=(ts // tb,),
        in_specs=[pl.BlockSpec((tb, d), tail), pl.BlockSpec((tb, d), tail), pl.BlockSpec((tb, d), tail),
                  st_spec,
                  pl.BlockSpec((1, d), vec), pl.BlockSpec((1, d), vec),
                  pl.BlockSpec(conv_w.shape, vec), pl.BlockSpec((1, d), vec),
                  pl.BlockSpec((1, d), vec), pl.BlockSpec((1, d), vec)],
        out_specs=[pl.BlockSpec((tb, d), lambda j: (j, 0)), pl.BlockSpec((tb, d), lambda j: (j, 0)), st_spec],
        out_shape=[jax.ShapeDtypeStruct((ts, d), BF16), jax.ShapeDtypeStruct((ts, d), BF16),
                   jax.ShapeDtypeStruct(state.shape, state.dtype)],
        compiler_params=_cparams("arbitrary"),
        name="mixs",
    )(u, v32, glu, state, ws0, bs0, conv_w, conv_b, ng, nb)


PROJ_ROWS = 128


def _proj_kernel(n_ptiles, tm, n_s,
                 yap_ref, ybp_ref, yas_ref, ybs_ref, ga_ref, gb_ref, xp_ref, xs_ref,
                 wa_ref, wb_ref, wo_ref, wrh_ref, wrl_ref, g2_ref,
                 gtp_ref, scp_ref, shp_ref, gts_ref, scs_ref, shs_ref,
                 x1_ref, h2_ref, h2p_ref, lg_ref):
    i = pl.program_id(0)

    def body(r0, rows, ya_ref, yb_ref, x_ref, gate, sc, sh):
        rs = slice(r0, r0 + rows)
        pa = _bdot(ya_ref[rs, :], wa_ref[...])
        pb = _bdot(yb_ref[rs, :], wb_ref[...])
        merged = ga_ref[rs, :].astype(F32) * pa + gb_ref[rs, :].astype(F32) * pb
        out = _bdot(merged.astype(BF16), wo_ref[...])
        x1 = x_ref[rs, :] + gate * out
        h2 = (_rms(x1) * g2_ref[...]) * (1.0 + sc) + sh
        hi = h2.astype(BF16)
        lo = (h2 - hi.astype(F32)).astype(BF16)
        logits = _bdot(hi, wrh_ref[...]) + (_bdot(hi, wrl_ref[...]) + _bdot(lo, wrh_ref[...]))
        x1_ref[rs, :] = x1
        h2_ref[rs, :] = hi
        _store_packed(h2, h2p_ref.at[pl.ds(r0 * SUBLANES, rows * SUBLANES)], rows)
        lg_ref[rs, :] = logits

    @pl.when(i < n_ptiles)
    def _():
        for r0 in range(0, tm, PROJ_ROWS):
            body(r0, PROJ_ROWS, yap_ref, ybp_ref, xp_ref, gtp_ref[...], scp_ref[...], shp_ref[...])

    @pl.when(i == n_ptiles)
    def _():
        body(0, n_s, yas_ref, ybs_ref, xs_ref, gts_ref[...], scs_ref[...], shs_ref[...])


def _proj(yap, ybp, yas, ybs, ga, gb, xp, xs, wa, wb, wo, wrh, wrl, g2,
          gtp, scp, shp, gts, scs, shs, seq, tm=256):
    tp, d = xp.shape
    ts = xs.shape[0]
    t = tp + ts
    ne = wrh.shape[1]
    npt = tp // tm
    per = seq // tm
    last = npt - 1
    prow = lambda i: (jnp.minimum(i, last), 0)
    row = lambda i: (i, 0)
    full = lambda i: (0, 0)
    pmod = pl.BlockSpec((None, 1, d), lambda i: (jnp.minimum(i, last) // per, 0, 0))
    smod = pl.BlockSpec((ts, d), full)
    return pl.pallas_call(
        functools.partial(_proj_kernel, npt, tm, ts),
        grid=(npt + 1,),
        in_specs=[pl.BlockSpec((tm, d), prow), pl.BlockSpec((tm, d), prow),
                  pl.BlockSpec((ts, d), full), pl.BlockSpec((ts, d), full),
                  pl.BlockSpec((tm, d), row), pl.BlockSpec((tm, d), row),
                  pl.BlockSpec((tm, d), prow), pl.BlockSpec((ts, d), full),
                  _const_spec(wa.shape), _const_spec(wb.shape), _const_spec(wo.shape),
                  _const_spec(wrh.shape), _const_spec(wrl.shape), _const_spec((1, d)),
                  pmod, pmod, pmod, smod, smod, smod],
        out_specs=[pl.BlockSpec((tm, d), row), pl.BlockSpec((tm, d), row),
                   pl.BlockSpec((tm * SUBLANES, LANES), row), pl.BlockSpec((tm, ne), row)],
        out_shape=[jax.ShapeDtypeStruct((t, d), F32), jax.ShapeDtypeStruct((t, d), BF16),
                   jax.ShapeDtypeStruct((t * SUBLANES, LANES), U32), jax.ShapeDtypeStruct((t, ne), F32)],
        compiler_params=_cparams("arbitrary"),
        name="proj",
    )(yap, ybp, yas, ybs, ga, gb, xp, xs, wa, wb, wo, wrh, wrl, g2.reshape(1, d),
      gtp, scp, shp, gts, scs, shs)


def _pack_cols(cols, lane):
    out = jnp.zeros(lane.shape, cols[0].dtype)
    for k, c in enumerate(cols):
        out = jnp.where(lane == k, c, out)
    return out


def _route_kernel(lg_ref, bias_ref, tri_ref, idx_ref, pos_ref, wt_ref, cnt_ref, carry_ref):
    i = pl.program_id(0)

    @pl.when(i == 0)
    def _():
        carry_ref[...] = jnp.zeros_like(carry_ref)

    scores = jax.nn.sigmoid(lg_ref[...])
    sel = scores + bias_ref[...]
    lane = lax.broadcasted_iota(I32, sel.shape, 1)
    lane_f = lane.astype(F32)
    grp = jnp.right_shift(lane, GROUP_SIZE.bit_length() - 1)

    gs_cols = []
    gs_full = jnp.zeros(sel.shape, F32)
    for g in range(N_EXPERT_GROUPS):
        in_g = grp == g
        xg = jnp.where(in_g, sel, NEG_INF)
        m1 = _lane_max(xg)
        n1 = _lane_sum(jnp.where(xg == m1, 1.0, 0.0))
        m2 = jnp.where(n1 >= 2.0, m1, _lane_max(jnp.where(xg < m1, xg, NEG_INF)))
        gs = m1 + m2
        gs_cols.append(gs)
        gs_full = jnp.where(in_g, gs, gs_full)

    beaten = jnp.zeros(sel.shape, F32)
    for g in range(N_EXPERT_GROUPS):
        better = (gs_cols[g] > gs_full) | ((gs_cols[g] == gs_full) & (g < grp))
        beaten = beaten + jnp.where(better, 1.0, 0.0)
    cur = jnp.where(beaten < float(TOPK_GROUPS), sel, NEG_INF)

    idx_cols, s_cols, hots = [], [], []
    picked = jnp.zeros(sel.shape, F32)
    for _ in range(TOP_K):
        m = _lane_max(cur)
        idx = _lane_min(jnp.where(cur == m, lane_f, float(N_EXPERTS)))
        hot = lane_f == idx
        idx_cols.append(idx)
        s_cols.append(_lane_sum(jnp.where(hot, scores, 0.0)))
        hots.append(hot)
        picked = jnp.where(hot, 1.0, picked)
        cur = jnp.where(hot, NEG_INF, cur)

    den = s_cols[0]
    for s in s_cols[1:]:
        den = den + s
    w_cols = [s / den * ROUTED_SCALE for s in s_cols]

    pos_full = _bdot(tri_ref[...], picked.astype(BF16)) + carry_ref[...]
    pos_cols = [_lane_sum(jnp.where(h, pos_full, 0.0)) for h in hots]
    carry_ref[...] = carry_ref[...] + jnp.sum(picked, axis=0, keepdims=True)
    cnt_ref[...] = carry_ref[...]

    lane_o = lax.broadcasted_iota(I32, idx_ref.shape, 1)
    idx_ref[...] = _pack_cols(idx_cols, lane_o).astype(I32)
    pos_ref[...] = _pack_cols(pos_cols, lane_o).astype(I32)
    wt_ref[...] = _pack_cols(w_cols, lane_o)


def _route(logits, bias, tm=640):
    t, ne = logits.shape
    tri = jnp.tril(jnp.ones((tm, tm), BF16), -1)
    row = lambda i: (i, 0)
    return pl.pallas_call(
        _route_kernel,
        grid=(t // tm,),
        in_specs=[pl.BlockSpec((tm, ne), row), pl.BlockSpec((1, ne), lambda i: (0, 0)),
                  pl.BlockSpec((tm, tm), lambda i: (0, 0))],
        out_specs=[pl.BlockSpec((tm, LANES), row), pl.BlockSpec((tm, LANES), row),
                   pl.BlockSpec((tm, LANES), row), pl.BlockSpec((1, ne), lambda i: (0, 0))],
        out_shape=[jax.ShapeDtypeStruct((t, LANES), I32), jax.ShapeDtypeStruct((t, LANES), I32),
                   jax.ShapeDtypeStruct((t, LANES), F32), jax.ShapeDtypeStruct((1, ne), F32)],
        scratch_shapes=[pltpu.VMEM((1, ne), F32)],
        compiler_params=_cparams("arbitrary"),
        name="route",
    )(logits, bias.reshape(1, ne), tri)


def _slots_kernel(cnt_ref, idx_ref, pos_ref, triu_ref, slot_ref, bs_ref, bn_ref, ps_ref):
    i = pl.program_id(0)
    ne = cnt_ref.shape[1]

    @pl.when(i == 0)
    def _():
        cnt = cnt_ref[...]
        nblk = jnp.floor((cnt + float(MOE_BLOCK - 1)) * (1.0 / MOE_BLOCK))
        pend = _bdot(jnp.broadcast_to(nblk, (SUBLANES, ne)).astype(BF16), triu_ref[...])[0:1, :]
        pstart = pend - nblk
        ps_ref[...] = pstart
        bs_ref[...] = pstart.astype(I32)
        bn_ref[...] = nblk.astype(I32)

    lane_o = lax.broadcasted_iota(I32, idx_ref.shape, 1)
    lane_e = lax.broadcasted_iota(I32, (idx_ref.shape[0], ne), 1)
    idx = idx_ref[...].astype(F32)
    pos = pos_ref[...].astype(F32)
    pstart = ps_ref[...]
    lane_ef = lane_e.astype(F32)
    cols = []
    for k in range(TOP_K):
        e_k = _lane_sum(jnp.where(lane_o == k, idx, 0.0))
        p_k = _lane_sum(jnp.where(lane_o == k, pos, 0.0))
        ps_k = _lane_sum(jnp.where(lane_ef == e_k, pstart, 0.0))
        cols.append(ps_k * float(MOE_BLOCK) + p_k)
    slot_ref[...] = _pack_cols(cols, lane_o).astype(I32)


def _slots(cnt, idx, pos, tm=640):
    t = idx.shape[0]
    ne = cnt.shape[1]
    triu = jnp.triu(jnp.ones((ne, ne), BF16))
    row = lambda i: (i, 0)
    full = lambda i: (0, 0)
    return pl.pallas_call(
        _slots_kernel,
        grid=(t // tm,),
        in_specs=[pl.BlockSpec((1, ne), full), pl.BlockSpec((tm, LANES), row),
                  pl.BlockSpec((tm, LANES), row), pl.BlockSpec((ne, ne), full)],
        out_specs=[pl.BlockSpec((tm, LANES), row), pl.BlockSpec((1, ne), full),
                   pl.BlockSpec((1, ne), full)],
        out_shape=[jax.ShapeDtypeStruct((t, LANES), I32), jax.ShapeDtypeStruct((1, ne), I32),
                   jax.ShapeDtypeStruct((1, ne), I32)],
        scratch_shapes=[pltpu.VMEM((1, ne), F32)],
        compiler_params=_cparams("arbitrary"),
        name="slots",
    )(cnt, idx, pos, triu)


BUF_RING = 4
IDS_RING = 8
GATHER_AHEAD = 2
TRASH_ROWS = BUF_RING * MOE_BLOCK
WEIGHT_DMA_PRIORITY = 1
N_HALF = 256


def _dispatch_kernel(n_tiles, tm, tab_ref, fill_hbm, h_hbm, inv_ref, x_hbm, buf, rsem, wsem):
    i = pl.program_id(0)
    rows = tm * SUBLANES
    n_buf = buf.shape[0]

    @pl.when(i == 0)
    def _():
        pltpu.sync_copy(fill_hbm, inv_ref)

    def read(j):
        s = lax.rem(j, n_buf)
        return pltpu.make_async_copy(h_hbm.at[pl.ds(pl.multiple_of(j * rows, rows), rows)], buf.at[s],
                                     rsem.at[s])

    def wait_writes(s):
        for _ in range(TOP_K):
            pltpu.make_async_copy(buf.at[s], x_hbm.at[pl.ds(0, rows)], wsem.at[s]).wait()

    @pl.when(i == 0)
    def _():
        read(i).start()

    @pl.when(i >= n_buf - 1)
    def _():
        wait_writes(lax.rem(i + 1, n_buf))

    @pl.when(i + 1 < n_tiles)
    def _():
        read(i + 1).start()

    read(i).wait()
    s = lax.rem(i, n_buf)

    first_id = i * (tm * TOP_K)

    def row(r, carry):
        src = buf.at[s, pl.ds(pl.multiple_of(r * SUBLANES, SUBLANES), SUBLANES)]
        for k in range(TOP_K):
            slot = tab_ref[0, 0, r * TOP_K + k]
            inv_ref[slot] = first_id + r * TOP_K + k
            dst = pl.multiple_of(slot * SUBLANES, SUBLANES)
            pltpu.make_async_copy(src, x_hbm.at[pl.ds(dst, SUBLANES)], wsem.at[s]).start()
        return carry
    lax.fori_loop(0, tm, row, 0, unroll=4)

    @pl.when(i == n_tiles - 1)
    def _():
        for j in range(max(n_tiles - (n_buf - 1), 0), n_tiles):
            wait_writes(j % n_buf)


def _dispatch(slot_tab, h2p, n_slots, fill_value, tm=128):
    t = h2p.shape[0] // SUBLANES
    n_tiles = t // tm
    tab = slot_tab.reshape(n_tiles, 1, tm * TOP_K)
    return pl.pallas_call(
        functools.partial(_dispatch_kernel, n_tiles, tm),
        grid=(n_tiles,),
        in_specs=[pl.BlockSpec((1, 1, tm * TOP_K), lambda i: (i, 0, 0), memory_space=pltpu.SMEM),
                  pl.BlockSpec(memory_space=pl.ANY), pl.BlockSpec(memory_space=pl.ANY)],
        out_specs=[pl.BlockSpec((n_slots,), lambda i: (0,), memory_space=pltpu.SMEM),
                   pl.BlockSpec(memory_space=pl.ANY)],
        out_shape=[jax.ShapeDtypeStruct((n_slots,), I32),
                   jax.ShapeDtypeStruct((n_slots * SUBLANES, LANES), U32)],
        scratch_shapes=[pltpu.VMEM((3, tm * SUBLANES, LANES), U32), pltpu.SemaphoreType.DMA((3,)),
                        pltpu.SemaphoreType.DMA((3,))],
        compiler_params=_cparams("arbitrary"),
        name="dispatch",
    )(tab, jnp.full((n_slots,), fill_value, I32), h2p)


def _experts_kernel(n_tab, t_tok, bs_ref, bn_ref, tab_hbm, x_hbm, wg_hbm, wu_hbm, wd_hbm, y_hbm,
                    xbuf, ybuf, wgf, wuf, wdf, wgb, wub, wdb, ids, gsem, ssem, isem, wsem):
    e = pl.program_id(0)
    n_exp = pl.num_programs(0)
    trash = TOP_K * t_tok
    k_shift = TOP_K.bit_length() - 1
    blk_rows = MOE_BLOCK * SUBLANES

    def weight_copies(ex):
        s = ex & 1
        return [pltpu.make_async_copy(src.at[ex], dst.at[s], wsem.at[s, i])
                for i, (src, dst) in enumerate(((wg_hbm, wgf), (wu_hbm, wuf), (wd_hbm, wdf)))]

    def ids_copy(g):
        q = g & (IDS_RING - 1)
        row = jnp.minimum(g, n_tab - 1)
        return pltpu.make_async_copy(tab_hbm.at[pl.ds(row, 1)], ids.at[pl.ds(q, 1)], isem.at[q])

    def gather_starts(g):
        s = g & (BUF_RING - 1)

        def one():
            row = pl.multiple_of(jnp.minimum(g, n_tab - 1) * blk_rows, blk_rows)
            pltpu.make_async_copy(x_hbm.at[pl.ds(row, blk_rows)], xbuf.at[s], gsem.at[s]).start()
        return [one]

    def scatter_starts(g):
        s, q = g & (BUF_RING - 1), g & (IDS_RING - 1)

        def one(r):
            a = ids[q, r]
            dst = jnp.where(a >= trash, trash + s * MOE_BLOCK + r,
                            (a & (TOP_K - 1)) * t_tok + jnp.right_shift(a, k_shift))
            out = y_hbm.at[pl.ds(pl.multiple_of(dst * SUBLANES, SUBLANES), SUBLANES)]
            pltpu.make_async_copy(ybuf.at[s, pl.ds(r * SUBLANES, SUBLANES)], out, ssem.at[s]).start()
        return [functools.partial(one, r) for r in range(MOE_BLOCK)]

    def wait_gather(s):
        pltpu.make_async_copy(x_hbm.at[pl.ds(0, blk_rows)], xbuf.at[s], gsem.at[s]).wait()

    def wait_scatter(s):
        pltpu.make_async_copy(ybuf.at[s], y_hbm.at[pl.ds(0, blk_rows)], ssem.at[s]).wait()

    @pl.when(e == 0)
    def _():
        for c in weight_copies(e):
            c.start(priority=WEIGHT_DMA_PRIORITY)
        for g0 in range(GATHER_AHEAD + 1):
            ids_copy(g0).start()
        for g0 in range(GATHER_AHEAD):
            ids_copy(g0).wait()
            for start in gather_starts(g0):
                start()
        q_fake = IDS_RING - 1
        for r in range(MOE_BLOCK):
            ids[q_fake, r] = trash
        ybuf[BUF_RING - 1] = jnp.zeros((blk_rows, LANES), U32)

    @pl.when(e + 1 < n_exp)
    def _():
        for c in weight_copies(e + 1):
            c.start(priority=WEIGHT_DMA_PRIORITY)

    for c in weight_copies(e):
        c.wait()

    @pl.when(bn_ref[e] > 0)
    def _():
        s = e & 1
        wgb[...] = wgf[s].astype(BF16)
        wub[...] = wuf[s].astype(BF16)
        wdb[...] = wdf[s].astype(BF16)

    d_exp = wgb.shape[1]
    n_gate = d_exp // N_HALF
    n_pair = wdb.shape[1] // (2 * N_HALF)
    n_phase = 2 * n_gate + 1

    def block(j, carry):
        g = bs_ref[e] + j
        s = g & (BUF_RING - 1)

        @pl.when(g >= BUF_RING - 1)
        def _():
            wait_scatter(s)

        ids_copy(g + GATHER_AHEAD).wait()
        wait_gather(s)

        starts = gather_starts(g + GATHER_AHEAD) + scatter_starts(g - 1)
        bounds = [len(starts) * p // n_phase for p in range(n_phase + 1)]
        phase = iter(range(n_phase))

        def issue():
            p = next(phase)
            for f in starts[bounds[p]:bounds[p + 1]]:
                f()

        x = _load_packed(xbuf.at[s], MOE_BLOCK).astype(BF16)
        ids_copy(g + GATHER_AHEAD + 1).start()
        acts = []
        for c in range(n_gate):
            cols = slice(c * N_HALF, (c + 1) * N_HALF)
            issue()
            gate = _bdot(x, wgb[:, cols])
            issue()
            up = _bdot(x, wub[:, cols])
            acts.append((gate * jax.nn.sigmoid(gate) * up).astype(BF16))
        act = jnp.concatenate(acts, axis=1)
        half = n_pair * N_HALF
        issue()
        for c in range(n_pair):
            y_lo = _bdot(act, wdb[:, c * N_HALF:(c + 1) * N_HALF])
            y_hi = _bdot(act, wdb[:, half + c * N_HALF:half + (c + 1) * N_HALF])
            _store_packed_cols(y_lo, y_hi, ybuf.at[s], c * (N_HALF // LANES), MOE_BLOCK)
        return carry

    lax.fori_loop(0, bn_ref[e], block, 0)

    @pl.when(e == n_exp - 1)
    def _():
        n_used = bs_ref[e] + bn_ref[e]
        for a in range(GATHER_AHEAD):
            wait_gather((n_used + a) & (BUF_RING - 1))
        ids_copy(n_used + GATHER_AHEAD).wait()
        for start in scatter_starts(n_used - 1):
            start()
        for s in range(BUF_RING):
            wait_scatter(s)
        ybuf[...] = jnp.zeros(ybuf.shape, U32)
        for s in range(BUF_RING):
            tail = pltpu.make_async_copy(
                ybuf.at[s], y_hbm.at[pl.ds((trash + s * MOE_BLOCK) * SUBLANES, blk_rows)], ssem.at[s])
            tail.start()
            tail.wait()


def _experts(blk_start, blk_count, table, x_sorted, t, wg, wu, wd):
    n_tab = table.shape[0]
    ne, d, de = wg.shape
    blk_rows = MOE_BLOCK * SUBLANES
    grid_spec = pltpu.PrefetchScalarGridSpec(
        num_scalar_prefetch=2,
        grid=(ne,),
        in_specs=[pl.BlockSpec(memory_space=pl.ANY)] * 5,
        out_specs=pl.BlockSpec(memory_space=pl.ANY),
        scratch_shapes=[pltpu.VMEM((BUF_RING, blk_rows, LANES), U32), pltpu.VMEM((BUF_RING, blk_rows, LANES), U32),
                        pltpu.VMEM((2, d, de), F32), pltpu.VMEM((2, d, de), F32), pltpu.VMEM((2, de, d), F32),
                        pltpu.VMEM((d, de), BF16), pltpu.VMEM((d, de), BF16), pltpu.VMEM((de, d), BF16),
                        pltpu.SMEM((IDS_RING, MOE_BLOCK), I32),
                        pltpu.SemaphoreType.DMA((BUF_RING,)), pltpu.SemaphoreType.DMA((BUF_RING,)),
                        pltpu.SemaphoreType.DMA((IDS_RING,)), pltpu.SemaphoreType.DMA((2, 3))],
    )
    return pl.pallas_call(
        functools.partial(_experts_kernel, n_tab, t),
        grid_spec=grid_spec,
        out_shape=jax.ShapeDtypeStruct(((TOP_K * t + TRASH_ROWS) * SUBLANES, LANES), U32),
        compiler_params=_cparams("arbitrary"),
        name="experts",
    )(blk_start, blk_count, table, x_sorted, wg, wu, wd)


def _combine_kernel(n_ptiles, *refs):
    y_refs = refs[:TOP_K]
    (wt_ref, h2_ref, x1_ref, wgs_ref, wus_ref, wds_ref, gf_ref, gtp_ref, gts_ref,
     yp_ref, ysm_ref) = refs[TOP_K:]
    i = pl.program_id(0)

    tm = h2_ref.shape[0]
    h = h2_ref[...]
    g = _bdot(h, wgs_ref[...])
    u = _bdot(h, wus_ref[...])
    f = _bdot((g * jax.nn.sigmoid(g) * u).astype(BF16), wds_ref[...])

    wt = wt_ref[...]
    routed = wt[:, 0:1] * _load_packed(y_refs[0], tm)
    for k in range(1, TOP_K):
        routed = routed + wt[:, k:k + 1] * _load_packed(y_refs[k], tm)
    f = routed + f

    @pl.when(i < n_ptiles)
    def _():
        yp_ref[...] = _rms(x1_ref[...] + gtp_ref[...] * f) * gf_ref[...]

    @pl.when(i == n_ptiles)
    def _():
        ysm_ref[...] = _rms(x1_ref[...] + gts_ref[...] * f) * gf_ref[...]


def _combine(y_planes, wts, h2, x1, wgs, wus, wds, gf, gtp, gts, n_prompt, seq, tm=128):
    t, d = h2.shape
    ts = t - n_prompt
    n_tiles = t // tm
    npt = n_prompt // tm
    per = seq // tm
    row = lambda i: (i, 0)
    full = lambda i: (0, 0)
    return pl.pallas_call(
        functools.partial(_combine_kernel, npt),
        grid=(n_tiles,),
        in_specs=[pl.BlockSpec((tm * SUBLANES, LANES), lambda i, k=k: (k * n_tiles + i, 0))
                  for k in range(TOP_K)] + [
                  pl.BlockSpec((tm, LANES), row), pl.BlockSpec((tm, d), row), pl.BlockSpec((tm, d), row),
                  _const_spec(wgs.shape), _const_spec(wus.shape), _const_spec(wds.shape),
                  pl.BlockSpec((1, d), full),
                  pl.BlockSpec((None, 1, d), lambda i: (jnp.minimum(i, npt - 1) // per, 0, 0)),
                  pl.BlockSpec((ts, d), full)],
        out_specs=[pl.BlockSpec((tm, d), lambda i: (jnp.minimum(i, npt - 1), 0)),
                   pl.BlockSpec((ts, d), full)],
        out_shape=[jax.ShapeDtypeStruct((n_prompt, d), F32), jax.ShapeDtypeStruct((ts, d), F32)],
        compiler_params=_cparams("arbitrary"),
        name="combine",
    )(*([y_planes] * TOP_K), wts, h2, x1, wgs, wus, wds, gf.reshape(1, d), gtp, gts)


def kernel(x_prompt, x_sample, state_conv, c_prompt, c_sample, w_ada, b_ada, g_norm1, w_in, a_norm_g, a_norm_b, w_s, b_s, conv_w, conv_b, b_norm_g, b_norm_b, w_a_out, w_b_out, w_o, g_norm2, w_router, router_bias, w_gate_e, w_up_e, w_down_e, w_gate_s, w_up_s, w_down_s, g_final):
    depth = w_ada.shape[0]
    assert depth == 1, "single-layer trunk"
    nb, seq, d = x_prompt.shape
    ts, dec_seq, _ = x_sample.shape
    assert dec_seq == 1
    tp = nb * seq
    t = tp + ts
    l = 0

    xp = x_prompt.reshape(tp, d)
    xs = x_sample.reshape(ts, d)

    pad = 16
    c_all = jnp.concatenate([c_prompt, jnp.zeros((pad - nb, d), F32), c_sample], axis=0)
    mod = _ada(c_all, w_ada[l], b_ada[l]).reshape(pad + ts, 6, d)
    mp = [mod[:nb, k].reshape(nb, 1, d) for k in range(6)]
    ms = [mod[pad:, k] for k in range(6)]

    h = _hmod(xp, xs, g_norm1[l], mp[1], mp[0], ms[1], ms[0], seq)
    w_in16 = w_in[l].astype(BF16)
    blk = lambda s: _col_block(w_in16, d, s)
    vec = lambda a: a.reshape(1, d)
    (u,) = _rows_call(_seg_u_kernel, h, ts, 512, [blk(0)], [], [BF16], "seg_u")
    v32, v16 = _rows_call(_seg_v_kernel, h, ts, 512, [blk(1)], [vec(a_norm_g[l]), vec(a_norm_b[l])],
                          [F32, BF16], "seg_v")
    (glu,) = _rows_call(_seg_glu_kernel, h, ts, 512, [blk(2), blk(3)], [], [F32], "seg_glu")
    ga, gb = _rows_call(_seg_gate_kernel, h, ts, 512, [blk(4), blk(5)], [], [BF16, BF16], "seg_gate")

    gdim = d // A_GROUPS
    cb, ng, nbb = vec(conv_b[l]), vec(b_norm_g[l]), vec(b_norm_b[l])
    yap, ybp = _mixp(u, v16, glu, w_s[l], b_s[l].T, conv_w[l], cb, ng, nbb, tp, seq)
    yas, ybs, conv_s = _mixs(u, v32, glu, state_conv, vec(jnp.repeat(w_s[l][:, 0, 0], gdim)),
                             vec(jnp.repeat(b_s[l][:, 0], gdim)), conv_w[l], cb, ng, nbb, tp)

    wr = w_router[l]
    wrh = wr.astype(BF16)
    wrl = (wr - wrh.astype(F32)).astype(BF16)
    x1, h2, h2p, logits = _proj(yap, ybp, yas, ybs, ga, gb, xp, xs,
                           w_a_out[l].astype(BF16), w_b_out[l].astype(BF16), w_o[l].astype(BF16),
                           wrh, wrl, g_norm2[l], mp[2], mp[4], mp[3], ms[2], ms[4], ms[3], seq)

    n_blocks = -(-(t * TOP_K) // MOE_BLOCK) + N_EXPERTS
    idx, pos, wts, cnt = _route(logits, router_bias[l])
    slot, blk_start, blk_count = _slots(cnt, idx, pos)
    slot_tab = slot[:, :TOP_K].reshape(t * TOP_K)
    inv, x_sorted = _dispatch(slot_tab, h2p, n_blocks * MOE_BLOCK, t * TOP_K)
    table = inv.reshape(n_blocks, MOE_BLOCK)
    y_planes = _experts(blk_start[0], blk_count[0], table, x_sorted, t, w_gate_e[l], w_up_e[l], w_down_e[l])
    yp, ysm = _combine(y_planes, wts, h2, x1,
                       w_gate_s[l].astype(BF16), w_up_s[l].astype(BF16), w_down_s[l].astype(BF16),
                       g_final, mp[5], ms[5], tp, seq)

    n_past = state_conv.shape[2]
    last_rows = lambda a, n: jnp.stack([a[(b + 1) * seq - n:(b + 1) * seq] for b in range(nb)])[None]
    conv_p = last_rows(glu, n_past)
    chunkv_p = last_rows(v32, CHUNK)
    chunkv_s = v32[tp:].reshape(1, ts, 1, d)
    return (yp.reshape(nb, seq, d), ysm.reshape(ts, 1, d), conv_p, conv_s, chunkv_p, chunkv_s)
```

```python
import functools

import jax
import jax.numpy as jnp
from jax import lax
from jax.experimental import pallas as pl
from jax.experimental.pallas import tpu as pltpu

F32 = jnp.float32
BF16 = jnp.bfloat16
I32 = jnp.int32

EPS = 1e-6
CHUNK = 128
A_GROUPS = 8
CONV_W = 31
N_EXPERTS = 256
TOP_K = 8
N_EXPERT_GROUPS = 8
GROUP_SIZE = N_EXPERTS // N_EXPERT_GROUPS
TOPK_GROUPS = 4
ROUTED_SCALE = 2.5
MOE_BLOCK = 128
LANES = 128
SUBLANES = 8
VMEM_LIMIT = 58 * 1024 * 1024

NEG_INF = float("-inf")


def _cparams(*sem):
    return pltpu.CompilerParams(dimension_semantics=sem, vmem_limit_bytes=VMEM_LIMIT)


def _const_spec(shape):
    nd = len(shape)
    return pl.BlockSpec(shape, lambda *_: (0,) * nd, pipeline_mode=pl.Buffered(1))


def _rms(x):
    return x * lax.rsqrt(jnp.mean(x * x, axis=-1, keepdims=True) + EPS)


def _layer_norm(x, g, b):
    mu = jnp.mean(x, axis=-1, keepdims=True)
    xc = x - mu
    var = jnp.mean(xc * xc, axis=-1, keepdims=True)
    return xc * lax.rsqrt(var + EPS) * g + b


def _bdot(a, b):
    return jnp.dot(a, b, preferred_element_type=F32)


U32 = jnp.uint32
HI_MASK = 0xFFFF0000


def _store_packed_cols(x_lo, x_hi, ref, c0, n_rows):
    for i in range(x_lo.shape[1] // LANES):
        lo = x_lo[:, i * LANES:(i + 1) * LANES].astype(BF16).astype(F32)
        hi = x_hi[:, i * LANES:(i + 1) * LANES].astype(BF16).astype(F32)
        word = (lax.bitcast_convert_type(hi, U32) & U32(HI_MASK)) | (lax.bitcast_convert_type(lo, U32) >> 16)
        ref[pl.ds(c0 + i, n_rows, stride=SUBLANES), :] = word


def _store_packed(x, ref, n_rows):
    half = x.shape[1] // 2
    assert half == SUBLANES * LANES
    _store_packed_cols(x[:, :half], x[:, half:], ref, 0, n_rows)


def _load_packed(ref, n_rows):
    lo, hi = [], []
    for c in range(SUBLANES):
        word = ref[pl.ds(c, n_rows, stride=SUBLANES), :]
        lo.append(lax.bitcast_convert_type(word << 16, F32))
        hi.append(lax.bitcast_convert_type(word & U32(HI_MASK), F32))
    return jnp.concatenate(lo + hi, axis=1)


def _lane_sum(x):
    return jnp.sum(x, axis=-1, keepdims=True)


def _lane_max(x):
    return jnp.max(x, axis=-1, keepdims=True)


def _lane_min(x):
    return jnp.min(x, axis=-1, keepdims=True)


def _ada_kernel(n_a, c_ref, w_ref, b_ref, oa_ref, ob_ref):
    c = c_ref[...]
    s = (c * jax.nn.sigmoid(c)).astype(BF16)
    m = _bdot(s, w_ref[...].astype(BF16)) + b_ref[...]
    oa_ref[...] = m[0:n_a, :]
    ob_ref[...] = m[n_a:, :]


def _ada(c_all, n_a, w_ada, b_ada, n_mod, tn=1024):
    m, d = c_all.shape
    n = w_ada.shape[1]
    per = d // tn
    out = lambda rows: pl.BlockSpec((None, rows, tn), lambda j: (j // per, 0, j % per))
    return pl.pallas_call(
        functools.partial(_ada_kernel, n_a),
        grid=(n // tn,),
        in_specs=[pl.BlockSpec((m, d), lambda j: (0, 0)),
                  pl.BlockSpec((d, tn), lambda j: (0, j)),
                  pl.BlockSpec((1, tn), lambda j: (0, j))],
        out_specs=[out(n_a), out(m - n_a)],
        out_shape=[jax.ShapeDtypeStruct((n_mod, n_a, d), F32), jax.ShapeDtypeStruct((n_mod, m - n_a, d), F32)],
        compiler_params=_cparams("arbitrary"),
        name="ada",
    )(c_all, w_ada, b_ada.reshape(1, n))


def _hmod_kernel(n_ptiles, n_s, xp_ref, xs_ref, g_ref, scp_ref, shp_ref, scs_ref, shs_ref, wu_ref,
                 o_ref, u_ref):
    i = pl.program_id(0)

    def body(rows, x, sc, sh):
        h = ((_rms(x) * g_ref[...]) * (1.0 + sc) + sh).astype(o_ref.dtype)
        o_ref[0:rows, :] = h
        u_ref[0:rows, :] = jax.nn.gelu(_bdot(h, wu_ref[...])).astype(u_ref.dtype)

    @pl.when(i < n_ptiles)
    def _():
        body(o_ref.shape[0], xp_ref[...], scp_ref[...], shp_ref[...])

    @pl.when(i == n_ptiles)
    def _():
        body(n_s, xs_ref[...], scs_ref[...], shs_ref[...])


def _hmod(xp, xs, g, scp, shp, scs, shs, w_in16, seq, tm=512):
    tp, d = xp.shape
    ts = xs.shape[0]
    npt = tp // tm
    per = seq // tm
    last = npt - 1
    pmod = pl.BlockSpec((None, 1, d), lambda i: (jnp.minimum(i, last) // per, 0, 0))
    full = lambda i: (0, 0)
    return pl.pallas_call(
        functools.partial(_hmod_kernel, npt, ts),
        grid=(npt + 1,),
        in_specs=[pl.BlockSpec((tm, d), lambda i: (jnp.minimum(i, last), 0)),
                  pl.BlockSpec((ts, d), full),
                  pl.BlockSpec((1, d), full),
                  pmod, pmod,
                  pl.BlockSpec((ts, d), full),
                  pl.BlockSpec((ts, d), full),
                  pl.BlockSpec((w_in16.shape[0], d), full, pipeline_mode=pl.Buffered(1))],
        out_specs=[pl.BlockSpec((tm, d), lambda i: (i, 0)), pl.BlockSpec((tm, d), lambda i: (i, 0))],
        out_shape=[jax.ShapeDtypeStruct((tp + ts, d), BF16), jax.ShapeDtypeStruct((tp + ts, d), BF16)],
        compiler_params=_cparams("arbitrary"),
        name="hmod",
    )(xp, xs, g.reshape(1, d), scp, shp, scs, shs, w_in16)


def _rows_call(kernel, h, n_tail, tm, weights, vecs, out_dtypes, name):
    t, d = h.shape
    n_full = (t - n_tail) // tm

    def wrapped(*refs):
        i = pl.program_id(0)

        @pl.when(i < n_full)
        def _():
            kernel(tm, *refs)

        @pl.when(i == n_full)
        def _():
            kernel(n_tail, *refs)

    in_specs = [pl.BlockSpec((tm, d), lambda i: (i, 0))]
    in_specs += [pl.BlockSpec(bs, im, pipeline_mode=pl.Buffered(1)) for (_, bs, im) in weights]
    in_specs += [pl.BlockSpec(v.shape, lambda i: (0, 0)) for v in vecs]
    return pl.pallas_call(
        wrapped,
        grid=(n_full + 1,),
        in_specs=in_specs,
        out_specs=[pl.BlockSpec((tm, d), lambda i: (i, 0)) for _ in out_dtypes],
        out_shape=[jax.ShapeDtypeStruct((t, d), dt) for dt in out_dtypes],
        compiler_params=_cparams("arbitrary"),
        name=name,
    )(h, *[w for (w, _, _) in weights], *vecs)


def _seg_v_kernel(rows, h_ref, w_ref, g_ref, b_ref, o32_ref, o16_ref):
    z = _bdot(h_ref[0:rows, :], w_ref[...])
    v = _layer_norm(jax.nn.gelu(z), g_ref[...], b_ref[...])
    o32_ref[0:rows, :] = v
    o16_ref[0:rows, :] = v.astype(o16_ref.dtype)


def _seg_glu_kernel(rows, h_ref, wa_ref, wb_ref, o_ref):
    h = h_ref[0:rows, :]
    o_ref[0:rows, :] = _bdot(h, wa_ref[...]) * jax.nn.sigmoid(_bdot(h, wb_ref[...]))


def _seg_gate_kernel(rows, h_ref, wa_ref, wb_ref, oa_ref, ob_ref):
    h = h_ref[0:rows, :]
    oa_ref[0:rows, :] = jax.nn.sigmoid(_bdot(h, wa_ref[...])).astype(oa_ref.dtype)
    ob_ref[0:rows, :] = jax.nn.sigmoid(_bdot(h, wb_ref[...])).astype(ob_ref.dtype)


def _col_block(w, d, seg):
    return (w, (w.shape[0], d), lambda i, seg=seg: (0, seg))


CONV_ROWS = 128
CONV_LANES = 128


def _mixp_kernel(tm, halo, per, u_ref, v_ref, glu_ref, prev_ref, ws_ref, bst_ref, cw_ref, cb_ref,
                 ng_ref, nb_ref, ya_ref, yb_ref, xp_ref, cv_ref):
    i = pl.program_id(0)
    d = u_ref.shape[1]
    n_chunks = tm // CHUNK
    gdim = d // A_GROUPS

    r_io = lax.broadcasted_iota(I32, (CHUNK, CHUNK), 0)
    c_io = lax.broadcasted_iota(I32, (CHUNK, CHUNK), 1)
    tril = c_io <= r_io
    for g in range(A_GROUPS):
        cs = slice(g * gdim, (g + 1) * gdim)
        wg = jnp.where(tril, ws_ref[g], 0.0).astype(BF16)
        vg = jnp.concatenate([v_ref[c * CHUNK:(c + 1) * CHUNK, cs] for c in range(n_chunks)], axis=1)
        mix = _bdot(wg, vg)
        bcol = bst_ref[:, g:g + 1]
        for c in range(n_chunks):
            rs = slice(c * CHUNK, (c + 1) * CHUNK)
            ya_ref[rs, cs] = (u_ref[rs, cs].astype(F32)
                              * (mix[:, c * gdim:(c + 1) * gdim] + bcol)).astype(ya_ref.dtype)

    xp_ref[0:halo, :] = jnp.where(i % per == 0, 0.0, prev_ref[...])
    xp_ref[halo:halo + tm, :] = glu_ref[...]
    first = halo - (CONV_W - 1)
    n_cb = d // CONV_LANES

    def chunk(idx, carry):
        r0 = pl.multiple_of((idx // n_cb) * CONV_ROWS, CONV_ROWS)
        c0 = pl.multiple_of((idx % n_cb) * CONV_LANES, CONV_LANES)
        acc = jnp.zeros((CONV_ROWS, CONV_LANES), F32)
        n_win = CONV_ROWS + halo
        window = xp_ref[pl.ds(r0, n_win), pl.ds(c0, CONV_LANES)]
        for j in range(SUBLANES):
            taps = [m for m in range(first, first + CONV_W) if m % SUBLANES == j]
            s_j = window if j == 0 else pltpu.roll(window, n_win - j, 0)
            for m in taps:
                w_row = cw_ref[pl.ds(m - first, 1), pl.ds(c0, CONV_LANES)]
                acc = acc + w_row * s_j[m - j:m - j + CONV_ROWS, :]
        cv_ref[pl.ds(r0, CONV_ROWS), pl.ds(c0, CONV_LANES)] = acc
        return carry

    lax.fori_loop(0, (tm // CONV_ROWS) * n_cb, chunk, 0)

    y = _layer_norm(cv_ref[...] + cb_ref[...], ng_ref[...], nb_ref[...])
    yb_ref[...] = (y * jax.nn.sigmoid(y)).astype(yb_ref.dtype)


def _mixp(u, v16, glu, w_s, b_s_t, conv_w, conv_b, ng, nb, n_prompt, seq, tm=512, halo=32):
    d = u.shape[1]
    npt = n_prompt // tm
    per = seq // tm
    hb = tm // halo
    row = lambda i: (i, 0)
    return pl.pallas_call(
        functools.partial(_mixp_kernel, tm, halo, per),
        grid=(npt,),
        in_specs=[pl.BlockSpec((tm, d), row),
                  pl.BlockSpec((tm, d), row),
                  pl.BlockSpec((tm, d), row),
                  pl.BlockSpec((halo, d), lambda i: (jnp.maximum(i * hb - 1, 0), 0)),
                  _const_spec(w_s.shape), _const_spec(b_s_t.shape), _const_spec(conv_w.shape),
                  _const_spec(conv_b.shape), _const_spec(ng.shape), _const_spec(nb.shape)],
        out_specs=[pl.BlockSpec((tm, d), row), pl.BlockSpec((tm, d), row)],
        out_shape=[jax.ShapeDtypeStruct((n_prompt, d), BF16), jax.ShapeDtypeStruct((n_prompt, d), BF16)],
        scratch_shapes=[pltpu.VMEM((halo + tm, d), F32), pltpu.VMEM((tm, d), F32)],
        compiler_params=_cparams("arbitrary"),
        name="mixp",
    )(u, v16, glu, glu, w_s, b_s_t, conv_w, conv_b, ng, nb)


def _mixs_kernel(u_ref, v_ref, glu_ref, st_ref, ws0_ref, bs0_ref, cw_ref, cb_ref, ng_ref, nb_ref,
                 ya_ref, yb_ref, st_out_ref):
    n_past = st_ref.shape[1]
    glu = glu_ref[...]
    state = st_ref[...]
    ya_ref[...] = (u_ref[...].astype(F32) * (ws0_ref[...] * v_ref[...] + bs0_ref[...])).astype(ya_ref.dtype)
    acc = cw_ref[n_past:n_past + 1, :] * glu + cb_ref[...]
    acc = acc + jnp.sum(state * cw_ref[0:n_past, :][None], axis=1)
    y = _layer_norm(acc, ng_ref[...], nb_ref[...])
    yb_ref[...] = (y * jax.nn.sigmoid(y)).astype(yb_ref.dtype)
    st_out_ref[:, 0:n_past - 1, :] = state[:, 1:n_past, :]
    st_out_ref[:, n_past - 1:n_past, :] = glu[:, None, :]


def _mixs(u, v32, glu, state, ws0, bs0, conv_w, conv_b, ng, nb, n_prompt, tb=32):
    _, ts, n_past, d = state.shape
    off = n_prompt // tb
    tail = lambda j: (off + j, 0)
    vec = lambda j: (0, 0)
    st_spec = pl.BlockSpec((None, tb, n_past, d), lambda j: (0, j, 0, 0))
    return pl.pallas_call(
        _mixs_kernel,
        grid=(ts // tb,),
        in_specs=[pl.BlockSpec((tb, d), tail), pl.BlockSpec((tb, d), tail), pl.BlockSpec((tb, d), tail),
                  st_spec,
                  pl.BlockSpec((1, d), vec), pl.BlockSpec((1, d), vec),
                  pl.BlockSpec(conv_w.shape, vec), pl.BlockSpec((1, d), vec),
                  pl.BlockSpec((1, d), vec), pl.BlockSpec((1, d), vec)],
        out_specs=[pl.BlockSpec((tb, d), lambda j: (j, 0)), pl.BlockSpec((tb, d), lambda j: (j, 0)), st_spec],
        out_shape=[jax.ShapeDtypeStruct((ts, d), BF16), jax.ShapeDtypeStruct((ts, d), BF16),
                   jax.ShapeDtypeStruct(state.shape, state.dtype)],
        compiler_params=_cparams("arbitrary"),
        name="mixs",
    )(u, v32, glu, state, ws0, bs0, conv_w, conv_b, ng, nb)


PROJ_ROWS = 128


def _proj_kernel(n_ptiles, tm, n_s,
                 yap_ref, ybp_ref, yas_ref, ybs_ref, ga_ref, gb_ref, xp_ref, xs_ref,
                 wa_ref, wb_ref, wo_ref, wrh_ref, wrl_ref, g2_ref,
                 gtp_ref, scp_ref, shp_ref, gts_ref, scs_ref, shs_ref,
                 x1_ref, h2_ref, h2p_ref, lg_ref):
    i = pl.program_id(0)

    def body(r0, rows, ya_ref, yb_ref, x_ref, gate, sc, sh):
        rs = slice(r0, r0 + rows)
        pa = _bdot(ya_ref[rs, :], wa_ref[...])
        pb = _bdot(yb_ref[rs, :], wb_ref[...])
        merged = ga_ref[rs, :].astype(F32) * pa + gb_ref[rs, :].astype(F32) * pb
        out = _bdot(merged.astype(BF16), wo_ref[...])
        x1 = x_ref[rs, :] + gate * out
        h2 = (_rms(x1) * g2_ref[...]) * (1.0 + sc) + sh
        hi = h2.astype(BF16)
        lo = (h2 - hi.astype(F32)).astype(BF16)
        logits = _bdot(hi, wrh_ref[...]) + (_bdot(hi, wrl_ref[...]) + _bdot(lo, wrh_ref[...]))
        x1_ref[rs, :] = x1
        h2_ref[rs, :] = hi
        _store_packed(h2, h2p_ref.at[pl.ds(r0 * SUBLANES, rows * SUBLANES)], rows)
        lg_ref[rs, :] = logits

    @pl.when(i < n_ptiles)
    def _():
        for r0 in range(0, tm, PROJ_ROWS):
            body(r0, PROJ_ROWS, yap_ref, ybp_ref, xp_ref, gtp_ref[...], scp_ref[...], shp_ref[...])

    @pl.when(i == n_ptiles)
    def _():
        body(0, n_s, yas_ref, ybs_ref, xs_ref, gts_ref[...], scs_ref[...], shs_ref[...])


def _proj(yap, ybp, yas, ybs, ga, gb, xp, xs, wa, wb, wo, wrh, wrl, g2,
          gtp, scp, shp, gts, scs, shs, seq, tm=256):
    tp, d = xp.shape
    ts = xs.shape[0]
    t = tp + ts
    ne = wrh.shape[1]
    npt = tp // tm
    per = seq // tm
    last = npt - 1
    prow = lambda i: (jnp.minimum(i, last), 0)
    row = lambda i: (i, 0)
    full = lambda i: (0, 0)
    pmod = pl.BlockSpec((None, 1, d), lambda i: (jnp.minimum(i, last) // per, 0, 0))
    smod = pl.BlockSpec((ts, d), full)
    return pl.pallas_call(
        functools.partial(_proj_kernel, npt, tm, ts),
        grid=(npt + 1,),
        in_specs=[pl.BlockSpec((tm, d), prow), pl.BlockSpec((tm, d), prow),
                  pl.BlockSpec((ts, d), full), pl.BlockSpec((ts, d), full),
                  pl.BlockSpec((tm, d), row), pl.BlockSpec((tm, d), row),
                  pl.BlockSpec((tm, d), prow), pl.BlockSpec((ts, d), full),
                  _const_spec(wa.shape), _const_spec(wb.shape), _const_spec(wo.shape),
                  _const_spec(wrh.shape), _const_spec(wrl.shape), _const_spec((1, d)),
                  pmod, pmod, pmod, smod, smod, smod],
        out_specs=[pl.BlockSpec((tm, d), row), pl.BlockSpec((tm, d), row),
                   pl.BlockSpec((tm * SUBLANES, LANES), row), pl.BlockSpec((tm, ne), row)],
        out_shape=[jax.ShapeDtypeStruct((t, d), F32), jax.ShapeDtypeStruct((t, d), BF16),
                   jax.ShapeDtypeStruct((t * SUBLANES, LANES), U32), jax.ShapeDtypeStruct((t, ne), F32)],
        compiler_params=_cparams("arbitrary"),
        name="proj",
    )(yap, ybp, yas, ybs, ga, gb, xp, xs, wa, wb, wo, wrh, wrl, g2.reshape(1, d),
      gtp, scp, shp, gts, scs, shs)


def _pack_cols(cols, lane):
    out = jnp.zeros(lane.shape, cols[0].dtype)
    for k, c in enumerate(cols):
        out = jnp.where(lane == k, c, out)
    return out


def _route_kernel(lg_ref, bias_ref, tri_ref, idx_ref, pos_ref, wt_ref, cnt_ref, carry_ref):
    i = pl.program_id(0)

    @pl.when(i == 0)
    def _():
        carry_ref[...] = jnp.zeros_like(carry_ref)

    scores = jax.nn.sigmoid(lg_ref[...])
    sel = scores + bias_ref[...]
    lane = lax.broadcasted_iota(I32, sel.shape, 1)
    lane_f = lane.astype(F32)
    grp = jnp.right_shift(lane, GROUP_SIZE.bit_length() - 1)

    gs_cols = []
    gs_full = jnp.zeros(sel.shape, F32)
    for g in range(N_EXPERT_GROUPS):
        in_g = grp == g
        xg = jnp.where(in_g, sel, NEG_INF)
        m1 = _lane_max(xg)
        n1 = _lane_sum(jnp.where(xg == m1, 1.0, 0.0))
        m2 = jnp.where(n1 >= 2.0, m1, _lane_max(jnp.where(xg < m1, xg, NEG_INF)))
        gs = m1 + m2
        gs_cols.append(gs)
        gs_full = jnp.where(in_g, gs, gs_full)

    beaten = jnp.zeros(sel.shape, F32)
    for g in range(N_EXPERT_GROUPS):
        better = (gs_cols[g] > gs_full) | ((gs_cols[g] == gs_full) & (g < grp))
        beaten = beaten + jnp.where(better, 1.0, 0.0)
    cur = jnp.where(beaten < float(TOPK_GROUPS), sel, NEG_INF)

    idx_cols, s_cols, hots = [], [], []
    picked = jnp.zeros(sel.shape, F32)
    for _ in range(TOP_K):
        m = _lane_max(cur)
        idx = _lane_min(jnp.where(cur == m, lane_f, float(N_EXPERTS)))
        hot = lane_f == idx
        idx_cols.append(idx)
        s_cols.append(_lane_sum(jnp.where(hot, scores, 0.0)))
        hots.append(hot)
        picked = jnp.where(hot, 1.0, picked)
        cur = jnp.where(hot, NEG_INF, cur)

    den = s_cols[0]
    for s in s_cols[1:]:
        den = den + s
    w_cols = [s / den * ROUTED_SCALE for s in s_cols]

    pos_full = _bdot(tri_ref[...], picked.astype(BF16)) + carry_ref[...]
    pos_cols = [_lane_sum(jnp.where(h, pos_full, 0.0)) for h in hots]
    carry_ref[...] = carry_ref[...] + jnp.sum(picked, axis=0, keepdims=True)
    cnt_ref[...] = carry_ref[...]

    lane_o = lax.broadcasted_iota(I32, idx_ref.shape, 1)
    idx_ref[...] = _pack_cols(idx_cols, lane_o).astype(I32)
    pos_ref[...] = _pack_cols(pos_cols, lane_o).astype(I32)
    wt_ref[...] = _pack_cols(w_cols, lane_o)


def _route(logits, bias, tm=640):
    t, ne = logits.shape
    tri = jnp.tril(jnp.ones((tm, tm), BF16), -1)
    row = lambda i: (i, 0)
    return pl.pallas_call(
        _route_kernel,
        grid=(t // tm,),
        in_specs=[pl.BlockSpec((tm, ne), row), pl.BlockSpec((1, ne), lambda i: (0, 0)),
                  pl.BlockSpec((tm, tm), lambda i: (0, 0))],
        out_specs=[pl.BlockSpec((tm, LANES), row), pl.BlockSpec((tm, LANES), row),
                   pl.BlockSpec((tm, LANES), row), pl.BlockSpec((1, ne), lambda i: (0, 0))],
        out_shape=[jax.ShapeDtypeStruct((t, LANES), I32), jax.ShapeDtypeStruct((t, LANES), I32),
                   jax.ShapeDtypeStruct((t, LANES), F32), jax.ShapeDtypeStruct((1, ne), F32)],
        scratch_shapes=[pltpu.VMEM((1, ne), F32)],
        compiler_params=_cparams("arbitrary"),
        name="route",
    )(logits, bias.reshape(1, ne), tri)


def _slots_kernel(cnt_ref, idx_ref, pos_ref, triu_ref, slot_ref, bs_ref, bn_ref, ps_ref):
    i = pl.program_id(0)
    ne = cnt_ref.shape[1]

    @pl.when(i == 0)
    def _():
        cnt = cnt_ref[...]
        nblk = jnp.floor((cnt + float(MOE_BLOCK - 1)) * (1.0 / MOE_BLOCK))
        pend = _bdot(jnp.broadcast_to(nblk, (SUBLANES, ne)).astype(BF16), triu_ref[...])[0:1, :]
        pstart = pend - nblk
        ps_ref[...] = pstart
        bs_ref[...] = pstart.astype(I32)
        bn_ref[...] = nblk.astype(I32)

    lane_o = lax.broadcasted_iota(I32, idx_ref.shape, 1)
    lane_e = lax.broadcasted_iota(I32, (idx_ref.shape[0], ne), 1)
    idx = idx_ref[...].astype(F32)
    pos = pos_ref[...].astype(F32)
    pstart = ps_ref[...]
    lane_ef = lane_e.astype(F32)
    cols = []
    for k in range(TOP_K):
        e_k = _lane_sum(jnp.where(lane_o == k, idx, 0.0))
        p_k = _lane_sum(jnp.where(lane_o == k, pos, 0.0))
        ps_k = _lane_sum(jnp.where(lane_ef == e_k, pstart, 0.0))
        cols.append(ps_k * float(MOE_BLOCK) + p_k)
    slot_ref[...] = _pack_cols(cols, lane_o).astype(I32)


def _slots(cnt, idx, pos, tm=640):
    t = idx.shape[0]
    ne = cnt.shape[1]
    triu = jnp.triu(jnp.ones((ne, ne), BF16))
    row = lambda i: (i, 0)
    full = lambda i: (0, 0)
    return pl.pallas_call(
        _slots_kernel,
        grid=(t // tm,),
        in_specs=[pl.BlockSpec((1, ne), full), pl.BlockSpec((tm, LANES), row),
                  pl.BlockSpec((tm, LANES), row), pl.BlockSpec((ne, ne), full)],
        out_specs=[pl.BlockSpec((tm, LANES), row), pl.BlockSpec((1, ne), full),
                   pl.BlockSpec((1, ne), full)],
        out_shape=[jax.ShapeDtypeStruct((t, LANES), I32), jax.ShapeDtypeStruct((1, ne), I32),
                   jax.ShapeDtypeStruct((1, ne), I32)],
        scratch_shapes=[pltpu.VMEM((1, ne), F32)],
        compiler_params=_cparams("arbitrary"),
        name="slots",
    )(cnt, idx, pos, triu)


BUF_RING = 4
IDS_RING = 8
GATHER_AHEAD = 2
TRASH_ROWS = BUF_RING * MOE_BLOCK
WEIGHT_DMA_PRIORITY = (1, 1, 1)
N_HALF = 256


def _dispatch_kernel(n_tiles, tm, tab_ref, fill_hbm, h_hbm, inv_ref, x_hbm, buf, rsem, wsem):
    i = pl.program_id(0)
    rows = tm * SUBLANES
    n_buf = buf.shape[0]

    @pl.when(i == 0)
    def _():
        pltpu.sync_copy(fill_hbm, inv_ref)

    def read(j):
        s = lax.rem(j, n_buf)
        return pltpu.make_async_copy(h_hbm.at[pl.ds(pl.multiple_of(j * rows, rows), rows)], buf.at[s],
                                     rsem.at[s])

    def wait_writes(s):
        for _ in range(TOP_K):
            pltpu.make_async_copy(buf.at[s], x_hbm.at[pl.ds(0, rows)], wsem.at[s]).wait()

    @pl.when(i == 0)
    def _():
        read(i).start()

    @pl.when(i >= n_buf - 1)
    def _():
        wait_writes(lax.rem(i + 1, n_buf))

    @pl.when(i + 1 < n_tiles)
    def _():
        read(i + 1).start()

    read(i).wait()
    s = lax.rem(i, n_buf)

    first_id = i * (tm * TOP_K)

    def row(r, carry):
        src = buf.at[s, pl.ds(pl.multiple_of(r * SUBLANES, SUBLANES), SUBLANES)]
        for k in range(TOP_K):
            slot = tab_ref[0, 0, r * TOP_K + k]
            inv_ref[slot] = first_id + r * TOP_K + k
            dst = pl.multiple_of(slot * SUBLANES, SUBLANES)
            pltpu.make_async_copy(src, x_hbm.at[pl.ds(dst, SUBLANES)], wsem.at[s]).start(priority=k % 2)
        return carry
    lax.fori_loop(0, tm, row, 0, unroll=4)

    @pl.when(i == n_tiles - 1)
    def _():
        for j in range(max(n_tiles - (n_buf - 1), 0), n_tiles):
            wait_writes(j % n_buf)


def _dispatch(slot_tab, h2p, n_slots, fill_value, tm=128):
    t = h2p.shape[0] // SUBLANES
    n_tiles = t // tm
    tab = slot_tab.reshape(n_tiles, 1, tm * TOP_K)
    return pl.pallas_call(
        functools.partial(_dispatch_kernel, n_tiles, tm),
        grid=(n_tiles,),
        in_specs=[pl.BlockSpec((1, 1, tm * TOP_K), lambda i: (i, 0, 0), memory_space=pltpu.SMEM),
                  pl.BlockSpec(memory_space=pl.ANY), pl.BlockSpec(memory_space=pl.ANY)],
        out_specs=[pl.BlockSpec((n_slots,), lambda i: (0,), memory_space=pltpu.SMEM),
                   pl.BlockSpec(memory_space=pl.ANY)],
        out_shape=[jax.ShapeDtypeStruct((n_slots,), I32),
                   jax.ShapeDtypeStruct((n_slots * SUBLANES, LANES), U32)],
        scratch_shapes=[pltpu.VMEM((3, tm * SUBLANES, LANES), U32), pltpu.SemaphoreType.DMA((3,)),
                        pltpu.SemaphoreType.DMA((3,))],
        compiler_params=_cparams("arbitrary"),
        name="dispatch",
    )(tab, jnp.full((n_slots,), fill_value, I32), h2p)


def _experts_kernel(n_tab, t_tok, bs_ref, bn_ref, tab_hbm, x_hbm, wg_hbm, wu_hbm, wd_hbm, y_hbm,
                    xbuf, ybuf, wgf, wuf, wdf, wgb, wub, wdb, ids, gsem, ssem, isem, wsem):
    e = pl.program_id(0)
    n_exp = pl.num_programs(0)
    trash = TOP_K * t_tok
    k_shift = TOP_K.bit_length() - 1
    blk_rows = MOE_BLOCK * SUBLANES

    def weight_copies(ex):
        s = ex & 1
        return [pltpu.make_async_copy(src.at[ex], dst.at[s], wsem.at[s, i])
                for i, (src, dst) in enumerate(((wg_hbm, wgf), (wu_hbm, wuf), (wd_hbm, wdf)))]

    def ids_copy(g):
        q = g & (IDS_RING - 1)
        row = jnp.minimum(g, n_tab - 1)
        return pltpu.make_async_copy(tab_hbm.at[pl.ds(row, 1)], ids.at[pl.ds(q, 1)], isem.at[q])

    def gather_starts(g):
        s = g & (BUF_RING - 1)

        def one():
            row = pl.multiple_of(jnp.minimum(g, n_tab - 1) * blk_rows, blk_rows)
            pltpu.make_async_copy(x_hbm.at[pl.ds(row, blk_rows)], xbuf.at[s], gsem.at[s]).start()
        return [one]

    def scatter_starts(g):
        s, q = g & (BUF_RING - 1), g & (IDS_RING - 1)

        def one(r):
            a = ids[q, r]
            dst = jnp.where(a >= trash, trash + s * MOE_BLOCK + r,
                            (a & (TOP_K - 1)) * t_tok + jnp.right_shift(a, k_shift))
            out = y_hbm.at[pl.ds(pl.multiple_of(dst * SUBLANES, SUBLANES), SUBLANES)]
            pltpu.make_async_copy(ybuf.at[s, pl.ds(r * SUBLANES, SUBLANES)], out, ssem.at[s]).start()
        return [functools.partial(one, r) for r in range(MOE_BLOCK)]

    def wait_gather(s):
        pltpu.make_async_copy(x_hbm.at[pl.ds(0, blk_rows)], xbuf.at[s], gsem.at[s]).wait()

    def wait_scatter(s):
        pltpu.make_async_copy(ybuf.at[s], y_hbm.at[pl.ds(0, blk_rows)], ssem.at[s]).wait()

    @pl.when(e == 0)
    def _():
        for i, c in enumerate(weight_copies(e)):
            c.start(priority=WEIGHT_DMA_PRIORITY[i])
        for g0 in range(GATHER_AHEAD + 1):
            ids_copy(g0).start()
        for g0 in range(GATHER_AHEAD):
            ids_copy(g0).wait()
            for start in gather_starts(g0):
                start()
        q_fake = IDS_RING - 1
        for r in range(MOE_BLOCK):
            ids[q_fake, r] = trash
        ybuf[BUF_RING - 1] = jnp.zeros((blk_rows, LANES), U32)

    @pl.when(e + 1 < n_exp)
    def _():
        for i, c in enumerate(weight_copies(e + 1)):
            c.start(priority=WEIGHT_DMA_PRIORITY[i])

    for c in weight_copies(e):
        c.wait()

    @pl.when(bn_ref[e] > 0)
    def _():
        s = e & 1
        wgb[...] = wgf[s].astype(BF16)
        wub[...] = wuf[s].astype(BF16)
        wdb[...] = wdf[s].astype(BF16)

    d_exp = wgb.shape[1]
    n_gate = d_exp // N_HALF
    n_pair = wdb.shape[1] // (2 * N_HALF)
    n_phase = 2 * n_gate + 1

    def block(j, carry):
        g = bs_ref[e] + j
        s = g & (BUF_RING - 1)

        @pl.when(g >= BUF_RING - 1)
        def _():
            wait_scatter(s)

        ids_copy(g + GATHER_AHEAD).wait()
        wait_gather(s)

        starts = gather_starts(g + GATHER_AHEAD) + scatter_starts(g - 1)
        bounds = [len(starts) * p // n_phase for p in range(n_phase + 1)]
        phase = iter(range(n_phase))

        def issue():
            p = next(phase)
            for f in starts[bounds[p]:bounds[p + 1]]:
                f()

        x = _load_packed(xbuf.at[s], MOE_BLOCK).astype(BF16)
        ids_copy(g + GATHER_AHEAD + 1).start()
        acts = []
        for c in range(n_gate):
            cols = slice(c * N_HALF, (c + 1) * N_HALF)
            issue()
            gate = _bdot(x, wgb[:, cols])
            issue()
            up = _bdot(x, wub[:, cols])
            acts.append((gate * jax.nn.sigmoid(gate) * up).astype(BF16))
        act = jnp.concatenate(acts, axis=1)
        half = n_pair * N_HALF
        issue()
        for c in range(n_pair):
            y_lo = _bdot(act, wdb[:, c * N_HALF:(c + 1) * N_HALF])
            y_hi = _bdot(act, wdb[:, half + c * N_HALF:half + (c + 1) * N_HALF])
            _store_packed_cols(y_lo, y_hi, ybuf.at[s], c * (N_HALF // LANES), MOE_BLOCK)
        return carry

    lax.fori_loop(0, bn_ref[e], block, 0)

    @pl.when(e == n_exp - 1)
    def _():
        n_used = bs_ref[e] + bn_ref[e]
        for a in range(GATHER_AHEAD):
            wait_gather((n_used + a) & (BUF_RING - 1))
        ids_copy(n_used + GATHER_AHEAD).wait()
        for start in scatter_starts(n_used - 1):
            start()
        for s in range(BUF_RING):
            wait_scatter(s)
        ybuf[...] = jnp.zeros(ybuf.shape, U32)
        for s in range(BUF_RING):
            tail = pltpu.make_async_copy(
                ybuf.at[s], y_hbm.at[pl.ds((trash + s * MOE_BLOCK) * SUBLANES, blk_rows)], ssem.at[s])
            tail.start()
            tail.wait()


def _experts(blk_start, blk_count, table, x_sorted, t, wg, wu, wd):
    n_tab = table.shape[0]
    ne, d, de = wg.shape
    blk_rows = MOE_BLOCK * SUBLANES
    grid_spec = pltpu.PrefetchScalarGridSpec(
        num_scalar_prefetch=2,
        grid=(ne,),
        in_specs=[pl.BlockSpec(memory_space=pl.ANY)] * 5,
        out_specs=pl.BlockSpec(memory_space=pl.ANY),
        scratch_shapes=[pltpu.VMEM((BUF_RING, blk_rows, LANES), U32), pltpu.VMEM((BUF_RING, blk_rows, LANES), U32),
                        pltpu.VMEM((2, d, de), F32), pltpu.VMEM((2, d, de), F32), pltpu.VMEM((2, de, d), F32),
                        pltpu.VMEM((d, de), BF16), pltpu.VMEM((d, de), BF16), pltpu.VMEM((de, d), BF16),
                        pltpu.SMEM((IDS_RING, MOE_BLOCK), I32),
                        pltpu.SemaphoreType.DMA((BUF_RING,)), pltpu.SemaphoreType.DMA((BUF_RING,)),
                        pltpu.SemaphoreType.DMA((IDS_RING,)), pltpu.SemaphoreType.DMA((2, 3))],
    )
    return pl.pallas_call(
        functools.partial(_experts_kernel, n_tab, t),
        grid_spec=grid_spec,
        out_shape=jax.ShapeDtypeStruct(((TOP_K * t + TRASH_ROWS) * SUBLANES, LANES), U32),
        compiler_params=_cparams("arbitrary"),
        name="experts",
    )(blk_start, blk_count, table, x_sorted, wg, wu, wd)


def _combine_kernel(n_ptiles, *refs):
    y_refs = refs[:TOP_K]
    (wt_ref, h2_ref, x1_ref, wgs_ref, wus_ref, wds_ref, gf_ref, gtp_ref, gts_ref,
     yp_ref, ysm_ref) = refs[TOP_K:]
    i = pl.program_id(0)

    tm = h2_ref.shape[0]
    h = h2_ref[...]
    g = _bdot(h, wgs_ref[...])
    u = _bdot(h, wus_ref[...])
    f = _bdot((g * jax.nn.sigmoid(g) * u).astype(BF16), wds_ref[...])

    wt = wt_ref[...]
    routed = wt[:, 0:1] * _load_packed(y_refs[0], tm)
    for k in range(1, TOP_K):
        routed = routed + wt[:, k:k + 1] * _load_packed(y_refs[k], tm)
    f = routed + f

    @pl.when(i < n_ptiles)
    def _():
        yp_ref[...] = _rms(x1_ref[...] + gtp_ref[...] * f) * gf_ref[...]

    @pl.when(i == n_ptiles)
    def _():
        ysm_ref[...] = _rms(x1_ref[...] + gts_ref[...] * f) * gf_ref[...]


def _combine(y_planes, wts, h2, x1, wgs, wus, wds, gf, gtp, gts, n_prompt, seq, tm=128):
    t, d = h2.shape
    ts = t - n_prompt
    n_tiles = t // tm
    npt = n_prompt // tm
    per = seq // tm
    row = lambda i: (i, 0)
    full = lambda i: (0, 0)
    return pl.pallas_call(
        functools.partial(_combine_kernel, npt),
        grid=(n_tiles,),
        in_specs=[pl.BlockSpec((tm * SUBLANES, LANES), lambda i, k=k: (k * n_tiles + i, 0))
                  for k in range(TOP_K)] + [
                  pl.BlockSpec((tm, LANES), row), pl.BlockSpec((tm, d), row), pl.BlockSpec((tm, d), row),
                  _const_spec(wgs.shape), _const_spec(wus.shape), _const_spec(wds.shape),
                  pl.BlockSpec((1, d), full),
                  pl.BlockSpec((None, 1, d), lambda i: (jnp.minimum(i, npt - 1) // per, 0, 0)),
                  pl.BlockSpec((ts, d), full)],
        out_specs=[pl.BlockSpec((tm, d), lambda i: (jnp.minimum(i, npt - 1), 0)),
                   pl.BlockSpec((ts, d), full)],
        out_shape=[jax.ShapeDtypeStruct((n_prompt, d), F32), jax.ShapeDtypeStruct((ts, d), F32)],
        compiler_params=_cparams("arbitrary"),
        name="combine",
    )(*([y_planes] * TOP_K), wts, h2, x1, wgs, wus, wds, gf.reshape(1, d), gtp, gts)


def kernel(x_prompt, x_sample, state_conv, c_prompt, c_sample, w_ada, b_ada, g_norm1, w_in, a_norm_g, a_norm_b, w_s, b_s, conv_w, conv_b, b_norm_g, b_norm_b, w_a_out, w_b_out, w_o, g_norm2, w_router, router_bias, w_gate_e, w_up_e, w_down_e, w_gate_s, w_up_s, w_down_s, g_final):
    depth = w_ada.shape[0]
    assert depth == 1, "single-layer trunk"
    nb, seq, d = x_prompt.shape
    ts, dec_seq, _ = x_sample.shape
    assert dec_seq == 1
    tp = nb * seq
    t = tp + ts
    l = 0

    xp = x_prompt.reshape(tp, d)
    xs = x_sample.reshape(ts, d)

    pad = 16
    n_mod = w_ada.shape[2] // d
    c_all = jnp.concatenate([c_sample, c_prompt, jnp.zeros((pad - nb, d), F32)], axis=0)
    mod_s, mod_p = _ada(c_all, ts, w_ada[l], b_ada[l], n_mod)
    mp = [mod_p[k, :nb].reshape(nb, 1, d) for k in range(n_mod)]
    ms = [mod_s[k] for k in range(n_mod)]

    w_in16 = w_in[l].astype(BF16)
    h, u = _hmod(xp, xs, g_norm1[l], mp[1], mp[0], ms[1], ms[0], w_in16, seq)
    blk = lambda s: _col_block(w_in16, d, s)
    vec = lambda a: a.reshape(1, d)
    v32, v16 = _rows_call(_seg_v_kernel, h, ts, 512, [blk(1)], [vec(a_norm_g[l]), vec(a_norm_b[l])],
                          [F32, BF16], "seg_v")
    (glu,) = _rows_call(_seg_glu_kernel, h, ts, 512, [blk(2), blk(3)], [], [F32], "seg_glu")
    ga, gb = _rows_call(_seg_gate_kernel, h, ts, 512, [blk(4), blk(5)], [], [BF16, BF16], "seg_gate")

    gdim = d // A_GROUPS
    cb, ng, nbb = vec(conv_b[l]), vec(b_norm_g[l]), vec(b_norm_b[l])
    yap, ybp = _mixp(u, v16, glu, w_s[l], b_s[l].T, conv_w[l], cb, ng, nbb, tp, seq)
    yas, ybs, conv_s = _mixs(u, v32, glu, state_conv, vec(jnp.repeat(w_s[l][:, 0, 0], gdim)),
                             vec(jnp.repeat(b_s[l][:, 0], gdim)), conv_w[l], cb, ng, nbb, tp)

    wr = w_router[l]
    wrh = wr.astype(BF16)
    wrl = (wr - wrh.astype(F32)).astype(BF16)
    x1, h2, h2p, logits = _proj(yap, ybp, yas, ybs, ga, gb, xp, xs,
                           w_a_out[l].astype(BF16), w_b_out[l].astype(BF16), w_o[l].astype(BF16),
                           wrh, wrl, g_norm2[l], mp[2], mp[4], mp[3], ms[2], ms[4], ms[3], seq)

    n_blocks = -(-(t * TOP_K) // MOE_BLOCK) + N_EXPERTS
    idx, pos, wts, cnt = _route(logits, router_bias[l])
    slot, blk_start, blk_count = _slots(cnt, idx, pos)
    slot_tab = slot[:, :TOP_K].reshape(t * TOP_K)
    inv, x_sorted = _dispatch(slot_tab, h2p, n_blocks * MOE_BLOCK, t * TOP_K)
    table = inv.reshape(n_blocks, MOE_BLOCK)
    y_planes = _experts(blk_start[0], blk_count[0], table, x_sorted, t, w_gate_e[l], w_up_e[l], w_down_e[l])
    yp, ysm = _combine(y_planes, wts, h2, x1,
                       w_gate_s[l].astype(BF16), w_up_s[l].astype(BF16), w_down_s[l].astype(BF16),
                       g_final, mp[5], ms[5], tp, seq)

    n_past = state_conv.shape[2]
    last_rows = lambda a, n: jnp.stack([a[(b + 1) * seq - n:(b + 1) * seq] for b in range(nb)])[None]
    conv_p = last_rows(glu, n_past)
    chunkv_p = last_rows(v32, CHUNK)
    chunkv_s = v32[tp:].reshape(1, ts, 1, d)
    return (yp.reshape(nb, seq, d), ysm.reshape(ts, 1, d), conv_p, conv_s, chunkv_p, chunkv_s)
```

```python
import functools

import jax
import jax.numpy as jnp
from jax import lax
from jax.experimental import pallas as pl
from jax.experimental.pallas import tpu as pltpu

F32 = jnp.float32
BF16 = jnp.bfloat16
I32 = jnp.int32

EPS = 1e-6
CHUNK = 128
A_GROUPS = 8
CONV_W = 31
N_EXPERTS = 256
TOP_K = 8
N_EXPERT_GROUPS = 8
GROUP_SIZE = N_EXPERTS // N_EXPERT_GROUPS
TOPK_GROUPS = 4
ROUTED_SCALE = 2.5
MOE_BLOCK = 128
LANES = 128
SUBLANES = 8
VMEM_LIMIT = 58 * 1024 * 1024

NEG_INF = float("-inf")


def _cparams(*sem):
    return pltpu.CompilerParams(dimension_semantics=sem, vmem_limit_bytes=VMEM_LIMIT)


def _const_spec(shape):
    nd = len(shape)
    return pl.BlockSpec(shape, lambda *_: (0,) * nd, pipeline_mode=pl.Buffered(1))


def _rms(x):
    return x * lax.rsqrt(jnp.mean(x * x, axis=-1, keepdims=True) + EPS)


def _layer_norm(x, g, b):
    mu = jnp.mean(x, axis=-1, keepdims=True)
    xc = x - mu
    var = jnp.mean(xc * xc, axis=-1, keepdims=True)
    return xc * lax.rsqrt(var + EPS) * g + b


def _bdot(a, b):
    return jnp.dot(a, b, preferred_element_type=F32)


U32 = jnp.uint32
HI_MASK = 0xFFFF0000


def _store_packed_cols(x_lo, x_hi, ref, c0, n_rows):
    for i in range(x_lo.shape[1] // LANES):
        lo = x_lo[:, i * LANES:(i + 1) * LANES].astype(BF16).astype(F32)
        hi = x_hi[:, i * LANES:(i + 1) * LANES].astype(BF16).astype(F32)
        word = (lax.bitcast_convert_type(hi, U32) & U32(HI_MASK)) | (lax.bitcast_convert_type(lo, U32) >> 16)
        ref[pl.ds(c0 + i, n_rows, stride=SUBLANES), :] = word


def _store_packed(x, ref, n_rows):
    half = x.shape[1] // 2
    assert half == SUBLANES * LANES
    _store_packed_cols(x[:, :half], x[:, half:], ref, 0, n_rows)


def _load_packed(ref, n_rows):
    lo, hi = [], []
    for c in range(SUBLANES):
        word = ref[pl.ds(c, n_rows, stride=SUBLANES), :]
        lo.append(lax.bitcast_convert_type(word << 16, F32))
        hi.append(lax.bitcast_convert_type(word & U32(HI_MASK), F32))
    return jnp.concatenate(lo + hi, axis=1)


def _lane_sum(x):
    return jnp.sum(x, axis=-1, keepdims=True)


def _lane_max(x):
    return jnp.max(x, axis=-1, keepdims=True)


def _lane_min(x):
    return jnp.min(x, axis=-1, keepdims=True)


def _ada_kernel(n_a, c_ref, w_ref, b_ref, oa_ref, ob_ref):
    c = c_ref[...]
    s = (c * jax.nn.sigmoid(c)).astype(BF16)
    m = _bdot(s, w_ref[...].astype(BF16)) + b_ref[...]
    oa_ref[...] = m[0:n_a, :]
    ob_ref[...] = m[n_a:, :]


def _ada(c_all, n_a, w_ada, b_ada, n_mod, tn=1024):
    m, d = c_all.shape
    n = w_ada.shape[1]
    per = d // tn
    out = lambda rows: pl.BlockSpec((None, rows, tn), lambda j: (j // per, 0, j % per))
    return pl.pallas_call(
        functools.partial(_ada_kernel, n_a),
        grid=(n // tn,),
        in_specs=[pl.BlockSpec((m, d), lambda j: (0, 0)),
                  pl.BlockSpec((d, tn), lambda j: (0, j)),
                  pl.BlockSpec((1, tn), lambda j: (0, j))],
        out_specs=[out(n_a), out(m - n_a)],
        out_shape=[jax.ShapeDtypeStruct((n_mod, n_a, d), F32), jax.ShapeDtypeStruct((n_mod, m - n_a, d), F32)],
        compiler_params=_cparams("arbitrary"),
        name="ada",
    )(c_all, w_ada, b_ada.reshape(1, n))


def _hmod_kernel(n_ptiles, n_s, xp_ref, xs_ref, g_ref, scp_ref, shp_ref, scs_ref, shs_ref, wu_ref,
                 o_ref, u_ref):
    i = pl.program_id(0)

    def body(rows, x, sc, sh):
        h = ((_rms(x) * g_ref[...]) * (1.0 + sc) + sh).astype(o_ref.dtype)
        o_ref[0:rows, :] = h
        u_ref[0:rows, :] = jax.nn.gelu(_bdot(h, wu_ref[...])).astype(u_ref.dtype)

    @pl.when(i < n_ptiles)
    def _():
        body(o_ref.shape[0], xp_ref[...], scp_ref[...], shp_ref[...])

    @pl.when(i == n_ptiles)
    def _():
        body(n_s, xs_ref[...], scs_ref[...], shs_ref[...])


def _hmod(xp, xs, g, scp, shp, scs, shs, w_in16, seq, tm=512):
    tp, d = xp.shape
    ts = xs.shape[0]
    npt = tp // tm
    per = seq // tm
    last = npt - 1
    pmod = pl.BlockSpec((None, 1, d), lambda i: (jnp.minimum(i, last) // per, 0, 0))
    full = lambda i: (0, 0)
    return pl.pallas_call(
        functools.partial(_hmod_kernel, npt, ts),
        grid=(npt + 1,),
        in_specs=[pl.BlockSpec((tm, d), lambda i: (jnp.minimum(i, last), 0)),
                  pl.BlockSpec((ts, d), full),
                  pl.BlockSpec((1, d), full),
                  pmod, pmod,
                  pl.BlockSpec((ts, d), full),
                  pl.BlockSpec((ts, d), full),
                  pl.BlockSpec((w_in16.shape[0], d), full, pipeline_mode=pl.Buffered(1))],
        out_specs=[pl.BlockSpec((tm, d), lambda i: (i, 0)), pl.BlockSpec((tm, d), lambda i: (i, 0))],
        out_shape=[jax.ShapeDtypeStruct((tp + ts, d), BF16), jax.ShapeDtypeStruct((tp + ts, d), BF16)],
        compiler_params=_cparams("arbitrary"),
        name="hmod",
    )(xp, xs, g.reshape(1, d), scp, shp, scs, shs, w_in16)


def _rows_call(kernel, h, n_tail, tm, weights, vecs, out_dtypes, name):
    t, d = h.shape
    n_full = (t - n_tail) // tm

    def wrapped(*refs):
        i = pl.program_id(0)

        @pl.when(i < n_full)
        def _():
            kernel(tm, *refs)

        @pl.when(i == n_full)
        def _():
            kernel(n_tail, *refs)

    in_specs = [pl.BlockSpec((tm, d), lambda i: (i, 0))]
    in_specs += [pl.BlockSpec(bs, im, pipeline_mode=pl.Buffered(1)) for (_, bs, im) in weights]
    in_specs += [pl.BlockSpec(v.shape, lambda i: (0, 0)) for v in vecs]
    return pl.pallas_call(
        wrapped,
        grid=(n_full + 1,),
        in_specs=in_specs,
        out_specs=[pl.BlockSpec((tm, d), lambda i: (i, 0)) for _ in out_dtypes],
        out_shape=[jax.ShapeDtypeStruct((t, d), dt) for dt in out_dtypes],
        compiler_params=_cparams("arbitrary"),
        name=name,
    )(h, *[w for (w, _, _) in weights], *vecs)


def _seg_v_kernel(rows, h_ref, w_ref, g_ref, b_ref, o32_ref, o16_ref):
    z = _bdot(h_ref[0:rows, :], w_ref[...])
    v = _layer_norm(jax.nn.gelu(z), g_ref[...], b_ref[...])
    o32_ref[0:rows, :] = v
    o16_ref[0:rows, :] = v.astype(o16_ref.dtype)


def _seg_glu_kernel(rows, h_ref, wa_ref, wb_ref, o_ref):
    h = h_ref[0:rows, :]
    o_ref[0:rows, :] = _bdot(h, wa_ref[...]) * jax.nn.sigmoid(_bdot(h, wb_ref[...]))


def _seg_gate_kernel(rows, h_ref, wa_ref, wb_ref, oa_ref, ob_ref):
    h = h_ref[0:rows, :]
    oa_ref[0:rows, :] = jax.nn.sigmoid(_bdot(h, wa_ref[...])).astype(oa_ref.dtype)
    ob_ref[0:rows, :] = jax.nn.sigmoid(_bdot(h, wb_ref[...])).astype(ob_ref.dtype)


def _col_block(w, d, seg):
    return (w, (w.shape[0], d), lambda i, seg=seg: (0, seg))


CONV_ROWS = 128
CONV_LANES = 128


def _mixp_kernel(tm, halo, per, u_ref, v_ref, glu_ref, prev_ref, ws_ref, bst_ref, cw_ref, cb_ref,
                 ng_ref, nb_ref, ya_ref, yb_ref, xp_ref, cv_ref):
    i = pl.program_id(0)
    d = u_ref.shape[1]
    n_chunks = tm // CHUNK
    gdim = d // A_GROUPS

    r_io = lax.broadcasted_iota(I32, (CHUNK, CHUNK), 0)
    c_io = lax.broadcasted_iota(I32, (CHUNK, CHUNK), 1)
    tril = c_io <= r_io
    for g in range(A_GROUPS):
        cs = slice(g * gdim, (g + 1) * gdim)
        wg = jnp.where(tril, ws_ref[g], 0.0).astype(BF16)
        vg = jnp.concatenate([v_ref[c * CHUNK:(c + 1) * CHUNK, cs] for c in range(n_chunks)], axis=1)
        mix = _bdot(wg, vg)
        bcol = bst_ref[:, g:g + 1]
        for c in range(n_chunks):
            rs = slice(c * CHUNK, (c + 1) * CHUNK)
            ya_ref[rs, cs] = (u_ref[rs, cs].astype(F32)
                              * (mix[:, c * gdim:(c + 1) * gdim] + bcol)).astype(ya_ref.dtype)

    xp_ref[0:halo, :] = jnp.where(i % per == 0, 0.0, prev_ref[...])
    xp_ref[halo:halo + tm, :] = glu_ref[...]
    first = halo - (CONV_W - 1)
    n_cb = d // CONV_LANES

    def chunk(idx, carry):
        r0 = pl.multiple_of((idx // n_cb) * CONV_ROWS, CONV_ROWS)
        c0 = pl.multiple_of((idx % n_cb) * CONV_LANES, CONV_LANES)
        acc = jnp.zeros((CONV_ROWS, CONV_LANES), F32)
        n_win = CONV_ROWS + halo
        window = xp_ref[pl.ds(r0, n_win), pl.ds(c0, CONV_LANES)]
        for j in range(SUBLANES):
            taps = [m for m in range(first, first + CONV_W) if m % SUBLANES == j]
            s_j = window if j == 0 else pltpu.roll(window, n_win - j, 0)
            for m in taps:
                w_row = cw_ref[pl.ds(m - first, 1), pl.ds(c0, CONV_LANES)]
                acc = acc + w_row * s_j[m - j:m - j + CONV_ROWS, :]
        cv_ref[pl.ds(r0, CONV_ROWS), pl.ds(c0, CONV_LANES)] = acc
        return carry

    lax.fori_loop(0, (tm // CONV_ROWS) * n_cb, chunk, 0)

    y = _layer_norm(cv_ref[...] + cb_ref[...], ng_ref[...], nb_ref[...])
    yb_ref[...] = (y * jax.nn.sigmoid(y)).astype(yb_ref.dtype)


def _mixp(u, v16, glu, w_s, b_s_t, conv_w, conv_b, ng, nb, n_prompt, seq, tm=512, halo=32):
    d = u.shape[1]
    npt = n_prompt // tm
    per = seq // tm
    hb = tm // halo
    row = lambda i: (i, 0)
    return pl.pallas_call(
        functools.partial(_mixp_kernel, tm, halo, per),
        grid=(npt,),
        in_specs=[pl.BlockSpec((tm, d), row),
                  pl.BlockSpec((tm, d), row),
                  pl.BlockSpec((tm, d), row),
                  pl.BlockSpec((halo, d), lambda i: (jnp.maximum(i * hb - 1, 0), 0)),
                  _const_spec(w_s.shape), _const_spec(b_s_t.shape), _const_spec(conv_w.shape),
                  _const_spec(conv_b.shape), _const_spec(ng.shape), _const_spec(nb.shape)],
        out_specs=[pl.BlockSpec((tm, d), row), pl.BlockSpec((tm, d), row)],
        out_shape=[jax.ShapeDtypeStruct((n_prompt, d), BF16), jax.ShapeDtypeStruct((n_prompt, d), BF16)],
        scratch_shapes=[pltpu.VMEM((halo + tm, d), F32), pltpu.VMEM((tm, d), F32)],
        compiler_params=_cparams("arbitrary"),
        name="mixp",
    )(u, v16, glu, glu, w_s, b_s_t, conv_w, conv_b, ng, nb)


def _mixs_kernel(u_ref, v_ref, glu_ref, st_ref, ws0_ref, bs0_ref, cw_ref, cb_ref, ng_ref, nb_ref,
                 ya_ref, yb_ref, st_out_ref):
    n_past = st_ref.shape[0]
    glu = glu_ref[...]
    ya_ref[...] = (u_ref[...].astype(F32) * (ws0_ref[...] * v_ref[...] + bs0_ref[...])).astype(ya_ref.dtype)
    acc = cw_ref[n_past:n_past + 1, :] * glu + cb_ref[...]
    for k in range(n_past):
        acc = acc + cw_ref[k:k + 1, :] * st_ref[k]
    y = _layer_norm(acc, ng_ref[...], nb_ref[...])
    yb_ref[...] = (y * jax.nn.sigmoid(y)).astype(yb_ref.dtype)
    for k in range(n_past - 1):
        st_out_ref[k] = st_ref[k + 1]
    st_out_ref[n_past - 1] = glu


def _mixs(u, v32, glu, state_t, ws0, bs0, conv_w, conv_b, ng, nb, n_prompt, tb=32):
    n_past, ts, d = state_t.shape
    off = n_prompt // tb
    tail = lambda j: (off + j, 0)
    vec = lambda j: (0, 0)
    st_spec = pl.BlockSpec((n_past, tb, d), lambda j: (0, j, 0))
    return pl.pallas_call(
        _mixs_kernel,
        grid=(ts // tb,),
        in_specs=[pl.BlockSpec((tb, d), tail), pl.BlockSpec((tb, d), tail), pl.BlockSpec((tb, d), tail),
                  st_spec,
                  pl.BlockSpec((1, d), vec), pl.BlockSpec((1, d), vec),
                  pl.BlockSpec(conv_w.shape, vec), pl.BlockSpec((1, d), vec),
                  pl.BlockSpec((1, d), vec), pl.BlockSpec((1, d), vec)],
        out_specs=[pl.BlockSpec((tb, d), lambda j: (j, 0)), pl.BlockSpec((tb, d), lambda j: (j, 0)), st_spec],
        out_shape=[jax.ShapeDtypeStruct((ts, d), BF16), jax.ShapeDtypeStruct((ts, d), BF16),
                   jax.ShapeDtypeStruct(state_t.shape, state_t.dtype)],
        compiler_params=_cparams("arbitrary"),
        name="mixs",
    )(u, v32, glu, state_t, ws0, bs0, conv_w, conv_b, ng, nb)


PROJ_ROWS = 128


def _proj_kernel(n_ptiles, tm, n_s,
                 yap_ref, ybp_ref, yas_ref, ybs_ref, ga_ref, gb_ref, xp_ref, xs_ref,
                 wa_ref, wb_ref, wo_ref, wrh_ref, wrl_ref, g2_ref,
                 gtp_ref, scp_ref, shp_ref, gts_ref, scs_ref, shs_ref,
                 x1_ref, h2_ref, h2p_ref, lg_ref):
    i = pl.program_id(0)

    def body(r0, rows, ya_ref, yb_ref, x_ref, gate, sc, sh):
        rs = slice(r0, r0 + rows)
        pa = _bdot(ya_ref[rs, :], wa_ref[...])
        pb = _bdot(yb_ref[rs, :], wb_ref[...])
        merged = ga_ref[rs, :].astype(F32) * pa + gb_ref[rs, :].astype(F32) * pb
        out = _bdot(merged.astype(BF16), wo_ref[...])
        x1 = x_ref[rs, :] + gate * out
        h2 = (_rms(x1) * g2_ref[...]) * (1.0 + sc) + sh
        hi = h2.astype(BF16)
        lo = (h2 - hi.astype(F32)).astype(BF16)
        logits = _bdot(hi, wrh_ref[...]) + (_bdot(hi, wrl_ref[...]) + _bdot(lo, wrh_ref[...]))
        x1_ref[rs, :] = x1
        h2_ref[rs, :] = hi
        _store_packed(h2, h2p_ref.at[pl.ds(r0 * SUBLANES, rows * SUBLANES)], rows)
        lg_ref[rs, :] = logits

    @pl.when(i < n_ptiles)
    def _():
        for r0 in range(0, tm, PROJ_ROWS):
            body(r0, PROJ_ROWS, yap_ref, ybp_ref, xp_ref, gtp_ref[...], scp_ref[...], shp_ref[...])

    @pl.when(i == n_ptiles)
    def _():
        body(0, n_s, yas_ref, ybs_ref, xs_ref, gts_ref[...], scs_ref[...], shs_ref[...])


def _proj(yap, ybp, yas, ybs, ga, gb, xp, xs, wa, wb, wo, wrh, wrl, g2,
          gtp, scp, shp, gts, scs, shs, seq, tm=256):
    tp, d = xp.shape
    ts = xs.shape[0]
    t = tp + ts
    ne = wrh.shape[1]
    npt = tp // tm
    per = seq // tm
    last = npt - 1
    prow = lambda i: (jnp.minimum(i, last), 0)
    row = lambda i: (i, 0)
    full = lambda i: (0, 0)
    pmod = pl.BlockSpec((None, 1, d), lambda i: (jnp.minimum(i, last) // per, 0, 0))
    smod = pl.BlockSpec((ts, d), full)
    return pl.pallas_call(
        functools.partial(_proj_kernel, npt, tm, ts),
        grid=(npt + 1,),
        in_specs=[pl.BlockSpec((tm, d), prow), pl.BlockSpec((tm, d), prow),
                  pl.BlockSpec((ts, d), full), pl.BlockSpec((ts, d), full),
                  pl.BlockSpec((tm, d), row), pl.BlockSpec((tm, d), row),
                  pl.BlockSpec((tm, d), prow), pl.BlockSpec((ts, d), full),
                  _const_spec(wa.shape), _const_spec(wb.shape), _const_spec(wo.shape),
                  _const_spec(wrh.shape), _const_spec(wrl.shape), _const_spec((1, d)),
                  pmod, pmod, pmod, smod, smod, smod],
        out_specs=[pl.BlockSpec((tm, d), row), pl.BlockSpec((tm, d), row),
                   pl.BlockSpec((tm * SUBLANES, LANES), row), pl.BlockSpec((tm, ne), row)],
        out_shape=[jax.ShapeDtypeStruct((t, d), F32), jax.ShapeDtypeStruct((t, d), BF16),
                   jax.ShapeDtypeStruct((t * SUBLANES, LANES), U32), jax.ShapeDtypeStruct((t, ne), F32)],
        compiler_params=_cparams("arbitrary"),
        name="proj",
    )(yap, ybp, yas, ybs, ga, gb, xp, xs, wa, wb, wo, wrh, wrl, g2.reshape(1, d),
      gtp, scp, shp, gts, scs, shs)


def _pack_cols(cols, lane):
    out = jnp.zeros(lane.shape, cols[0].dtype)
    for k, c in enumerate(cols):
        out = jnp.where(lane == k, c, out)
    return out


def _route_kernel(lg_ref, bias_ref, tri_ref, idx_ref, pos_ref, wt_ref, cnt_ref, carry_ref):
    i = pl.program_id(0)

    @pl.when(i == 0)
    def _():
        carry_ref[...] = jnp.zeros_like(carry_ref)

    scores = jax.nn.sigmoid(lg_ref[...])
    sel = scores + bias_ref[...]
    lane = lax.broadcasted_iota(I32, sel.shape, 1)
    lane_f = lane.astype(F32)
    grp = jnp.right_shift(lane, GROUP_SIZE.bit_length() - 1)

    gs_cols = []
    gs_full = jnp.zeros(sel.shape, F32)
    for g in range(N_EXPERT_GROUPS):
        in_g = grp == g
        xg = jnp.where(in_g, sel, NEG_INF)
        m1 = _lane_max(xg)
        n1 = _lane_sum(jnp.where(xg == m1, 1.0, 0.0))
        m2 = jnp.where(n1 >= 2.0, m1, _lane_max(jnp.where(xg < m1, xg, NEG_INF)))
        gs = m1 + m2
        gs_cols.append(gs)
        gs_full = jnp.where(in_g, gs, gs_full)

    beaten = jnp.zeros(sel.shape, F32)
    for g in range(N_EXPERT_GROUPS):
        better = (gs_cols[g] > gs_full) | ((gs_cols[g] == gs_full) & (g < grp))
        beaten = beaten + jnp.where(better, 1.0, 0.0)
    cur = jnp.where(beaten < float(TOPK_GROUPS), sel, NEG_INF)

    idx_cols, s_cols, hots = [], [], []
    picked = jnp.zeros(sel.shape, F32)
    for _ in range(TOP_K):
        m = _lane_max(cur)
        idx = _lane_min(jnp.where(cur == m, lane_f, float(N_EXPERTS)))
        hot = lane_f == idx
        idx_cols.append(idx)
        s_cols.append(_lane_sum(jnp.where(hot, scores, 0.0)))
        hots.append(hot)
        picked = jnp.where(hot, 1.0, picked)
        cur = jnp.where(hot, NEG_INF, cur)

    den = s_cols[0]
    for s in s_cols[1:]:
        den = den + s
    w_cols = [s / den * ROUTED_SCALE for s in s_cols]

    pos_full = _bdot(tri_ref[...], picked.astype(BF16)) + carry_ref[...]
    pos_cols = [_lane_sum(jnp.where(h, pos_full, 0.0)) for h in hots]
    carry_ref[...] = carry_ref[...] + jnp.sum(picked, axis=0, keepdims=True)
    cnt_ref[...] = carry_ref[...]

    lane_o = lax.broadcasted_iota(I32, idx_ref.shape, 1)
    idx_ref[...] = _pack_cols(idx_cols, lane_o).astype(I32)
    pos_ref[...] = _pack_cols(pos_cols, lane_o).astype(I32)
    wt_ref[...] = _pack_cols(w_cols, lane_o)


def _route(logits, bias, tm=640):
    t, ne = logits.shape
    tri = jnp.tril(jnp.ones((tm, tm), BF16), -1)
    row = lambda i: (i, 0)
    return pl.pallas_call(
        _route_kernel,
        grid=(t // tm,),
        in_specs=[pl.BlockSpec((tm, ne), row), pl.BlockSpec((1, ne), lambda i: (0, 0)),
                  pl.BlockSpec((tm, tm), lambda i: (0, 0))],
        out_specs=[pl.BlockSpec((tm, LANES), row), pl.BlockSpec((tm, LANES), row),
                   pl.BlockSpec((tm, LANES), row), pl.BlockSpec((1, ne), lambda i: (0, 0))],
        out_shape=[jax.ShapeDtypeStruct((t, LANES), I32), jax.ShapeDtypeStruct((t, LANES), I32),
                   jax.ShapeDtypeStruct((t, LANES), F32), jax.ShapeDtypeStruct((1, ne), F32)],
        scratch_shapes=[pltpu.VMEM((1, ne), F32)],
        compiler_params=_cparams("arbitrary"),
        name="route",
    )(logits, bias.reshape(1, ne), tri)


def _slots_kernel(cnt_ref, idx_ref, pos_ref, triu_ref, slot_ref, bs_ref, bn_ref, ps_ref):
    i = pl.program_id(0)
    ne = cnt_ref.shape[1]

    @pl.when(i == 0)
    def _():
        cnt = cnt_ref[...]
        nblk = jnp.floor((cnt + float(MOE_BLOCK - 1)) * (1.0 / MOE_BLOCK))
        pend = _bdot(jnp.broadcast_to(nblk, (SUBLANES, ne)).astype(BF16), triu_ref[...])[0:1, :]
        pstart = pend - nblk
        ps_ref[...] = pstart
        bs_ref[...] = pstart.astype(I32)
        bn_ref[...] = nblk.astype(I32)

    lane_o = lax.broadcasted_iota(I32, idx_ref.shape, 1)
    lane_e = lax.broadcasted_iota(I32, (idx_ref.shape[0], ne), 1)
    idx = idx_ref[...].astype(F32)
    pos = pos_ref[...].astype(F32)
    pstart = ps_ref[...]
    lane_ef = lane_e.astype(F32)
    cols = []
    for k in range(TOP_K):
        e_k = _lane_sum(jnp.where(lane_o == k, idx, 0.0))
        p_k = _lane_sum(jnp.where(lane_o == k, pos, 0.0))
        ps_k = _lane_sum(jnp.where(lane_ef == e_k, pstart, 0.0))
        cols.append(ps_k * float(MOE_BLOCK) + p_k)
    slot_ref[...] = _pack_cols(cols, lane_o).astype(I32)


def _slots(cnt, idx, pos, tm=640):
    t = idx.shape[0]
    ne = cnt.shape[1]
    triu = jnp.triu(jnp.ones((ne, ne), BF16))
    row = lambda i: (i, 0)
    full = lambda i: (0, 0)
    return pl.pallas_call(
        _slots_kernel,
        grid=(t // tm,),
        in_specs=[pl.BlockSpec((1, ne), full), pl.BlockSpec((tm, LANES), row),
                  pl.BlockSpec((tm, LANES), row), pl.BlockSpec((ne, ne), full)],
        out_specs=[pl.BlockSpec((tm, LANES), row), pl.BlockSpec((1, ne), full),
                   pl.BlockSpec((1, ne), full)],
        out_shape=[jax.ShapeDtypeStruct((t, LANES), I32), jax.ShapeDtypeStruct((1, ne), I32),
                   jax.ShapeDtypeStruct((1, ne), I32)],
        scratch_shapes=[pltpu.VMEM((1, ne), F32)],
        compiler_params=_cparams("arbitrary"),
        name="slots",
    )(cnt, idx, pos, triu)


BUF_RING = 4
IDS_RING = 8
GATHER_AHEAD = 2
TRASH_ROWS = BUF_RING * MOE_BLOCK
WEIGHT_DMA_PRIORITY = 1
N_HALF = 256


def _dispatch_kernel(n_tiles, tm, tab_ref, fill_hbm, h_hbm, inv_ref, x_hbm, buf, rsem, wsem):
    i = pl.program_id(0)
    rows = tm * SUBLANES
    n_buf = buf.shape[0]

    @pl.when(i == 0)
    def _():
        pltpu.sync_copy(fill_hbm, inv_ref)

    def read(j):
        s = lax.rem(j, n_buf)
        return pltpu.make_async_copy(h_hbm.at[pl.ds(pl.multiple_of(j * rows, rows), rows)], buf.at[s],
                                     rsem.at[s])

    def wait_writes(s):
        for _ in range(TOP_K):
            pltpu.make_async_copy(buf.at[s], x_hbm.at[pl.ds(0, rows)], wsem.at[s]).wait()

    @pl.when(i == 0)
    def _():
        read(i).start()

    @pl.when(i >= n_buf - 1)
    def _():
        wait_writes(lax.rem(i + 1, n_buf))

    @pl.when(i + 1 < n_tiles)
    def _():
        read(i + 1).start()

    read(i).wait()
    s = lax.rem(i, n_buf)

    first_id = i * (tm * TOP_K)

    def row(r, carry):
        src = buf.at[s, pl.ds(pl.multiple_of(r * SUBLANES, SUBLANES), SUBLANES)]
        for k in range(TOP_K):
            slot = tab_ref[0, 0, r * TOP_K + k]
            inv_ref[slot] = first_id + r * TOP_K + k
            dst = pl.multiple_of(slot * SUBLANES, SUBLANES)
            pltpu.make_async_copy(src, x_hbm.at[pl.ds(dst, SUBLANES)], wsem.at[s]).start(priority=k % 2)
        return carry
    lax.fori_loop(0, tm, row, 0, unroll=4)

    @pl.when(i == n_tiles - 1)
    def _():
        for j in range(max(n_tiles - (n_buf - 1), 0), n_tiles):
            wait_writes(j % n_buf)


def _dispatch(slot_tab, h2p, n_slots, fill_value, tm=128):
    t = h2p.shape[0] // SUBLANES
    n_tiles = t // tm
    tab = slot_tab.reshape(n_tiles, 1, tm * TOP_K)
    return pl.pallas_call(
        functools.partial(_dispatch_kernel, n_tiles, tm),
        grid=(n_tiles,),
        in_specs=[pl.BlockSpec((1, 1, tm * TOP_K), lambda i: (i, 0, 0), memory_space=pltpu.SMEM),
                  pl.BlockSpec(memory_space=pl.ANY), pl.BlockSpec(memory_space=pl.ANY)],
        out_specs=[pl.BlockSpec((n_slots,), lambda i: (0,), memory_space=pltpu.SMEM),
                   pl.BlockSpec(memory_space=pl.ANY)],
        out_shape=[jax.ShapeDtypeStruct((n_slots,), I32),
                   jax.ShapeDtypeStruct((n_slots * SUBLANES, LANES), U32)],
        scratch_shapes=[pltpu.VMEM((3, tm * SUBLANES, LANES), U32), pltpu.SemaphoreType.DMA((3,)),
                        pltpu.SemaphoreType.DMA((3,))],
        compiler_params=_cparams("arbitrary"),
        name="dispatch",
    )(tab, jnp.full((n_slots,), fill_value, I32), h2p)


def _experts_kernel(n_tab, t_tok, bs_ref, bn_ref, tab_hbm, x_hbm, wg_hbm, wu_hbm, wd_hbm, y_hbm,
                    xbuf, ybuf, wgf, wuf, wdf, wgb, wub, wdb, ids, gsem, ssem, isem, wsem):
    e = pl.program_id(0)
    n_exp = pl.num_programs(0)
    trash = TOP_K * t_tok
    k_shift = TOP_K.bit_length() - 1
    blk_rows = MOE_BLOCK * SUBLANES

    def weight_copies(ex):
        s = ex & 1
        return [pltpu.make_async_copy(src.at[ex], dst.at[s], wsem.at[s, i])
                for i, (src, dst) in enumerate(((wg_hbm, wgf), (wu_hbm, wuf), (wd_hbm, wdf)))]

    def ids_copy(g):
        q = g & (IDS_RING - 1)
        row = jnp.minimum(g, n_tab - 1)
        return pltpu.make_async_copy(tab_hbm.at[pl.ds(row, 1)], ids.at[pl.ds(q, 1)], isem.at[q])

    def gather_starts(g):
        s = g & (BUF_RING - 1)

        def one():
            row = pl.multiple_of(jnp.minimum(g, n_tab - 1) * blk_rows, blk_rows)
            pltpu.make_async_copy(x_hbm.at[pl.ds(row, blk_rows)], xbuf.at[s], gsem.at[s]).start()
        return [one]

    def scatter_starts(g):
        s, q = g & (BUF_RING - 1), g & (IDS_RING - 1)

        def one(r):
            a = ids[q, r]
            dst = jnp.where(a >= trash, trash + s * MOE_BLOCK + r,
                            (a & (TOP_K - 1)) * t_tok + jnp.right_shift(a, k_shift))
            out = y_hbm.at[pl.ds(pl.multiple_of(dst * SUBLANES, SUBLANES), SUBLANES)]
            pltpu.make_async_copy(ybuf.at[s, pl.ds(r * SUBLANES, SUBLANES)], out, ssem.at[s]).start()
        return [functools.partial(one, r) for r in range(MOE_BLOCK)]

    def wait_gather(s):
        pltpu.make_async_copy(x_hbm.at[pl.ds(0, blk_rows)], xbuf.at[s], gsem.at[s]).wait()

    def wait_scatter(s):
        pltpu.make_async_copy(ybuf.at[s], y_hbm.at[pl.ds(0, blk_rows)], ssem.at[s]).wait()

    @pl.when(e == 0)
    def _():
        for c in weight_copies(e):
            c.start(priority=WEIGHT_DMA_PRIORITY)
        for g0 in range(GATHER_AHEAD + 1):
            ids_copy(g0).start()
        for g0 in range(GATHER_AHEAD):
            ids_copy(g0).wait()
            for start in gather_starts(g0):
                start()
        q_fake = IDS_RING - 1
        for r in range(MOE_BLOCK):
            ids[q_fake, r] = trash
        ybuf[BUF_RING - 1] = jnp.zeros((blk_rows, LANES), U32)

    @pl.when(e + 1 < n_exp)
    def _():
        for c in weight_copies(e + 1):
            c.start(priority=WEIGHT_DMA_PRIORITY)

    for c in weight_copies(e):
        c.wait()

    @pl.when(bn_ref[e] > 0)
    def _():
        s = e & 1
        wgb[...] = wgf[s].astype(BF16)
        wub[...] = wuf[s].astype(BF16)
        wdb[...] = wdf[s].astype(BF16)

    d_exp = wgb.shape[1]
    n_gate = d_exp // N_HALF
    n_pair = wdb.shape[1] // (2 * N_HALF)
    n_phase = 2 * n_gate + 1

    def block(j, carry):
        g = bs_ref[e] + j
        s = g & (BUF_RING - 1)

        @pl.when(g >= BUF_RING - 1)
        def _():
            wait_scatter(s)

        ids_copy(g + GATHER_AHEAD).wait()
        wait_gather(s)

        starts = gather_starts(g + GATHER_AHEAD) + scatter_starts(g - 1)
        bounds = [len(starts) * p // n_phase for p in range(n_phase + 1)]
        phase = iter(range(n_phase))

        def issue():
            p = next(phase)
            for f in starts[bounds[p]:bounds[p + 1]]:
                f()

        x = _load_packed(xbuf.at[s], MOE_BLOCK).astype(BF16)
        ids_copy(g + GATHER_AHEAD + 1).start()
        acts = []
        for c in range(n_gate):
            cols = slice(c * N_HALF, (c + 1) * N_HALF)
            issue()
            gate = _bdot(x, wgb[:, cols])
            issue()
            up = _bdot(x, wub[:, cols])
            acts.append((gate * jax.nn.sigmoid(gate) * up).astype(BF16))
        act = jnp.concatenate(acts, axis=1)
        half = n_pair * N_HALF
        issue()
        for c in range(n_pair):
            y_lo = _bdot(act, wdb[:, c * N_HALF:(c + 1) * N_HALF])
            y_hi = _bdot(act, wdb[:, half + c * N_HALF:half + (c + 1) * N_HALF])
            _store_packed_cols(y_lo, y_hi, ybuf.at[s], c * (N_HALF // LANES), MOE_BLOCK)
        return carry

    lax.fori_loop(0, bn_ref[e], block, 0)

    @pl.when(e == n_exp - 1)
    def _():
        n_used = bs_ref[e] + bn_ref[e]
        for a in range(GATHER_AHEAD):
            wait_gather((n_used + a) & (BUF_RING - 1))
        ids_copy(n_used + GATHER_AHEAD).wait()
        for start in scatter_starts(n_used - 1):
            start()
        for s in range(BUF_RING):
            wait_scatter(s)
        ybuf[...] = jnp.zeros(ybuf.shape, U32)
        for s in range(BUF_RING):
            tail = pltpu.make_async_copy(
                ybuf.at[s], y_hbm.at[pl.ds((trash + s * MOE_BLOCK) * SUBLANES, blk_rows)], ssem.at[s])
            tail.start()
            tail.wait()


def _experts(blk_start, blk_count, table, x_sorted, t, wg, wu, wd):
    n_tab = table.shape[0]
    ne, d, de = wg.shape
    blk_rows = MOE_BLOCK * SUBLANES
    grid_spec = pltpu.PrefetchScalarGridSpec(
        num_scalar_prefetch=2,
        grid=(ne,),
        in_specs=[pl.BlockSpec(memory_space=pl.ANY)] * 5,
        out_specs=pl.BlockSpec(memory_space=pl.ANY),
        scratch_shapes=[pltpu.VMEM((BUF_RING, blk_rows, LANES), U32), pltpu.VMEM((BUF_RING, blk_rows, LANES), U32),
                        pltpu.VMEM((2, d, de), F32), pltpu.VMEM((2, d, de), F32), pltpu.VMEM((2, de, d), F32),
                        pltpu.VMEM((d, de), BF16), pltpu.VMEM((d, de), BF16), pltpu.VMEM((de, d), BF16),
                        pltpu.SMEM((IDS_RING, MOE_BLOCK), I32),
                        pltpu.SemaphoreType.DMA((BUF_RING,)), pltpu.SemaphoreType.DMA((BUF_RING,)),
                        pltpu.SemaphoreType.DMA((IDS_RING,)), pltpu.SemaphoreType.DMA((2, 3))],
    )
    return pl.pallas_call(
        functools.partial(_experts_kernel, n_tab, t),
        grid_spec=grid_spec,
        out_shape=jax.ShapeDtypeStruct(((TOP_K * t + TRASH_ROWS) * SUBLANES, LANES), U32),
        compiler_params=_cparams("arbitrary"),
        name="experts",
    )(blk_start, blk_count, table, x_sorted, wg, wu, wd)


def _combine_kernel(n_ptiles, *refs):
    y_refs = refs[:TOP_K]
    (wt_ref, h2_ref, x1_ref, wgs_ref, wus_ref, wds_ref, gf_ref, gtp_ref, gts_ref,
     yp_ref, ysm_ref) = refs[TOP_K:]
    i = pl.program_id(0)

    tm = h2_ref.shape[0]
    h = h2_ref[...]
    g = _bdot(h, wgs_ref[...])
    u = _bdot(h, wus_ref[...])
    f = _bdot((g * jax.nn.sigmoid(g) * u).astype(BF16), wds_ref[...])

    wt = wt_ref[...]
    routed = wt[:, 0:1] * _load_packed(y_refs[0], tm)
    for k in range(1, TOP_K):
        routed = routed + wt[:, k:k + 1] * _load_packed(y_refs[k], tm)
    f = routed + f

    @pl.when(i < n_ptiles)
    def _():
        yp_ref[...] = _rms(x1_ref[...] + gtp_ref[...] * f) * gf_ref[...]

    @pl.when(i == n_ptiles)
    def _():
        ysm_ref[...] = _rms(x1_ref[...] + gts_ref[...] * f) * gf_ref[...]


def _combine(y_planes, wts, h2, x1, wgs, wus, wds, gf, gtp, gts, n_prompt, seq, tm=128):
    t, d = h2.shape
    ts = t - n_prompt
    n_tiles = t // tm
    npt = n_prompt // tm
    per = seq // tm
    row = lambda i: (i, 0)
    full = lambda i: (0, 0)
    return pl.pallas_call(
        functools.partial(_combine_kernel, npt),
        grid=(n_tiles,),
        in_specs=[pl.BlockSpec((tm * SUBLANES, LANES), lambda i, k=k: (k * n_tiles + i, 0))
                  for k in range(TOP_K)] + [
                  pl.BlockSpec((tm, LANES), row), pl.BlockSpec((tm, d), row), pl.BlockSpec((tm, d), row),
                  _const_spec(wgs.shape), _const_spec(wus.shape), _const_spec(wds.shape),
                  pl.BlockSpec((1, d), full),
                  pl.BlockSpec((None, 1, d), lambda i: (jnp.minimum(i, npt - 1) // per, 0, 0)),
                  pl.BlockSpec((ts, d), full)],
        out_specs=[pl.BlockSpec((tm, d), lambda i: (jnp.minimum(i, npt - 1), 0)),
                   pl.BlockSpec((ts, d), full)],
        out_shape=[jax.ShapeDtypeStruct((n_prompt, d), F32), jax.ShapeDtypeStruct((ts, d), F32)],
        compiler_params=_cparams("arbitrary"),
        name="combine",
    )(*([y_planes] * TOP_K), wts, h2, x1, wgs, wus, wds, gf.reshape(1, d), gtp, gts)


def kernel(x_prompt, x_sample, state_conv, c_prompt, c_sample, w_ada, b_ada, g_norm1, w_in, a_norm_g, a_norm_b, w_s, b_s, conv_w, conv_b, b_norm_g, b_norm_b, w_a_out, w_b_out, w_o, g_norm2, w_router, router_bias, w_gate_e, w_up_e, w_down_e, w_gate_s, w_up_s, w_down_s, g_final):
    depth = w_ada.shape[0]
    assert depth == 1, "single-layer trunk"
    nb, seq, d = x_prompt.shape
    ts, dec_seq, _ = x_sample.shape
    assert dec_seq == 1
    tp = nb * seq
    t = tp + ts
    l = 0

    xp = x_prompt.reshape(tp, d)
    xs = x_sample.reshape(ts, d)

    pad = 16
    n_mod = w_ada.shape[2] // d
    c_all = jnp.concatenate([c_sample, c_prompt, jnp.zeros((pad - nb, d), F32)], axis=0)
    mod_s, mod_p = _ada(c_all, ts, w_ada[l], b_ada[l], n_mod)
    mp = [mod_p[k, :nb].reshape(nb, 1, d) for k in range(n_mod)]
    ms = [mod_s[k] for k in range(n_mod)]

    w_in16 = w_in[l].astype(BF16)
    h, u = _hmod(xp, xs, g_norm1[l], mp[1], mp[0], ms[1], ms[0], w_in16, seq)
    blk = lambda s: _col_block(w_in16, d, s)
    vec = lambda a: a.reshape(1, d)
    v32, v16 = _rows_call(_seg_v_kernel, h, ts, 512, [blk(1)], [vec(a_norm_g[l]), vec(a_norm_b[l])],
                          [F32, BF16], "seg_v")
    (glu,) = _rows_call(_seg_glu_kernel, h, ts, 512, [blk(2), blk(3)], [], [F32], "seg_glu")
    ga, gb = _rows_call(_seg_gate_kernel, h, ts, 512, [blk(4), blk(5)], [], [BF16, BF16], "seg_gate")

    gdim = d // A_GROUPS
    cb, ng, nbb = vec(conv_b[l]), vec(b_norm_g[l]), vec(b_norm_b[l])
    yap, ybp = _mixp(u, v16, glu, w_s[l], b_s[l].T, conv_w[l], cb, ng, nbb, tp, seq)
    state_t = jnp.transpose(state_conv[l], (1, 0, 2))
    yas, ybs, conv_t = _mixs(u, v32, glu, state_t, vec(jnp.repeat(w_s[l][:, 0, 0], gdim)),
                             vec(jnp.repeat(b_s[l][:, 0], gdim)), conv_w[l], cb, ng, nbb, tp)
    conv_s = jnp.transpose(conv_t, (1, 0, 2))[None]

    wr = w_router[l]
    wrh = wr.astype(BF16)
    wrl = (wr - wrh.astype(F32)).astype(BF16)
    x1, h2, h2p, logits = _proj(yap, ybp, yas, ybs, ga, gb, xp, xs,
                                w_a_out[l].astype(BF16), w_b_out[l].astype(BF16), w_o[l].astype(BF16),
                                wrh, wrl, g_norm2[l], mp[2], mp[4], mp[3], ms[2], ms[4], ms[3], seq)

    n_blocks = -(-(t * TOP_K) // MOE_BLOCK) + N_EXPERTS
    idx, pos, wts, cnt = _route(logits, router_bias[l])
    slot, blk_start, blk_count = _slots(cnt, idx, pos)
    slot_tab = slot[:, :TOP_K].reshape(t * TOP_K)
    inv, x_sorted = _dispatch(slot_tab, h2p, n_blocks * MOE_BLOCK, t * TOP_K)
    table = inv.reshape(n_blocks, MOE_BLOCK)
    y_planes = _experts(blk_start[0], blk_count[0], table, x_sorted, t, w_gate_e[l], w_up_e[l], w_down_e[l])
    yp, ysm = _combine(y_planes, wts, h2, x1,
                       w_gate_s[l].astype(BF16), w_up_s[l].astype(BF16), w_down_s[l].astype(BF16),
                       g_final, mp[5], ms[5], tp, seq)

    n_past = state_conv.shape[2]
    last_rows = lambda a, n: jnp.stack([a[(b + 1) * seq - n:(b + 1) * seq] for b in range(nb)])[None]
    conv_p = last_rows(glu, n_past)
    chunkv_p = last_rows(v32, CHUNK)
    chunkv_s = v32[tp:].reshape(1, ts, 1, d)
    return (yp.reshape(nb, seq, d), ysm.reshape(ts, 1, d), conv_p, conv_s, chunkv_p, chunkv_s)
```

```python
import functools

import jax
import jax.numpy as jnp
from jax import lax
from jax.experimental import pallas as pl
from jax.experimental.pallas import tpu as pltpu

F32 = jnp.float32
BF16 = jnp.bfloat16
I32 = jnp.int32

EPS = 1e-6
CHUNK = 128
A_GROUPS = 8
CONV_W = 31
N_EXPERTS = 256
TOP_K = 8
N_EXPERT_GROUPS = 8
GROUP_SIZE = N_EXPERTS // N_EXPERT_GROUPS
TOPK_GROUPS = 4
ROUTED_SCALE = 2.5
MOE_BLOCK = 128
LANES = 128
SUBLANES = 8
VMEM_LIMIT = 58 * 1024 * 1024

NEG_INF = float("-inf")


def _cparams(*sem):
    return pltpu.CompilerParams(dimension_semantics=sem, vmem_limit_bytes=VMEM_LIMIT)


def _const_spec(shape):
    nd = len(shape)
    return pl.BlockSpec(shape, lambda *_: (0,) * nd, pipeline_mode=pl.Buffered(1))


def _rms(x):
    return x * lax.rsqrt(jnp.mean(x * x, axis=-1, keepdims=True) + EPS)


def _layer_norm(x, g, b):
    mu = jnp.mean(x, axis=-1, keepdims=True)
    xc = x - mu
    var = jnp.mean(xc * xc, axis=-1, keepdims=True)
    return xc * lax.rsqrt(var + EPS) * g + b


def _bdot(a, b):
    return jnp.dot(a, b, preferred_element_type=F32)


U32 = jnp.uint32
HI_MASK = 0xFFFF0000


def _store_packed_cols(x_lo, x_hi, ref, c0, n_rows):
    for i in range(x_lo.shape[1] // LANES):
        lo = x_lo[:, i * LANES:(i + 1) * LANES].astype(BF16).astype(F32)
        hi = x_hi[:, i * LANES:(i + 1) * LANES].astype(BF16).astype(F32)
        word = (lax.bitcast_convert_type(hi, U32) & U32(HI_MASK)) | (lax.bitcast_convert_type(lo, U32) >> 16)
        ref[pl.ds(c0 + i, n_rows, stride=SUBLANES), :] = word


def _store_packed(x, ref, n_rows):
    half = x.shape[1] // 2
    assert half == SUBLANES * LANES
    _store_packed_cols(x[:, :half], x[:, half:], ref, 0, n_rows)


def _load_packed(ref, n_rows):
    lo, hi = [], []
    for c in range(SUBLANES):
        word = ref[pl.ds(c, n_rows, stride=SUBLANES), :]
        lo.append(lax.bitcast_convert_type(word << 16, F32))
        hi.append(lax.bitcast_convert_type(word & U32(HI_MASK), F32))
    return jnp.concatenate(lo + hi, axis=1)


def _lane_sum(x):
    return jnp.sum(x, axis=-1, keepdims=True)


def _lane_max(x):
    return jnp.max(x, axis=-1, keepdims=True)


def _lane_min(x):
    return jnp.min(x, axis=-1, keepdims=True)


def _ada_kernel(n_a, c_ref, w_ref, b_ref, oa_ref, ob_ref):
    c = c_ref[...]
    s = (c * jax.nn.sigmoid(c)).astype(BF16)
    m = _bdot(s, w_ref[...].astype(BF16)) + b_ref[...]
    oa_ref[...] = m[0:n_a, :]
    ob_ref[...] = m[n_a:, :]


def _ada(c_all, n_a, w_ada, b_ada, n_mod, tn=1024):
    m, d = c_all.shape
    n = w_ada.shape[1]
    per = d // tn
    out = lambda rows: pl.BlockSpec((None, rows, tn), lambda j: (j // per, 0, j % per))
    return pl.pallas_call(
        functools.partial(_ada_kernel, n_a),
        grid=(n // tn,),
        in_specs=[pl.BlockSpec((m, d), lambda j: (0, 0)),
                  pl.BlockSpec((d, tn), lambda j: (0, j)),
                  pl.BlockSpec((1, tn), lambda j: (0, j))],
        out_specs=[out(n_a), out(m - n_a)],
        out_shape=[jax.ShapeDtypeStruct((n_mod, n_a, d), F32), jax.ShapeDtypeStruct((n_mod, m - n_a, d), F32)],
        compiler_params=_cparams("arbitrary"),
        name="ada",
    )(c_all, w_ada, b_ada.reshape(1, n))


def _hmod_kernel(n_ptiles, n_s, xp_ref, xs_ref, g_ref, scp_ref, shp_ref, scs_ref, shs_ref, wu_ref,
                 o_ref, u_ref):
    i = pl.program_id(0)

    def body(rows, x, sc, sh):
        h = ((_rms(x) * g_ref[...]) * (1.0 + sc) + sh).astype(o_ref.dtype)
        o_ref[0:rows, :] = h
        u_ref[0:rows, :] = jax.nn.gelu(_bdot(h, wu_ref[...])).astype(u_ref.dtype)

    @pl.when(i < n_ptiles)
    def _():
        body(o_ref.shape[0], xp_ref[...], scp_ref[...], shp_ref[...])

    @pl.when(i == n_ptiles)
    def _():
        body(n_s, xs_ref[...], scs_ref[...], shs_ref[...])


def _hmod(xp, xs, g, scp, shp, scs, shs, w_in16, seq, tm=1024):
    tp, d = xp.shape
    ts = xs.shape[0]
    npt = tp // tm
    per = seq // tm
    last = npt - 1
    pmod = pl.BlockSpec((None, 1, d), lambda i: (jnp.minimum(i, last) // per, 0, 0))
    full = lambda i: (0, 0)
    return pl.pallas_call(
        functools.partial(_hmod_kernel, npt, ts),
        grid=(npt + 1,),
        in_specs=[pl.BlockSpec((tm, d), lambda i: (jnp.minimum(i, last), 0)),
                  pl.BlockSpec((ts, d), full),
                  pl.BlockSpec((1, d), full),
                  pmod, pmod,
                  pl.BlockSpec((ts, d), full),
                  pl.BlockSpec((ts, d), full),
                  pl.BlockSpec((w_in16.shape[0], d), full, pipeline_mode=pl.Buffered(1))],
        out_specs=[pl.BlockSpec((tm, d), lambda i: (i, 0)), pl.BlockSpec((tm, d), lambda i: (i, 0))],
        out_shape=[jax.ShapeDtypeStruct((tp + ts, d), BF16), jax.ShapeDtypeStruct((tp + ts, d), BF16)],
        compiler_params=_cparams("arbitrary"),
        name="hmod",
    )(xp, xs, g.reshape(1, d), scp, shp, scs, shs, w_in16)


def _rows_call(kernel, h, n_tail, tm, weights, vecs, out_dtypes, name):
    t, d = h.shape
    n_full = (t - n_tail) // tm

    def wrapped(*refs):
        i = pl.program_id(0)

        @pl.when(i < n_full)
        def _():
            kernel(tm, *refs)

        @pl.when(i == n_full)
        def _():
            kernel(n_tail, *refs)

    in_specs = [pl.BlockSpec((tm, d), lambda i: (i, 0))]
    in_specs += [pl.BlockSpec(bs, im, pipeline_mode=pl.Buffered(1)) for (_, bs, im) in weights]
    in_specs += [pl.BlockSpec(v.shape, lambda i: (0, 0)) for v in vecs]
    return pl.pallas_call(
        wrapped,
        grid=(n_full + 1,),
        in_specs=in_specs,
        out_specs=[pl.BlockSpec((tm, d), lambda i: (i, 0)) for _ in out_dtypes],
        out_shape=[jax.ShapeDtypeStruct((t, d), dt) for dt in out_dtypes],
        compiler_params=_cparams("arbitrary"),
        name=name,
    )(h, *[w for (w, _, _) in weights], *vecs)


def _seg_v_kernel(rows, h_ref, w_ref, g_ref, b_ref, o32_ref, o16_ref):
    z = _bdot(h_ref[0:rows, :], w_ref[...])
    v = _layer_norm(jax.nn.gelu(z), g_ref[...], b_ref[...])
    o32_ref[0:rows, :] = v
    o16_ref[0:rows, :] = v.astype(o16_ref.dtype)


def _seg_glu_kernel(rows, h_ref, wa_ref, wb_ref, o_ref):
    h = h_ref[0:rows, :]
    o_ref[0:rows, :] = _bdot(h, wa_ref[...]) * jax.nn.sigmoid(_bdot(h, wb_ref[...]))


def _seg_gate_kernel(rows, h_ref, wa_ref, wb_ref, oa_ref, ob_ref):
    h = h_ref[0:rows, :]
    oa_ref[0:rows, :] = jax.nn.sigmoid(_bdot(h, wa_ref[...])).astype(oa_ref.dtype)
    ob_ref[0:rows, :] = jax.nn.sigmoid(_bdot(h, wb_ref[...])).astype(ob_ref.dtype)


def _col_block(w, d, seg):
    return (w, (w.shape[0], d), lambda i, seg=seg: (0, seg))


CONV_ROWS = 128
CONV_LANES = 128


def _mixp_kernel(tm, halo, per, u_ref, v_ref, glu_ref, prev_ref, ws_ref, bst_ref, cw_ref, cb_ref,
                 ng_ref, nb_ref, ya_ref, yb_ref, xp_ref, cv_ref):
    i = pl.program_id(0)
    d = u_ref.shape[1]
    n_chunks = tm // CHUNK
    gdim = d // A_GROUPS

    r_io = lax.broadcasted_iota(I32, (CHUNK, CHUNK), 0)
    c_io = lax.broadcasted_iota(I32, (CHUNK, CHUNK), 1)
    tril = c_io <= r_io
    for g in range(A_GROUPS):
        cs = slice(g * gdim, (g + 1) * gdim)
        wg = jnp.where(tril, ws_ref[g], 0.0).astype(BF16)
        vg = jnp.concatenate([v_ref[c * CHUNK:(c + 1) * CHUNK, cs] for c in range(n_chunks)], axis=1)
        mix = _bdot(wg, vg)
        bcol = bst_ref[:, g:g + 1]
        for c in range(n_chunks):
            rs = slice(c * CHUNK, (c + 1) * CHUNK)
            ya_ref[rs, cs] = (u_ref[rs, cs].astype(F32)
                              * (mix[:, c * gdim:(c + 1) * gdim] + bcol)).astype(ya_ref.dtype)

    xp_ref[0:halo, :] = jnp.where(i % per == 0, 0.0, prev_ref[...])
    xp_ref[halo:halo + tm, :] = glu_ref[...]
    first = halo - (CONV_W - 1)
    n_cb = d // CONV_LANES

    def chunk(idx, carry):
        r0 = pl.multiple_of((idx // n_cb) * CONV_ROWS, CONV_ROWS)
        c0 = pl.multiple_of((idx % n_cb) * CONV_LANES, CONV_LANES)
        acc = jnp.zeros((CONV_ROWS, CONV_LANES), F32)
        n_win = CONV_ROWS + halo
        window = xp_ref[pl.ds(r0, n_win), pl.ds(c0, CONV_LANES)]
        for j in range(SUBLANES):
            taps = [m for m in range(first, first + CONV_W) if m % SUBLANES == j]
            s_j = window if j == 0 else pltpu.roll(window, n_win - j, 0)
            for m in taps:
                w_row = cw_ref[pl.ds(m - first, 1), pl.ds(c0, CONV_LANES)]
                acc = acc + w_row * s_j[m - j:m - j + CONV_ROWS, :]
        cv_ref[pl.ds(r0, CONV_ROWS), pl.ds(c0, CONV_LANES)] = acc
        return carry

    lax.fori_loop(0, (tm // CONV_ROWS) * n_cb, chunk, 0)

    y = _layer_norm(cv_ref[...] + cb_ref[...], ng_ref[...], nb_ref[...])
    yb_ref[...] = (y * jax.nn.sigmoid(y)).astype(yb_ref.dtype)


def _mixp(u, v16, glu, w_s, b_s_t, conv_w, conv_b, ng, nb, n_prompt, seq, tm=512, halo=32):
    d = u.shape[1]
    npt = n_prompt // tm
    per = seq // tm
    hb = tm // halo
    row = lambda i: (i, 0)
    return pl.pallas_call(
        functools.partial(_mixp_kernel, tm, halo, per),
        grid=(npt,),
        in_specs=[pl.BlockSpec((tm, d), row),
                  pl.BlockSpec((tm, d), row),
                  pl.BlockSpec((tm, d), row),
                  pl.BlockSpec((halo, d), lambda i: (jnp.maximum(i * hb - 1, 0), 0)),
                  _const_spec(w_s.shape), _const_spec(b_s_t.shape), _const_spec(conv_w.shape),
                  _const_spec(conv_b.shape), _const_spec(ng.shape), _const_spec(nb.shape)],
        out_specs=[pl.BlockSpec((tm, d), row), pl.BlockSpec((tm, d), row)],
        out_shape=[jax.ShapeDtypeStruct((n_prompt, d), BF16), jax.ShapeDtypeStruct((n_prompt, d), BF16)],
        scratch_shapes=[pltpu.VMEM((halo + tm, d), F32), pltpu.VMEM((tm, d), F32)],
        compiler_params=_cparams("arbitrary"),
        name="mixp",
    )(u, v16, glu, glu, w_s, b_s_t, conv_w, conv_b, ng, nb)


def _mixs_kernel(u_ref, v_ref, glu_ref, st_ref, ws0_ref, bs0_ref, cw_ref, cb_ref, ng_ref, nb_ref,
                 ya_ref, yb_ref, st_out_ref):
    n_past = st_ref.shape[0]
    glu = glu_ref[...]
    ya_ref[...] = (u_ref[...].astype(F32) * (ws0_ref[...] * v_ref[...] + bs0_ref[...])).astype(ya_ref.dtype)
    acc = cw_ref[n_past:n_past + 1, :] * glu + cb_ref[...]
    for k in range(n_past):
        acc = acc + cw_ref[k:k + 1, :] * st_ref[k]
    y = _layer_norm(acc, ng_ref[...], nb_ref[...])
    yb_ref[...] = (y * jax.nn.sigmoid(y)).astype(yb_ref.dtype)
    for k in range(n_past - 1):
        st_out_ref[k] = st_ref[k + 1]
    st_out_ref[n_past - 1] = glu


def _mixs(u, v32, glu, state_t, ws0, bs0, conv_w, conv_b, ng, nb, n_prompt, tb=32):
    n_past, ts, d = state_t.shape
    off = n_prompt // tb
    tail = lambda j: (off + j, 0)
    vec = lambda j: (0, 0)
    st_spec = pl.BlockSpec((n_past, tb, d), lambda j: (0, j, 0))
    return pl.pallas_call(
        _mixs_kernel,
        grid=(ts // tb,),
        in_specs=[pl.BlockSpec((tb, d), tail), pl.BlockSpec((tb, d), tail), pl.BlockSpec((tb, d), tail),
                  st_spec,
                  pl.BlockSpec((1, d), vec), pl.BlockSpec((1, d), vec),
                  pl.BlockSpec(conv_w.shape, vec), pl.BlockSpec((1, d), vec),
                  pl.BlockSpec((1, d), vec), pl.BlockSpec((1, d), vec)],
        out_specs=[pl.BlockSpec((tb, d), lambda j: (j, 0)), pl.BlockSpec((tb, d), lambda j: (j, 0)), st_spec],
        out_shape=[jax.ShapeDtypeStruct((ts, d), BF16), jax.ShapeDtypeStruct((ts, d), BF16),
                   jax.ShapeDtypeStruct(state_t.shape, state_t.dtype)],
        compiler_params=_cparams("arbitrary"),
        name="mixs",
    )(u, v32, glu, state_t, ws0, bs0, conv_w, conv_b, ng, nb)


PROJ_ROWS = 128


def _proj_kernel(n_ptiles, tm, n_s,
                 yap_ref, ybp_ref, yas_ref, ybs_ref, ga_ref, gb_ref, xp_ref, xs_ref,
                 wa_ref, wb_ref, wo_ref, wrh_ref, wrl_ref, g2_ref,
                 gtp_ref, scp_ref, shp_ref, gts_ref, scs_ref, shs_ref,
                 x1_ref, h2_ref, h2p_ref, lg_ref):
    i = pl.program_id(0)

    def body(r0, rows, ya_ref, yb_ref, x_ref, gate, sc, sh):
        rs = slice(r0, r0 + rows)
        pa = _bdot(ya_ref[rs, :], wa_ref[...])
        pb = _bdot(yb_ref[rs, :], wb_ref[...])
        merged = ga_ref[rs, :].astype(F32) * pa + gb_ref[rs, :].astype(F32) * pb
        out = _bdot(merged.astype(BF16), wo_ref[...])
        x1 = x_ref[rs, :] + gate * out
        h2 = (_rms(x1) * g2_ref[...]) * (1.0 + sc) + sh
        hi = h2.astype(BF16)
        lo = (h2 - hi.astype(F32)).astype(BF16)
        logits = _bdot(hi, wrh_ref[...]) + (_bdot(hi, wrl_ref[...]) + _bdot(lo, wrh_ref[...]))
        x1_ref[rs, :] = x1
        h2_ref[rs, :] = hi
        _store_packed(h2, h2p_ref.at[pl.ds(r0 * SUBLANES, rows * SUBLANES)], rows)
        lg_ref[rs, :] = logits

    @pl.when(i < n_ptiles)
    def _():
        for r0 in range(0, tm, PROJ_ROWS):
            body(r0, PROJ_ROWS, yap_ref, ybp_ref, xp_ref, gtp_ref[...], scp_ref[...], shp_ref[...])

    @pl.when(i == n_ptiles)
    def _():
        body(0, n_s, yas_ref, ybs_ref, xs_ref, gts_ref[...], scs_ref[...], shs_ref[...])


def _proj(yap, ybp, yas, ybs, ga, gb, xp, xs, wa, wb, wo, wrh, wrl, g2,
          gtp, scp, shp, gts, scs, shs, seq, tm=256):
    tp, d = xp.shape
    ts = xs.shape[0]
    t = tp + ts
    ne = wrh.shape[1]
    npt = tp // tm
    per = seq // tm
    last = npt - 1
    prow = lambda i: (jnp.minimum(i, last), 0)
    row = lambda i: (i, 0)
    full = lambda i: (0, 0)
    pmod = pl.BlockSpec((None, 1, d), lambda i: (jnp.minimum(i, last) // per, 0, 0))
    smod = pl.BlockSpec((ts, d), full)
    return pl.pallas_call(
        functools.partial(_proj_kernel, npt, tm, ts),
        grid=(npt + 1,),
        in_specs=[pl.BlockSpec((tm, d), prow), pl.BlockSpec((tm, d), prow),
                  pl.BlockSpec((ts, d), full), pl.BlockSpec((ts, d), full),
                  pl.BlockSpec((tm, d), row), pl.BlockSpec((tm, d), row),
                  pl.BlockSpec((tm, d), prow), pl.BlockSpec((ts, d), full),
                  _const_spec(wa.shape), _const_spec(wb.shape), _const_spec(wo.shape),
                  _const_spec(wrh.shape), _const_spec(wrl.shape), _const_spec((1, d)),
                  pmod, pmod, pmod, smod, smod, smod],
        out_specs=[pl.BlockSpec((tm, d), row), pl.BlockSpec((tm, d), row),
                   pl.BlockSpec((tm * SUBLANES, LANES), row), pl.BlockSpec((tm, ne), row)],
        out_shape=[jax.ShapeDtypeStruct((t, d), F32), jax.ShapeDtypeStruct((t, d), BF16),
                   jax.ShapeDtypeStruct((t * SUBLANES, LANES), U32), jax.ShapeDtypeStruct((t, ne), F32)],
        compiler_params=_cparams("arbitrary"),
        name="proj",
    )(yap, ybp, yas, ybs, ga, gb, xp, xs, wa, wb, wo, wrh, wrl, g2.reshape(1, d),
      gtp, scp, shp, gts, scs, shs)


def _pack_cols(cols, lane):
    out = jnp.zeros(lane.shape, cols[0].dtype)
    for k, c in enumerate(cols):
        out = jnp.where(lane == k, c, out)
    return out


def _route_kernel(lg_ref, bias_ref, tri_ref, idx_ref, pos_ref, wt_ref, cnt_ref, carry_ref):
    i = pl.program_id(0)

    @pl.when(i == 0)
    def _():
        carry_ref[...] = jnp.zeros_like(carry_ref)

    scores = jax.nn.sigmoid(lg_ref[...])
    sel = scores + bias_ref[...]
    lane = lax.broadcasted_iota(I32, sel.shape, 1)
    lane_f = lane.astype(F32)
    grp = jnp.right_shift(lane, GROUP_SIZE.bit_length() - 1)

    gs_cols = []
    gs_full = jnp.zeros(sel.shape, F32)
    for g in range(N_EXPERT_GROUPS):
        in_g = grp == g
        xg = jnp.where(in_g, sel, NEG_INF)
        m1 = _lane_max(xg)
        n1 = _lane_sum(jnp.where(xg == m1, 1.0, 0.0))
        m2 = jnp.where(n1 >= 2.0, m1, _lane_max(jnp.where(xg < m1, xg, NEG_INF)))
        gs = m1 + m2
        gs_cols.append(gs)
        gs_full = jnp.where(in_g, gs, gs_full)

    beaten = jnp.zeros(sel.shape, F32)
    for g in range(N_EXPERT_GROUPS):
        better = (gs_cols[g] > gs_full) | ((gs_cols[g] == gs_full) & (g < grp))
        beaten = beaten + jnp.where(better, 1.0, 0.0)
    cur = jnp.where(beaten < float(TOPK_GROUPS), sel, NEG_INF)

    idx_cols, s_cols, hots = [], [], []
    picked = jnp.zeros(sel.shape, F32)
    for _ in range(TOP_K):
        m = _lane_max(cur)
        idx = _lane_min(jnp.where(cur == m, lane_f, float(N_EXPERTS)))
        hot = lane_f == idx
        idx_cols.append(idx)
        s_cols.append(_lane_sum(jnp.where(hot, scores, 0.0)))
        hots.append(hot)
        picked = jnp.where(hot, 1.0, picked)
        cur = jnp.where(hot, NEG_INF, cur)

    den = s_cols[0]
    for s in s_cols[1:]:
        den = den + s
    w_cols = [s / den * ROUTED_SCALE for s in s_cols]

    pos_full = _bdot(tri_ref[...], picked.astype(BF16)) + carry_ref[...]
    pos_cols = [_lane_sum(jnp.where(h, pos_full, 0.0)) for h in hots]
    carry_ref[...] = carry_ref[...] + jnp.sum(picked, axis=0, keepdims=True)
    cnt_ref[...] = carry_ref[...]

    lane_o = lax.broadcasted_iota(I32, idx_ref.shape, 1)
    idx_ref[...] = _pack_cols(idx_cols, lane_o).astype(I32)
    pos_ref[...] = _pack_cols(pos_cols, lane_o).astype(I32)
    wt_ref[...] = _pack_cols(w_cols, lane_o)


def _route(logits, bias, tm=640):
    t, ne = logits.shape
    tri = jnp.tril(jnp.ones((tm, tm), BF16), -1)
    row = lambda i: (i, 0)
    return pl.pallas_call(
        _route_kernel,
        grid=(t // tm,),
        in_specs=[pl.BlockSpec((tm, ne), row), pl.BlockSpec((1, ne), lambda i: (0, 0)),
                  pl.BlockSpec((tm, tm), lambda i: (0, 0))],
        out_specs=[pl.BlockSpec((tm, LANES), row), pl.BlockSpec((tm, LANES), row),
                   pl.BlockSpec((tm, LANES), row), pl.BlockSpec((1, ne), lambda i: (0, 0))],
        out_shape=[jax.ShapeDtypeStruct((t, LANES), I32), jax.ShapeDtypeStruct((t, LANES), I32),
                   jax.ShapeDtypeStruct((t, LANES), F32), jax.ShapeDtypeStruct((1, ne), F32)],
        scratch_shapes=[pltpu.VMEM((1, ne), F32)],
        compiler_params=_cparams("arbitrary"),
        name="route",
    )(logits, bias.reshape(1, ne), tri)


def _slots_kernel(cnt_ref, idx_ref, pos_ref, triu_ref, slot_ref, bs_ref, bn_ref, ps_ref):
    i = pl.program_id(0)
    ne = cnt_ref.shape[1]

    @pl.when(i == 0)
    def _():
        cnt = cnt_ref[...]
        nblk = jnp.floor((cnt + float(MOE_BLOCK - 1)) * (1.0 / MOE_BLOCK))
        pend = _bdot(jnp.broadcast_to(nblk, (SUBLANES, ne)).astype(BF16), triu_ref[...])[0:1, :]
        pstart = pend - nblk
        ps_ref[...] = pstart
        bs_ref[...] = pstart.astype(I32)
        bn_ref[...] = nblk.astype(I32)

    lane_o = lax.broadcasted_iota(I32, idx_ref.shape, 1)
    lane_e = lax.broadcasted_iota(I32, (idx_ref.shape[0], ne), 1)
    idx = idx_ref[...].astype(F32)
    pos = pos_ref[...].astype(F32)
    pstart = ps_ref[...]
    lane_ef = lane_e.astype(F32)
    cols = []
    for k in range(TOP_K):
        e_k = _lane_sum(jnp.where(lane_o == k, idx, 0.0))
        p_k = _lane_sum(jnp.where(lane_o == k, pos, 0.0))
        ps_k = _lane_sum(jnp.where(lane_ef == e_k, pstart, 0.0))
        cols.append(ps_k * float(MOE_BLOCK) + p_k)
    slot_ref[...] = _pack_cols(cols, lane_o).astype(I32)


def _slots(cnt, idx, pos, tm=640):
    t = idx.shape[0]
    ne = cnt.shape[1]
    triu = jnp.triu(jnp.ones((ne, ne), BF16))
    row = lambda i: (i, 0)
    full = lambda i: (0, 0)
    return pl.pallas_call(
        _slots_kernel,
        grid=(t // tm,),
        in_specs=[pl.BlockSpec((1, ne), full), pl.BlockSpec((tm, LANES), row),
                  pl.BlockSpec((tm, LANES), row), pl.BlockSpec((ne, ne), full)],
        out_specs=[pl.BlockSpec((tm, LANES), row), pl.BlockSpec((1, ne), full),
                   pl.BlockSpec((1, ne), full)],
        out_shape=[jax.ShapeDtypeStruct((t, LANES), I32), jax.ShapeDtypeStruct((1, ne), I32),
                   jax.ShapeDtypeStruct((1, ne), I32)],
        scratch_shapes=[pltpu.VMEM((1, ne), F32)],
        compiler_params=_cparams("arbitrary"),
        name="slots",
    )(cnt, idx, pos, triu)


BUF_RING = 4
IDS_RING = 8
GATHER_AHEAD = 2
TRASH_ROWS = BUF_RING * MOE_BLOCK
WEIGHT_DMA_PRIORITY = 1
N_HALF = 256


def _dispatch_kernel(n_tiles, tm, tab_ref, fill_hbm, h_hbm, inv_ref, x_hbm, buf, rsem, wsem):
    i = pl.program_id(0)
    rows = tm * SUBLANES
    n_buf = buf.shape[0]

    @pl.when(i == 0)
    def _():
        pltpu.sync_copy(fill_hbm, inv_ref)

    def read(j):
        s = lax.rem(j, n_buf)
        return pltpu.make_async_copy(h_hbm.at[pl.ds(pl.multiple_of(j * rows, rows), rows)], buf.at[s],
                                     rsem.at[s])

    def wait_writes(s):
        for _ in range(TOP_K):
            pltpu.make_async_copy(buf.at[s], x_hbm.at[pl.ds(0, rows)], wsem.at[s]).wait()

    @pl.when(i == 0)
    def _():
        read(i).start()

    @pl.when(i >= n_buf - 1)
    def _():
        wait_writes(lax.rem(i + 1, n_buf))

    @pl.when(i + 1 < n_tiles)
    def _():
        read(i + 1).start()

    read(i).wait()
    s = lax.rem(i, n_buf)

    first_id = i * (tm * TOP_K)

    def row(r, carry):
        src = buf.at[s, pl.ds(pl.multiple_of(r * SUBLANES, SUBLANES), SUBLANES)]
        for k in range(TOP_K):
            slot = tab_ref[0, 0, r * TOP_K + k]
            inv_ref[slot] = first_id + r * TOP_K + k
            dst = pl.multiple_of(slot * SUBLANES, SUBLANES)
            pltpu.make_async_copy(src, x_hbm.at[pl.ds(dst, SUBLANES)], wsem.at[s]).start(priority=k % 2)
        return carry
    lax.fori_loop(0, tm, row, 0, unroll=4)

    @pl.when(i == n_tiles - 1)
    def _():
        for j in range(max(n_tiles - (n_buf - 1), 0), n_tiles):
            wait_writes(j % n_buf)


def _dispatch(slot_tab, h2p, n_slots, fill_value, tm=128):
    t = h2p.shape[0] // SUBLANES
    n_tiles = t // tm
    tab = slot_tab.reshape(n_tiles, 1, tm * TOP_K)
    return pl.pallas_call(
        functools.partial(_dispatch_kernel, n_tiles, tm),
        grid=(n_tiles,),
        in_specs=[pl.BlockSpec((1, 1, tm * TOP_K), lambda i: (i, 0, 0), memory_space=pltpu.SMEM),
                  pl.BlockSpec(memory_space=pl.ANY), pl.BlockSpec(memory_space=pl.ANY)],
        out_specs=[pl.BlockSpec((n_slots,), lambda i: (0,), memory_space=pltpu.SMEM),
                   pl.BlockSpec(memory_space=pl.ANY)],
        out_shape=[jax.ShapeDtypeStruct((n_slots,), I32),
                   jax.ShapeDtypeStruct((n_slots * SUBLANES, LANES), U32)],
        scratch_shapes=[pltpu.VMEM((3, tm * SUBLANES, LANES), U32), pltpu.SemaphoreType.DMA((3,)),
                        pltpu.SemaphoreType.DMA((3,))],
        compiler_params=_cparams("arbitrary"),
        name="dispatch",
    )(tab, jnp.full((n_slots,), fill_value, I32), h2p)


def _experts_kernel(n_tab, t_tok, bs_ref, bn_ref, tab_hbm, x_hbm, wg_hbm, wu_hbm, wd_hbm, y_hbm,
                    xbuf, ybuf, wgf, wuf, wdf, wgb, wub, wdb, ids, gsem, ssem, isem, wsem):
    e = pl.program_id(0)
    n_exp = pl.num_programs(0)
    trash = TOP_K * t_tok
    k_shift = TOP_K.bit_length() - 1
    blk_rows = MOE_BLOCK * SUBLANES

    def weight_copies(ex):
        s = ex & 1
        return [pltpu.make_async_copy(src.at[ex], dst.at[s], wsem.at[s, i])
                for i, (src, dst) in enumerate(((wg_hbm, wgf), (wu_hbm, wuf), (wd_hbm, wdf)))]

    def ids_copy(g):
        q = g & (IDS_RING - 1)
        row = jnp.minimum(g, n_tab - 1)
        return pltpu.make_async_copy(tab_hbm.at[pl.ds(row, 1)], ids.at[pl.ds(q, 1)], isem.at[q])

    def gather_starts(g):
        s = g & (BUF_RING - 1)

        def one():
            row = pl.multiple_of(jnp.minimum(g, n_tab - 1) * blk_rows, blk_rows)
            pltpu.make_async_copy(x_hbm.at[pl.ds(row, blk_rows)], xbuf.at[s], gsem.at[s]).start()
        return [one]

    def scatter_starts(g):
        s, q = g & (BUF_RING - 1), g & (IDS_RING - 1)

        def one(r):
            a = ids[q, r]
            dst = jnp.where(a >= trash, trash + s * MOE_BLOCK + r,
                            (a & (TOP_K - 1)) * t_tok + jnp.right_shift(a, k_shift))
            out = y_hbm.at[pl.ds(pl.multiple_of(dst * SUBLANES, SUBLANES), SUBLANES)]
            pltpu.make_async_copy(ybuf.at[s, pl.ds(r * SUBLANES, SUBLANES)], out, ssem.at[s]).start()
        return [functools.partial(one, r) for r in range(MOE_BLOCK)]

    def wait_gather(s):
        pltpu.make_async_copy(x_hbm.at[pl.ds(0, blk_rows)], xbuf.at[s], gsem.at[s]).wait()

    def wait_scatter(s):
        pltpu.make_async_copy(ybuf.at[s], y_hbm.at[pl.ds(0, blk_rows)], ssem.at[s]).wait()

    @pl.when(e == 0)
    def _():
        for c in weight_copies(e):
            c.start(priority=WEIGHT_DMA_PRIORITY)
        for g0 in range(GATHER_AHEAD + 1):
            ids_copy(g0).start()
        for g0 in range(GATHER_AHEAD):
            ids_copy(g0).wait()
            for start in gather_starts(g0):
                start()
        q_fake = IDS_RING - 1
        for r in range(MOE_BLOCK):
            ids[q_fake, r] = trash
        ybuf[BUF_RING - 1] = jnp.zeros((blk_rows, LANES), U32)

    @pl.when(e + 1 < n_exp)
    def _():
        for c in weight_copies(e + 1):
            c.start(priority=WEIGHT_DMA_PRIORITY)

    for c in weight_copies(e):
        c.wait()

    @pl.when(bn_ref[e] > 0)
    def _():
        s = e & 1
        wgb[...] = wgf[s].astype(BF16)
        wub[...] = wuf[s].astype(BF16)
        wdb[...] = wdf[s].astype(BF16)

    d_exp = wgb.shape[1]
    n_gate = d_exp // N_HALF
    n_pair = wdb.shape[1] // (2 * N_HALF)
    n_phase = 2 * n_gate + 1

    def block(j, carry):
        g = bs_ref[e] + j
        s = g & (BUF_RING - 1)

        @pl.when(g >= BUF_RING - 1)
        def _():
            wait_scatter(s)

        ids_copy(g + GATHER_AHEAD).wait()
        wait_gather(s)

        starts = gather_starts(g + GATHER_AHEAD) + scatter_starts(g - 1)
        bounds = [len(starts) * p // n_phase for p in range(n_phase + 1)]
        phase = iter(range(n_phase))

        def issue():
            p = next(phase)
            for f in starts[bounds[p]:bounds[p + 1]]:
                f()

        x = _load_packed(xbuf.at[s], MOE_BLOCK).astype(BF16)
        ids_copy(g + GATHER_AHEAD + 1).start()
        acts = []
        for c in range(n_gate):
            cols = slice(c * N_HALF, (c + 1) * N_HALF)
            issue()
            gate = _bdot(x, wgb[:, cols])
            issue()
            up = _bdot(x, wub[:, cols])
            acts.append((gate * jax.nn.sigmoid(gate) * up).astype(BF16))
        act = jnp.concatenate(acts, axis=1)
        half = n_pair * N_HALF
        issue()
        for c in range(n_pair):
            y_lo = _bdot(act, wdb[:, c * N_HALF:(c + 1) * N_HALF])
            y_hi = _bdot(act, wdb[:, half + c * N_HALF:half + (c + 1) * N_HALF])
            _store_packed_cols(y_lo, y_hi, ybuf.at[s], c * (N_HALF // LANES), MOE_BLOCK)
        return carry

    lax.fori_loop(0, bn_ref[e], block, 0)

    @pl.when(e == n_exp - 1)
    def _():
        n_used = bs_ref[e] + bn_ref[e]
        for a in range(GATHER_AHEAD):
            wait_gather((n_used + a) & (BUF_RING - 1))
        ids_copy(n_used + GATHER_AHEAD).wait()
        for start in scatter_starts(n_used - 1):
            start()
        for s in range(BUF_RING):
            wait_scatter(s)
        ybuf[...] = jnp.zeros(ybuf.shape, U32)
        for s in range(BUF_RING):
            tail = pltpu.make_async_copy(
                ybuf.at[s], y_hbm.at[pl.ds((trash + s * MOE_BLOCK) * SUBLANES, blk_rows)], ssem.at[s])
            tail.start()
            tail.wait()


def _experts(blk_start, blk_count, table, x_sorted, t, wg, wu, wd):
    n_tab = table.shape[0]
    ne, d, de = wg.shape
    blk_rows = MOE_BLOCK * SUBLANES
    grid_spec = pltpu.PrefetchScalarGridSpec(
        num_scalar_prefetch=2,
        grid=(ne,),
        in_specs=[pl.BlockSpec(memory_space=pl.ANY)] * 5,
        out_specs=pl.BlockSpec(memory_space=pl.ANY),
        scratch_shapes=[pltpu.VMEM((BUF_RING, blk_rows, LANES), U32), pltpu.VMEM((BUF_RING, blk_rows, LANES), U32),
                        pltpu.VMEM((2, d, de), F32), pltpu.VMEM((2, d, de), F32), pltpu.VMEM((2, de, d), F32),
                        pltpu.VMEM((d, de), BF16), pltpu.VMEM((d, de), BF16), pltpu.VMEM((de, d), BF16),
                        pltpu.SMEM((IDS_RING, MOE_BLOCK), I32),
                        pltpu.SemaphoreType.DMA((BUF_RING,)), pltpu.SemaphoreType.DMA((BUF_RING,)),
                        pltpu.SemaphoreType.DMA((IDS_RING,)), pltpu.SemaphoreType.DMA((2, 3))],
    )
    return pl.pallas_call(
        functools.partial(_experts_kernel, n_tab, t),
        grid_spec=grid_spec,
        out_shape=jax.ShapeDtypeStruct(((TOP_K * t + TRASH_ROWS) * SUBLANES, LANES), U32),
        compiler_params=_cparams("arbitrary"),
        name="experts",
    )(blk_start, blk_count, table, x_sorted, wg, wu, wd)


def _combine_kernel(n_ptiles, *refs):
    y_refs = refs[:TOP_K]
    (wt_ref, h2_ref, x1_ref, wgs_ref, wus_ref, wds_ref, gf_ref, gtp_ref, gts_ref,
     yp_ref, ysm_ref) = refs[TOP_K:]
    i = pl.program_id(0)

    tm = h2_ref.shape[0]
    h = h2_ref[...]
    g = _bdot(h, wgs_ref[...])
    u = _bdot(h, wus_ref[...])
    f = _bdot((g * jax.nn.sigmoid(g) * u).astype(BF16), wds_ref[...])

    wt = wt_ref[...]
    routed = wt[:, 0:1] * _load_packed(y_refs[0], tm)
    for k in range(1, TOP_K):
        routed = routed + wt[:, k:k + 1] * _load_packed(y_refs[k], tm)
    f = routed + f

    @pl.when(i < n_ptiles)
    def _():
        yp_ref[...] = _rms(x1_ref[...] + gtp_ref[...] * f) * gf_ref[...]

    @pl.when(i == n_ptiles)
    def _():
        ysm_ref[...] = _rms(x1_ref[...] + gts_ref[...] * f) * gf_ref[...]


def _combine(y_planes, wts, h2, x1, wgs, wus, wds, gf, gtp, gts, n_prompt, seq, tm=128):
    t, d = h2.shape
    ts = t - n_prompt
    n_tiles = t // tm
    npt = n_prompt // tm
    per = seq // tm
    row = lambda i: (i, 0)
    full = lambda i: (0, 0)
    return pl.pallas_call(
        functools.partial(_combine_kernel, npt),
        grid=(n_tiles,),
        in_specs=[pl.BlockSpec((tm * SUBLANES, LANES), lambda i, k=k: (k * n_tiles + i, 0))
                  for k in range(TOP_K)] + [
                  pl.BlockSpec((tm, LANES), row), pl.BlockSpec((tm, d), row), pl.BlockSpec((tm, d), row),
                  _const_spec(wgs.shape), _const_spec(wus.shape), _const_spec(wds.shape),
                  pl.BlockSpec((1, d), full),
                  pl.BlockSpec((None, 1, d), lambda i: (jnp.minimum(i, npt - 1) // per, 0, 0)),
                  pl.BlockSpec((ts, d), full)],
        out_specs=[pl.BlockSpec((tm, d), lambda i: (jnp.minimum(i, npt - 1), 0)),
                   pl.BlockSpec((ts, d), full)],
        out_shape=[jax.ShapeDtypeStruct((n_prompt, d), F32), jax.ShapeDtypeStruct((ts, d), F32)],
        compiler_params=_cparams("arbitrary"),
        name="combine",
    )(*([y_planes] * TOP_K), wts, h2, x1, wgs, wus, wds, gf.reshape(1, d), gtp, gts)


def kernel(x_prompt, x_sample, state_conv, c_prompt, c_sample, w_ada, b_ada, g_norm1, w_in, a_norm_g, a_norm_b, w_s, b_s, conv_w, conv_b, b_norm_g, b_norm_b, w_a_out, w_b_out, w_o, g_norm2, w_router, router_bias, w_gate_e, w_up_e, w_down_e, w_gate_s, w_up_s, w_down_s, g_final):
    depth = w_ada.shape[0]
    assert depth == 1, "single-layer trunk"
    nb, seq, d = x_prompt.shape
    ts, dec_seq, _ = x_sample.shape
    assert dec_seq == 1
    tp = nb * seq
    t = tp + ts
    l = 0

    xp = x_prompt.reshape(tp, d)
    xs = x_sample.reshape(ts, d)

    pad = 16
    n_mod = w_ada.shape[2] // d
    c_all = jnp.concatenate([c_sample, c_prompt, jnp.zeros((pad - nb, d), F32)], axis=0)
    mod_s, mod_p = _ada(c_all, ts, w_ada[l], b_ada[l], n_mod)
    mp = [mod_p[k, :nb].reshape(nb, 1, d) for k in range(n_mod)]
    ms = [mod_s[k] for k in range(n_mod)]

    w_in16 = w_in[l].astype(BF16)
    h, u = _hmod(xp, xs, g_norm1[l], mp[1], mp[0], ms[1], ms[0], w_in16, seq)
    blk = lambda s: _col_block(w_in16, d, s)
    vec = lambda a: a.reshape(1, d)
    v32, v16 = _rows_call(_seg_v_kernel, h, ts, 1024, [blk(1)], [vec(a_norm_g[l]), vec(a_norm_b[l])],
                          [F32, BF16], "seg_v")
    (glu,) = _rows_call(_seg_glu_kernel, h, ts, 1024, [blk(2), blk(3)], [], [F32], "seg_glu")
    ga, gb = _rows_call(_seg_gate_kernel, h, ts, 1024, [blk(4), blk(5)], [], [BF16, BF16], "seg_gate")

    gdim = d // A_GROUPS
    cb, ng, nbb = vec(conv_b[l]), vec(b_norm_g[l]), vec(b_norm_b[l])
    yap, ybp = _mixp(u, v16, glu, w_s[l], b_s[l].T, conv_w[l], cb, ng, nbb, tp, seq)
    state_t = jnp.transpose(state_conv[l], (1, 0, 2))
    yas, ybs, conv_t = _mixs(u, v32, glu, state_t, vec(jnp.repeat(w_s[l][:, 0, 0], gdim)),
                             vec(jnp.repeat(b_s[l][:, 0], gdim)), conv_w[l], cb, ng, nbb, tp)
    conv_s = jnp.transpose(conv_t, (1, 0, 2))[None]

    wr = w_router[l]
    wrh = wr.astype(BF16)
    wrl = (wr - wrh.astype(F32)).astype(BF16)
    x1, h2, h2p, logits = _proj(yap, ybp, yas, ybs, ga, gb, xp, xs,
                                w_a_out[l].astype(BF16), w_b_out[l].astype(BF16), w_o[l].astype(BF16),
                                wrh, wrl, g_norm2[l], mp[2], mp[4], mp[3], ms[2], ms[4], ms[3], seq)

    n_blocks = -(-(t * TOP_K) // MOE_BLOCK) + N_EXPERTS
    idx, pos, wts, cnt = _route(logits, router_bias[l])
    slot, blk_start, blk_count = _slots(cnt, idx, pos)
    slot_tab = slot[:, :TOP_K].reshape(t * TOP_K)
    inv, x_sorted = _dispatch(slot_tab, h2p, n_blocks * MOE_BLOCK, t * TOP_K)
    table = inv.reshape(n_blocks, MOE_BLOCK)
    y_planes = _experts(blk_start[0], blk_count[0], table, x_sorted, t, w_gate_e[l], w_up_e[l], w_down_e[l])
    yp, ysm = _combine(y_planes, wts, h2, x1,
                       w_gate_s[l].astype(BF16), w_up_s[l].astype(BF16), w_down_s[l].astype(BF16),
                       g_final, mp[5], ms[5], tp, seq)

    n_past = state_conv.shape[2]
    last_rows = lambda a, n: jnp.stack([a[(b + 1) * seq - n:(b + 1) * seq] for b in range(nb)])[None]
    conv_p = last_rows(glu, n_past)
    chunkv_p = last_rows(v32, CHUNK)
    chunkv_s = v32[tp:].reshape(1, ts, 1, d)
    return (yp.reshape(nb, seq, d), ysm.reshape(ts, 1, d), conv_p, conv_s, chunkv_p, chunkv_s)
```

```python
import functools

import jax
import jax.numpy as jnp
from jax import lax
from jax.experimental import pallas as pl
from jax.experimental.pallas import tpu as pltpu

F32 = jnp.float32
BF16 = jnp.bfloat16
I32 = jnp.int32

EPS = 1e-6
CHUNK = 128
A_GROUPS = 8
CONV_W = 31
N_EXPERTS = 256
TOP_K = 8
N_EXPERT_GROUPS = 8
GROUP_SIZE = N_EXPERTS // N_EXPERT_GROUPS
TOPK_GROUPS = 4
ROUTED_SCALE = 2.5
MOE_BLOCK = 128
LANES = 128
SUBLANES = 8
VMEM_LIMIT = 58 * 1024 * 1024

NEG_INF = float("-inf")


def _cparams(*sem):
    return pltpu.CompilerParams(dimension_semantics=sem, vmem_limit_bytes=VMEM_LIMIT)


def _const_spec(shape):
    nd = len(shape)
    return pl.BlockSpec(shape, lambda *_: (0,) * nd, pipeline_mode=pl.Buffered(1))


def _rms(x):
    return x * lax.rsqrt(jnp.mean(x * x, axis=-1, keepdims=True) + EPS)


def _layer_norm(x, g, b):
    mu = jnp.mean(x, axis=-1, keepdims=True)
    xc = x - mu
    var = jnp.mean(xc * xc, axis=-1, keepdims=True)
    return xc * lax.rsqrt(var + EPS) * g + b


def _bdot(a, b):
    return jnp.dot(a, b, preferred_element_type=F32)


U32 = jnp.uint32
HI_MASK = 0xFFFF0000


def _store_packed_cols(x_lo, x_hi, ref, c0, n_rows):
    for i in range(x_lo.shape[1] // LANES):
        lo = x_lo[:, i * LANES:(i + 1) * LANES].astype(BF16).astype(F32)
        hi = x_hi[:, i * LANES:(i + 1) * LANES].astype(BF16).astype(F32)
        word = (lax.bitcast_convert_type(hi, U32) & U32(HI_MASK)) | (lax.bitcast_convert_type(lo, U32) >> 16)
        ref[pl.ds(c0 + i, n_rows, stride=SUBLANES), :] = word


def _store_packed(x, ref, n_rows):
    half = x.shape[1] // 2
    assert half == SUBLANES * LANES
    _store_packed_cols(x[:, :half], x[:, half:], ref, 0, n_rows)


def _load_packed(ref, n_rows):
    lo, hi = [], []
    for c in range(SUBLANES):
        word = ref[pl.ds(c, n_rows, stride=SUBLANES), :]
        lo.append(lax.bitcast_convert_type(word << 16, F32))
        hi.append(lax.bitcast_convert_type(word & U32(HI_MASK), F32))
    return jnp.concatenate(lo + hi, axis=1)


def _lane_sum(x):
    return jnp.sum(x, axis=-1, keepdims=True)


def _lane_max(x):
    return jnp.max(x, axis=-1, keepdims=True)


def _lane_min(x):
    return jnp.min(x, axis=-1, keepdims=True)


def _ada_kernel(n_a, c_ref, w_ref, b_ref, oa_ref, ob_ref):
    c = c_ref[...]
    s = (c * jax.nn.sigmoid(c)).astype(BF16)
    m = _bdot(s, w_ref[...].astype(BF16)) + b_ref[...]
    oa_ref[...] = m[0:n_a, :]
    ob_ref[...] = m[n_a:, :]


def _ada(c_all, n_a, w_ada, b_ada, n_mod, tn=1024):
    m, d = c_all.shape
    n = w_ada.shape[1]
    per = d // tn
    out = lambda rows: pl.BlockSpec((None, rows, tn), lambda j: (j // per, 0, j % per))
    return pl.pallas_call(
        functools.partial(_ada_kernel, n_a),
        grid=(n // tn,),
        in_specs=[pl.BlockSpec((m, d), lambda j: (0, 0)),
                  pl.BlockSpec((d, tn), lambda j: (0, j)),
                  pl.BlockSpec((1, tn), lambda j: (0, j))],
        out_specs=[out(n_a), out(m - n_a)],
        out_shape=[jax.ShapeDtypeStruct((n_mod, n_a, d), F32), jax.ShapeDtypeStruct((n_mod, m - n_a, d), F32)],
        compiler_params=_cparams("arbitrary"),
        name="ada",
    )(c_all, w_ada, b_ada.reshape(1, n))


def _hmod_kernel(n_ptiles, n_s, xp_ref, xs_ref, g_ref, scp_ref, shp_ref, scs_ref, shs_ref, wu_ref,
                 o_ref, u_ref):
    i = pl.program_id(0)

    def body(rows, x, sc, sh):
        h = ((_rms(x) * g_ref[...]) * (1.0 + sc) + sh).astype(o_ref.dtype)
        o_ref[0:rows, :] = h
        u_ref[0:rows, :] = jax.nn.gelu(_bdot(h, wu_ref[...])).astype(u_ref.dtype)

    @pl.when(i < n_ptiles)
    def _():
        body(o_ref.shape[0], xp_ref[...], scp_ref[...], shp_ref[...])

    @pl.when(i == n_ptiles)
    def _():
        body(n_s, xs_ref[...], scs_ref[...], shs_ref[...])


def _hmod(xp, xs, g, scp, shp, scs, shs, w_in16, seq, tm=1024):
    tp, d = xp.shape
    ts = xs.shape[0]
    npt = tp // tm
    per = seq // tm
    last = npt - 1
    pmod = pl.BlockSpec((None, 1, d), lambda i: (jnp.minimum(i, last) // per, 0, 0))
    full = lambda i: (0, 0)
    return pl.pallas_call(
        functools.partial(_hmod_kernel, npt, ts),
        grid=(npt + 1,),
        in_specs=[pl.BlockSpec((tm, d), lambda i: (jnp.minimum(i, last), 0)),
                  pl.BlockSpec((ts, d), full),
                  pl.BlockSpec((1, d), full),
                  pmod, pmod,
                  pl.BlockSpec((ts, d), full),
                  pl.BlockSpec((ts, d), full),
                  pl.BlockSpec((w_in16.shape[0], d), full, pipeline_mode=pl.Buffered(1))],
        out_specs=[pl.BlockSpec((tm, d), lambda i: (i, 0)), pl.BlockSpec((tm, d), lambda i: (i, 0))],
        out_shape=[jax.ShapeDtypeStruct((tp + ts, d), BF16), jax.ShapeDtypeStruct((tp + ts, d), BF16)],
        compiler_params=_cparams("arbitrary"),
        name="hmod",
    )(xp, xs, g.reshape(1, d), scp, shp, scs, shs, w_in16)


def _rows_call(kernel, h, n_tail, tm, weights, vecs, out_dtypes, name):
    t, d = h.shape
    n_full = (t - n_tail) // tm

    def wrapped(*refs):
        i = pl.program_id(0)

        @pl.when(i < n_full)
        def _():
            kernel(tm, *refs)

        @pl.when(i == n_full)
        def _():
            kernel(n_tail, *refs)

    in_specs = [pl.BlockSpec((tm, d), lambda i: (i, 0))]
    in_specs += [pl.BlockSpec(bs, im, pipeline_mode=pl.Buffered(1)) for (_, bs, im) in weights]
    in_specs += [pl.BlockSpec(v.shape, lambda i: (0, 0)) for v in vecs]
    return pl.pallas_call(
        wrapped,
        grid=(n_full + 1,),
        in_specs=in_specs,
        out_specs=[pl.BlockSpec((tm, d), lambda i: (i, 0)) for _ in out_dtypes],
        out_shape=[jax.ShapeDtypeStruct((t, d), dt) for dt in out_dtypes],
        compiler_params=_cparams("arbitrary"),
        name=name,
    )(h, *[w for (w, _, _) in weights], *vecs)


def _seg_v_kernel(rows, h_ref, w_ref, g_ref, b_ref, o32_ref, o16_ref):
    z = _bdot(h_ref[0:rows, :], w_ref[...])
    v = _layer_norm(jax.nn.gelu(z), g_ref[...], b_ref[...])
    o32_ref[0:rows, :] = v
    o16_ref[0:rows, :] = v.astype(o16_ref.dtype)


def _seg_glu_kernel(rows, h_ref, wa_ref, wb_ref, o_ref):
    h = h_ref[0:rows, :]
    o_ref[0:rows, :] = _bdot(h, wa_ref[...]) * jax.nn.sigmoid(_bdot(h, wb_ref[...]))


def _seg_gate_kernel(rows, h_ref, wa_ref, wb_ref, oa_ref, ob_ref):
    h = h_ref[0:rows, :]
    oa_ref[0:rows, :] = jax.nn.sigmoid(_bdot(h, wa_ref[...])).astype(oa_ref.dtype)
    ob_ref[0:rows, :] = jax.nn.sigmoid(_bdot(h, wb_ref[...])).astype(ob_ref.dtype)


def _col_block(w, d, seg):
    return (w, (w.shape[0], d), lambda i, seg=seg: (0, seg))


CONV_ROWS = 128
CONV_LANES = 128


def _mixp_kernel(tm, halo, per, u_ref, v_ref, glu_ref, prev_ref, ws_ref, bst_ref, cw_ref, cb_ref,
                 ng_ref, nb_ref, ya_ref, yb_ref, xp_ref, cv_ref):
    i = pl.program_id(0)
    d = u_ref.shape[1]
    n_chunks = tm // CHUNK
    gdim = d // A_GROUPS

    r_io = lax.broadcasted_iota(I32, (CHUNK, CHUNK), 0)
    c_io = lax.broadcasted_iota(I32, (CHUNK, CHUNK), 1)
    tril = c_io <= r_io
    for g in range(A_GROUPS):
        cs = slice(g * gdim, (g + 1) * gdim)
        wg = jnp.where(tril, ws_ref[g], 0.0).astype(BF16)
        vg = jnp.concatenate([v_ref[c * CHUNK:(c + 1) * CHUNK, cs] for c in range(n_chunks)], axis=1)
        mix = _bdot(wg, vg)
        bcol = bst_ref[:, g:g + 1]
        for c in range(n_chunks):
            rs = slice(c * CHUNK, (c + 1) * CHUNK)
            ya_ref[rs, cs] = (u_ref[rs, cs].astype(F32)
                              * (mix[:, c * gdim:(c + 1) * gdim] + bcol)).astype(ya_ref.dtype)

    xp_ref[0:halo, :] = jnp.where(i % per == 0, 0.0, prev_ref[...])
    xp_ref[halo:halo + tm, :] = glu_ref[...]
    first = halo - (CONV_W - 1)
    n_cb = d // CONV_LANES

    def chunk(idx, carry):
        r0 = pl.multiple_of((idx // n_cb) * CONV_ROWS, CONV_ROWS)
        c0 = pl.multiple_of((idx % n_cb) * CONV_LANES, CONV_LANES)
        acc = jnp.zeros((CONV_ROWS, CONV_LANES), F32)
        n_win = CONV_ROWS + halo
        window = xp_ref[pl.ds(r0, n_win), pl.ds(c0, CONV_LANES)]
        for j in range(SUBLANES):
            taps = [m for m in range(first, first + CONV_W) if m % SUBLANES == j]
            s_j = window if j == 0 else pltpu.roll(window, n_win - j, 0)
            for m in taps:
                w_row = cw_ref[pl.ds(m - first, 1), pl.ds(c0, CONV_LANES)]
                acc = acc + w_row * s_j[m - j:m - j + CONV_ROWS, :]
        cv_ref[pl.ds(r0, CONV_ROWS), pl.ds(c0, CONV_LANES)] = acc
        return carry

    lax.fori_loop(0, (tm // CONV_ROWS) * n_cb, chunk, 0)

    y = _layer_norm(cv_ref[...] + cb_ref[...], ng_ref[...], nb_ref[...])
    yb_ref[...] = (y * jax.nn.sigmoid(y)).astype(yb_ref.dtype)


def _mixp(u, v16, glu, w_s, b_s_t, conv_w, conv_b, ng, nb, n_prompt, seq, tm=512, halo=32):
    d = u.shape[1]
    npt = n_prompt // tm
    per = seq // tm
    hb = tm // halo
    row = lambda i: (i, 0)
    return pl.pallas_call(
        functools.partial(_mixp_kernel, tm, halo, per),
        grid=(npt,),
        in_specs=[pl.BlockSpec((tm, d), row),
                  pl.BlockSpec((tm, d), row),
                  pl.BlockSpec((tm, d), row),
                  pl.BlockSpec((halo, d), lambda i: (jnp.maximum(i * hb - 1, 0), 0)),
                  _const_spec(w_s.shape), _const_spec(b_s_t.shape), _const_spec(conv_w.shape),
                  _const_spec(conv_b.shape), _const_spec(ng.shape), _const_spec(nb.shape)],
        out_specs=[pl.BlockSpec((tm, d), row), pl.BlockSpec((tm, d), row)],
        out_shape=[jax.ShapeDtypeStruct((n_prompt, d), BF16), jax.ShapeDtypeStruct((n_prompt, d), BF16)],
        scratch_shapes=[pltpu.VMEM((halo + tm, d), F32), pltpu.VMEM((tm, d), F32)],
        compiler_params=_cparams("arbitrary"),
        name="mixp",
    )(u, v16, glu, glu, w_s, b_s_t, conv_w, conv_b, ng, nb)


def _mixs_kernel(u_ref, v_ref, glu_ref, st_ref, ws0_ref, bs0_ref, cw_ref, cb_ref, ng_ref, nb_ref,
                 ya_ref, yb_ref, st_out_ref):
    n_past = st_ref.shape[0]
    glu = glu_ref[...]
    ya_ref[...] = (u_ref[...].astype(F32) * (ws0_ref[...] * v_ref[...] + bs0_ref[...])).astype(ya_ref.dtype)
    acc = cw_ref[n_past:n_past + 1, :] * glu + cb_ref[...]
    for k in range(n_past):
        acc = acc + cw_ref[k:k + 1, :] * st_ref[k]
    y = _layer_norm(acc, ng_ref[...], nb_ref[...])
    yb_ref[...] = (y * jax.nn.sigmoid(y)).astype(yb_ref.dtype)
    for k in range(n_past - 1):
        st_out_ref[k] = st_ref[k + 1]
    st_out_ref[n_past - 1] = glu


def _mixs(u, v32, glu, state_t, ws0, bs0, conv_w, conv_b, ng, nb, n_prompt, tb=32):
    n_past, ts, d = state_t.shape
    off = n_prompt // tb
    tail = lambda j: (off + j, 0)
    vec = lambda j: (0, 0)
    st_spec = pl.BlockSpec((n_past, tb, d), lambda j: (0, j, 0))
    return pl.pallas_call(
        _mixs_kernel,
        grid=(ts // tb,),
        in_specs=[pl.BlockSpec((tb, d), tail), pl.BlockSpec((tb, d), tail), pl.BlockSpec((tb, d), tail),
                  st_spec,
                  pl.BlockSpec((1, d), vec), pl.BlockSpec((1, d), vec),
                  pl.BlockSpec(conv_w.shape, vec), pl.BlockSpec((1, d), vec),
                  pl.BlockSpec((1, d), vec), pl.BlockSpec((1, d), vec)],
        out_specs=[pl.BlockSpec((tb, d), lambda j: (j, 0)), pl.BlockSpec((tb, d), lambda j: (j, 0)), st_spec],
        out_shape=[jax.ShapeDtypeStruct((ts, d), BF16), jax.ShapeDtypeStruct((ts, d), BF16),
                   jax.ShapeDtypeStruct(state_t.shape, state_t.dtype)],
        compiler_params=_cparams("arbitrary"),
        name="mixs",
    )(u, v32, glu, state_t, ws0, bs0, conv_w, conv_b, ng, nb)


PROJ_ROWS = 128


def _proj_kernel(n_ptiles, tm, n_s,
                 yap_ref, ybp_ref, yas_ref, ybs_ref, ga_ref, gb_ref, xp_ref, xs_ref,
                 wa_ref, wb_ref, wo_ref, wrh_ref, wrl_ref, g2_ref,
                 gtp_ref, scp_ref, shp_ref, gts_ref, scs_ref, shs_ref,
                 x1_ref, h2_ref, h2p_ref, lg_ref):
    i = pl.program_id(0)

    def body(r0, rows, ya_ref, yb_ref, x_ref, gate, sc, sh):
        rs = slice(r0, r0 + rows)
        pa = _bdot(ya_ref[rs, :], wa_ref[...])
        pb = _bdot(yb_ref[rs, :], wb_ref[...])
        merged = ga_ref[rs, :].astype(F32) * pa + gb_ref[rs, :].astype(F32) * pb
        out = _bdot(merged.astype(BF16), wo_ref[...])
        x1 = x_ref[rs, :] + gate * out
        h2 = (_rms(x1) * g2_ref[...]) * (1.0 + sc) + sh
        hi = h2.astype(BF16)
        lo = (h2 - hi.astype(F32)).astype(BF16)
        logits = _bdot(hi, wrh_ref[...]) + (_bdot(hi, wrl_ref[...]) + _bdot(lo, wrh_ref[...]))
        x1_ref[rs, :] = x1
        h2_ref[rs, :] = hi
        _store_packed(h2, h2p_ref.at[pl.ds(r0 * SUBLANES, rows * SUBLANES)], rows)
        lg_ref[rs, :] = logits

    @pl.when(i < n_ptiles)
    def _():
        for r0 in range(0, tm, PROJ_ROWS):
            body(r0, PROJ_ROWS, yap_ref, ybp_ref, xp_ref, gtp_ref[...], scp_ref[...], shp_ref[...])

    @pl.when(i == n_ptiles)
    def _():
        body(0, n_s, yas_ref, ybs_ref, xs_ref, gts_ref[...], scs_ref[...], shs_ref[...])


def _proj(yap, ybp, yas, ybs, ga, gb, xp, xs, wa, wb, wo, wrh, wrl, g2,
          gtp, scp, shp, gts, scs, shs, seq, tm=256):
    tp, d = xp.shape
    ts = xs.shape[0]
    t = tp + ts
    ne = wrh.shape[1]
    npt = tp // tm
    per = seq // tm
    last = npt - 1
    prow = lambda i: (jnp.minimum(i, last), 0)
    row = lambda i: (i, 0)
    full = lambda i: (0, 0)
    pmod = pl.BlockSpec((None, 1, d), lambda i: (jnp.minimum(i, last) // per, 0, 0))
    smod = pl.BlockSpec((ts, d), full)
    return pl.pallas_call(
        functools.partial(_proj_kernel, npt, tm, ts),
        grid=(npt + 1,),
        in_specs=[pl.BlockSpec((tm, d), prow), pl.BlockSpec((tm, d), prow),
                  pl.BlockSpec((ts, d), full), pl.BlockSpec((ts, d), full),
                  pl.BlockSpec((tm, d), row), pl.BlockSpec((tm, d), row),
                  pl.BlockSpec((tm, d), prow), pl.BlockSpec((ts, d), full),
                  _const_spec(wa.shape), _const_spec(wb.shape), _const_spec(wo.shape),
                  _const_spec(wrh.shape), _const_spec(wrl.shape), _const_spec((1, d)),
                  pmod, pmod, pmod, smod, smod, smod],
        out_specs=[pl.BlockSpec((tm, d), row), pl.BlockSpec((tm, d), row),
                   pl.BlockSpec((tm * SUBLANES, LANES), row), pl.BlockSpec((tm, ne), row)],
        out_shape=[jax.ShapeDtypeStruct((t, d), F32), jax.ShapeDtypeStruct((t, d), BF16),
                   jax.ShapeDtypeStruct((t * SUBLANES, LANES), U32), jax.ShapeDtypeStruct((t, ne), F32)],
        compiler_params=_cparams("arbitrary"),
        name="proj",
    )(yap, ybp, yas, ybs, ga, gb, xp, xs, wa, wb, wo, wrh, wrl, g2.reshape(1, d),
      gtp, scp, shp, gts, scs, shs)


def _pack_cols(cols, lane):
    out = jnp.zeros(lane.shape, cols[0].dtype)
    for k, c in enumerate(cols):
        out = jnp.where(lane == k, c, out)
    return out


def _route_kernel(lg_ref, bias_ref, tri_ref, idx_ref, pos_ref, wt_ref, cnt_ref, carry_ref):
    i = pl.program_id(0)

    @pl.when(i == 0)
    def _():
        carry_ref[...] = jnp.zeros_like(carry_ref)

    scores = jax.nn.sigmoid(lg_ref[...])
    sel = scores + bias_ref[...]
    lane = lax.broadcasted_iota(I32, sel.shape, 1)
    lane_f = lane.astype(F32)
    grp = jnp.right_shift(lane, GROUP_SIZE.bit_length() - 1)

    gs_cols = []
    gs_full = jnp.zeros(sel.shape, F32)
    for g in range(N_EXPERT_GROUPS):
        in_g = grp == g
        xg = jnp.where(in_g, sel, NEG_INF)
        m1 = _lane_max(xg)
        n1 = _lane_sum(jnp.where(xg == m1, 1.0, 0.0))
        m2 = jnp.where(n1 >= 2.0, m1, _lane_max(jnp.where(xg < m1, xg, NEG_INF)))
        gs = m1 + m2
        gs_cols.append(gs)
        gs_full = jnp.where(in_g, gs, gs_full)

    beaten = jnp.zeros(sel.shape, F32)
    for g in range(N_EXPERT_GROUPS):
        better = (gs_cols[g] > gs_full) | ((gs_cols[g] == gs_full) & (g < grp))
        beaten = beaten + jnp.where(better, 1.0, 0.0)
    cur = jnp.where(beaten < float(TOPK_GROUPS), sel, NEG_INF)

    idx_cols, s_cols, hots = [], [], []
    picked = jnp.zeros(sel.shape, F32)
    for _ in range(TOP_K):
        m = _lane_max(cur)
        idx = _lane_min(jnp.where(cur == m, lane_f, float(N_EXPERTS)))
        hot = lane_f == idx
        idx_cols.append(idx)
        s_cols.append(_lane_sum(jnp.where(hot, scores, 0.0)))
        hots.append(hot)
        picked = jnp.where(hot, 1.0, picked)
        cur = jnp.where(hot, NEG_INF, cur)

    den = s_cols[0]
    for s in s_cols[1:]:
        den = den + s
    w_cols = [s / den * ROUTED_SCALE for s in s_cols]

    pos_full = _bdot(tri_ref[...], picked.astype(BF16)) + carry_ref[...]
    pos_cols = [_lane_sum(jnp.where(h, pos_full, 0.0)) for h in hots]
    carry_ref[...] = carry_ref[...] + jnp.sum(picked, axis=0, keepdims=True)
    cnt_ref[...] = carry_ref[...]

    lane_o = lax.broadcasted_iota(I32, idx_ref.shape, 1)
    idx_ref[...] = _pack_cols(idx_cols, lane_o).astype(I32)
    pos_ref[...] = _pack_cols(pos_cols, lane_o).astype(I32)
    wt_ref[...] = _pack_cols(w_cols, lane_o)


def _route(logits, bias, tm=640):
    t, ne = logits.shape
    tri = jnp.tril(jnp.ones((tm, tm), BF16), -1)
    row = lambda i: (i, 0)
    return pl.pallas_call(
        _route_kernel,
        grid=(t // tm,),
        in_specs=[pl.BlockSpec((tm, ne), row), pl.BlockSpec((1, ne), lambda i: (0, 0)),
                  pl.BlockSpec((tm, tm), lambda i: (0, 0))],
        out_specs=[pl.BlockSpec((tm, LANES), row), pl.BlockSpec((tm, LANES), row),
                   pl.BlockSpec((tm, LANES), row), pl.BlockSpec((1, ne), lambda i: (0, 0))],
        out_shape=[jax.ShapeDtypeStruct((t, LANES), I32), jax.ShapeDtypeStruct((t, LANES), I32),
                   jax.ShapeDtypeStruct((t, LANES), F32), jax.ShapeDtypeStruct((1, ne), F32)],
        scratch_shapes=[pltpu.VMEM((1, ne), F32)],
        compiler_params=_cparams("arbitrary"),
        name="route",
    )(logits, bias.reshape(1, ne), tri)


def _slots_kernel(cnt_ref, idx_ref, pos_ref, triu_ref, slot_ref, bs_ref, bn_ref, ps_ref):
    i = pl.program_id(0)
    ne = cnt_ref.shape[1]

    @pl.when(i == 0)
    def _():
        cnt = cnt_ref[...]
        nblk = jnp.floor((cnt + float(MOE_BLOCK - 1)) * (1.0 / MOE_BLOCK))
        pend = _bdot(jnp.broadcast_to(nblk, (SUBLANES, ne)).astype(BF16), triu_ref[...])[0:1, :]
        pstart = pend - nblk
        ps_ref[...] = pstart
        bs_ref[...] = pstart.astype(I32)
        bn_ref[...] = nblk.astype(I32)

    lane_o = lax.broadcasted_iota(I32, idx_ref.shape, 1)
    lane_e = lax.broadcasted_iota(I32, (idx_ref.shape[0], ne), 1)
    idx = idx_ref[...].astype(F32)
    pos = pos_ref[...].astype(F32)
    pstart = ps_ref[...]
    lane_ef = lane_e.astype(F32)
    cols = []
    for k in range(TOP_K):
        e_k = _lane_sum(jnp.where(lane_o == k, idx, 0.0))
        p_k = _lane_sum(jnp.where(lane_o == k, pos, 0.0))
        ps_k = _lane_sum(jnp.where(lane_ef == e_k, pstart, 0.0))
        cols.append(ps_k * float(MOE_BLOCK) + p_k)
    slot_ref[...] = _pack_cols(cols, lane_o).astype(I32)


def _slots(cnt, idx, pos, tm=640):
    t = idx.shape[0]
    ne = cnt.shape[1]
    triu = jnp.triu(jnp.ones((ne, ne), BF16))
    row = lambda i: (i, 0)
    full = lambda i: (0, 0)
    return pl.pallas_call(
        _slots_kernel,
        grid=(t // tm,),
        in_specs=[pl.BlockSpec((1, ne), full), pl.BlockSpec((tm, LANES), row),
                  pl.BlockSpec((tm, LANES), row), pl.BlockSpec((ne, ne), full)],
        out_specs=[pl.BlockSpec((tm, LANES), row), pl.BlockSpec((1, ne), full),
                   pl.BlockSpec((1, ne), full)],
        out_shape=[jax.ShapeDtypeStruct((t, LANES), I32), jax.ShapeDtypeStruct((1, ne), I32),
                   jax.ShapeDtypeStruct((1, ne), I32)],
        scratch_shapes=[pltpu.VMEM((1, ne), F32)],
        compiler_params=_cparams("arbitrary"),
        name="slots",
    )(cnt, idx, pos, triu)


BUF_RING = 4
IDS_RING = 8
GATHER_AHEAD = 2
TRASH_ROWS = BUF_RING * MOE_BLOCK
WEIGHT_DMA_PRIORITY = 1
N_HALF = 256


def _dispatch_kernel(n_tiles, tm, tab_ref, fill_hbm, h_hbm, wgs_ref, wus_ref, wds_ref, inv_ref, x_hbm,
                     fs_ref, buf, rsem, wsem):
    i = pl.program_id(0)
    rows = tm * SUBLANES
    n_buf = buf.shape[0]

    @pl.when(i == 0)
    def _():
        pltpu.sync_copy(fill_hbm, inv_ref)

    def read(j):
        s = lax.rem(j, n_buf)
        return pltpu.make_async_copy(h_hbm.at[pl.ds(pl.multiple_of(j * rows, rows), rows)], buf.at[s],
                                     rsem.at[s])

    def wait_writes(s):
        for _ in range(TOP_K):
            pltpu.make_async_copy(buf.at[s], x_hbm.at[pl.ds(0, rows)], wsem.at[s]).wait()

    @pl.when(i == 0)
    def _():
        read(i).start()

    @pl.when(i >= n_buf - 1)
    def _():
        wait_writes(lax.rem(i + 1, n_buf))

    @pl.when(i + 1 < n_tiles)
    def _():
        read(i + 1).start()

    read(i).wait()
    s = lax.rem(i, n_buf)

    h = _load_packed(buf.at[s], tm).astype(BF16)
    g = _bdot(h, wgs_ref[...])
    u = _bdot(h, wus_ref[...])
    fs_ref[...] = _bdot((g * jax.nn.sigmoid(g) * u).astype(BF16), wds_ref[...]).astype(fs_ref.dtype)

    first_id = i * (tm * TOP_K)

    def row(r, carry):
        src = buf.at[s, pl.ds(pl.multiple_of(r * SUBLANES, SUBLANES), SUBLANES)]
        for k in range(TOP_K):
            slot = tab_ref[0, 0, r * TOP_K + k]
            inv_ref[slot] = first_id + r * TOP_K + k
            dst = pl.multiple_of(slot * SUBLANES, SUBLANES)
            pltpu.make_async_copy(src, x_hbm.at[pl.ds(dst, SUBLANES)], wsem.at[s]).start(priority=k % 2)
        return carry
    lax.fori_loop(0, tm, row, 0, unroll=4)

    @pl.when(i == n_tiles - 1)
    def _():
        for j in range(max(n_tiles - (n_buf - 1), 0), n_tiles):
            wait_writes(j % n_buf)


def _dispatch(slot_tab, h2p, n_slots, fill_value, wgs, wus, wds, tm=128):
    t = h2p.shape[0] // SUBLANES
    d = wds.shape[1]
    n_tiles = t // tm
    tab = slot_tab.reshape(n_tiles, 1, tm * TOP_K)
    return pl.pallas_call(
        functools.partial(_dispatch_kernel, n_tiles, tm),
        grid=(n_tiles,),
        in_specs=[pl.BlockSpec((1, 1, tm * TOP_K), lambda i: (i, 0, 0), memory_space=pltpu.SMEM),
                  pl.BlockSpec(memory_space=pl.ANY), pl.BlockSpec(memory_space=pl.ANY),
                  _const_spec(wgs.shape), _const_spec(wus.shape), _const_spec(wds.shape)],
        out_specs=[pl.BlockSpec((n_slots,), lambda i: (0,), memory_space=pltpu.SMEM),
                   pl.BlockSpec(memory_space=pl.ANY),
                   pl.BlockSpec((tm, d), lambda i: (i, 0))],
        out_shape=[jax.ShapeDtypeStruct((n_slots,), I32),
                   jax.ShapeDtypeStruct((n_slots * SUBLANES, LANES), U32),
                   jax.ShapeDtypeStruct((t, d), BF16)],
        scratch_shapes=[pltpu.VMEM((3, tm * SUBLANES, LANES), U32), pltpu.SemaphoreType.DMA((3,)),
                        pltpu.SemaphoreType.DMA((3,))],
        compiler_params=_cparams("arbitrary"),
        name="dispatch",
    )(tab, jnp.full((n_slots,), fill_value, I32), h2p, wgs, wus, wds)


def _experts_kernel(n_tab, t_tok, bs_ref, bn_ref, tab_hbm, x_hbm, wg_hbm, wu_hbm, wd_hbm, y_hbm,
                    xbuf, ybuf, wgf, wuf, wdf, wgb, wub, wdb, ids, gsem, ssem, isem, wsem):
    e = pl.program_id(0)
    n_exp = pl.num_programs(0)
    trash = TOP_K * t_tok
    k_shift = TOP_K.bit_length() - 1
    blk_rows = MOE_BLOCK * SUBLANES

    def weight_copies(ex):
        s = ex & 1
        return [pltpu.make_async_copy(src.at[ex], dst.at[s], wsem.at[s, i])
                for i, (src, dst) in enumerate(((wg_hbm, wgf), (wu_hbm, wuf), (wd_hbm, wdf)))]

    def ids_copy(g):
        q = g & (IDS_RING - 1)
        row = jnp.minimum(g, n_tab - 1)
        return pltpu.make_async_copy(tab_hbm.at[pl.ds(row, 1)], ids.at[pl.ds(q, 1)], isem.at[q])

    def gather_starts(g):
        s = g & (BUF_RING - 1)

        def one():
            row = pl.multiple_of(jnp.minimum(g, n_tab - 1) * blk_rows, blk_rows)
            pltpu.make_async_copy(x_hbm.at[pl.ds(row, blk_rows)], xbuf.at[s], gsem.at[s]).start()
        return [one]

    def scatter_starts(g):
        s, q = g & (BUF_RING - 1), g & (IDS_RING - 1)

        def one(r):
            a = ids[q, r]
            dst = jnp.where(a >= trash, trash + s * MOE_BLOCK + r,
                            (a & (TOP_K - 1)) * t_tok + jnp.right_shift(a, k_shift))
            out = y_hbm.at[pl.ds(pl.multiple_of(dst * SUBLANES, SUBLANES), SUBLANES)]
            pltpu.make_async_copy(ybuf.at[s, pl.ds(r * SUBLANES, SUBLANES)], out, ssem.at[s]).start()
        return [functools.partial(one, r) for r in range(MOE_BLOCK)]

    def wait_gather(s):
        pltpu.make_async_copy(x_hbm.at[pl.ds(0, blk_rows)], xbuf.at[s], gsem.at[s]).wait()

    def wait_scatter(s):
        pltpu.make_async_copy(ybuf.at[s], y_hbm.at[pl.ds(0, blk_rows)], ssem.at[s]).wait()

    @pl.when(e == 0)
    def _():
        for c in weight_copies(e):
            c.start(priority=WEIGHT_DMA_PRIORITY)
        for g0 in range(GATHER_AHEAD + 1):
            ids_copy(g0).start()
        for g0 in range(GATHER_AHEAD):
            ids_copy(g0).wait()
            for start in gather_starts(g0):
                start()
        q_fake = IDS_RING - 1
        for r in range(MOE_BLOCK):
            ids[q_fake, r] = trash
        ybuf[BUF_RING - 1] = jnp.zeros((blk_rows, LANES), U32)

    @pl.when(e + 1 < n_exp)
    def _():
        for c in weight_copies(e + 1):
            c.start(priority=WEIGHT_DMA_PRIORITY)

    for c in weight_copies(e):
        c.wait()

    @pl.when(bn_ref[e] > 0)
    def _():
        s = e & 1
        wgb[...] = wgf[s].astype(BF16)
        wub[...] = wuf[s].astype(BF16)
        wdb[...] = wdf[s].astype(BF16)

    d_exp = wgb.shape[1]
    n_gate = d_exp // N_HALF
    n_pair = wdb.shape[1] // (2 * N_HALF)
    n_phase = 2 * n_gate + 1

    def block(j, carry):
        g = bs_ref[e] + j
        s = g & (BUF_RING - 1)

        @pl.when(g >= BUF_RING - 1)
        def _():
            wait_scatter(s)

        ids_copy(g + GATHER_AHEAD).wait()
        wait_gather(s)

        starts = gather_starts(g + GATHER_AHEAD) + scatter_starts(g - 1)
        bounds = [len(starts) * p // n_phase for p in range(n_phase + 1)]
        phase = iter(range(n_phase))

        def issue():
            p = next(phase)
            for f in starts[bounds[p]:bounds[p + 1]]:
                f()

        x = _load_packed(xbuf.at[s], MOE_BLOCK).astype(BF16)
        ids_copy(g + GATHER_AHEAD + 1).start()
        acts = []
        for c in range(n_gate):
            cols = slice(c * N_HALF, (c + 1) * N_HALF)
            issue()
            gate = _bdot(x, wgb[:, cols])
            issue()
            up = _bdot(x, wub[:, cols])
            acts.append((gate * jax.nn.sigmoid(gate) * up).astype(BF16))
        act = jnp.concatenate(acts, axis=1)
        half = n_pair * N_HALF
        issue()
        for c in range(n_pair):
            y_lo = _bdot(act, wdb[:, c * N_HALF:(c + 1) * N_HALF])
            y_hi = _bdot(act, wdb[:, half + c * N_HALF:half + (c + 1) * N_HALF])
            _store_packed_cols(y_lo, y_hi, ybuf.at[s], c * (N_HALF // LANES), MOE_BLOCK)
        return carry

    lax.fori_loop(0, bn_ref[e], block, 0)

    @pl.when(e == n_exp - 1)
    def _():
        n_used = bs_ref[e] + bn_ref[e]
        for a in range(GATHER_AHEAD):
            wait_gather((n_used + a) & (BUF_RING - 1))
        ids_copy(n_used + GATHER_AHEAD).wait()
        for start in scatter_starts(n_used - 1):
            start()
        for s in range(BUF_RING):
            wait_scatter(s)
        ybuf[...] = jnp.zeros(ybuf.shape, U32)
        for s in range(BUF_RING):
            tail = pltpu.make_async_copy(
                ybuf.at[s], y_hbm.at[pl.ds((trash + s * MOE_BLOCK) * SUBLANES, blk_rows)], ssem.at[s])
            tail.start()
            tail.wait()


def _experts(blk_start, blk_count, table, x_sorted, t, wg, wu, wd):
    n_tab = table.shape[0]
    ne, d, de = wg.shape
    blk_rows = MOE_BLOCK * SUBLANES
    grid_spec = pltpu.PrefetchScalarGridSpec(
        num_scalar_prefetch=2,
        grid=(ne,),
        in_specs=[pl.BlockSpec(memory_space=pl.ANY)] * 5,
        out_specs=pl.BlockSpec(memory_space=pl.ANY),
        scratch_shapes=[pltpu.VMEM((BUF_RING, blk_rows, LANES), U32), pltpu.VMEM((BUF_RING, blk_rows, LANES), U32),
                        pltpu.VMEM((2, d, de), F32), pltpu.VMEM((2, d, de), F32), pltpu.VMEM((2, de, d), F32),
                        pltpu.VMEM((d, de), BF16), pltpu.VMEM((d, de), BF16), pltpu.VMEM((de, d), BF16),
                        pltpu.SMEM((IDS_RING, MOE_BLOCK), I32),
                        pltpu.SemaphoreType.DMA((BUF_RING,)), pltpu.SemaphoreType.DMA((BUF_RING,)),
                        pltpu.SemaphoreType.DMA((IDS_RING,)), pltpu.SemaphoreType.DMA((2, 3))],
    )
    return pl.pallas_call(
        functools.partial(_experts_kernel, n_tab, t),
        grid_spec=grid_spec,
        out_shape=jax.ShapeDtypeStruct(((TOP_K * t + TRASH_ROWS) * SUBLANES, LANES), U32),
        compiler_params=_cparams("arbitrary"),
        name="experts",
    )(blk_start, blk_count, table, x_sorted, wg, wu, wd)


def _combine_kernel(n_ptiles, *refs):
    y_refs = refs[:TOP_K]
    (wt_ref, fs_ref, x1_ref, gf_ref, gtp_ref, gts_ref, yp_ref, ysm_ref) = refs[TOP_K:]
    i = pl.program_id(0)

    tm = fs_ref.shape[0]
    f = fs_ref[...].astype(F32)

    wt = wt_ref[...]
    routed = wt[:, 0:1] * _load_packed(y_refs[0], tm)
    for k in range(1, TOP_K):
        routed = routed + wt[:, k:k + 1] * _load_packed(y_refs[k], tm)
    f = routed + f

    @pl.when(i < n_ptiles)
    def _():
        yp_ref[...] = _rms(x1_ref[...] + gtp_ref[...] * f) * gf_ref[...]

    @pl.when(i == n_ptiles)
    def _():
        ysm_ref[...] = _rms(x1_ref[...] + gts_ref[...] * f) * gf_ref[...]


def _combine(y_planes, wts, h2, x1, gf, gtp, gts, n_prompt, seq, tm=128):
    t, d = h2.shape
    ts = t - n_prompt
    n_tiles = t // tm
    npt = n_prompt // tm
    per = seq // tm
    row = lambda i: (i, 0)
    full = lambda i: (0, 0)
    return pl.pallas_call(
        functools.partial(_combine_kernel, npt),
        grid=(n_tiles,),
        in_specs=[pl.BlockSpec((tm * SUBLANES, LANES), lambda i, k=k: (k * n_tiles + i, 0))
                  for k in range(TOP_K)] + [
                  pl.BlockSpec((tm, LANES), row), pl.BlockSpec((tm, d), row), pl.BlockSpec((tm, d), row),
                  pl.BlockSpec((1, d), full),
                  pl.BlockSpec((None, 1, d), lambda i: (jnp.minimum(i, npt - 1) // per, 0, 0)),
                  pl.BlockSpec((ts, d), full)],
        out_specs=[pl.BlockSpec((tm, d), lambda i: (jnp.minimum(i, npt - 1), 0)),
                   pl.BlockSpec((ts, d), full)],
        out_shape=[jax.ShapeDtypeStruct((n_prompt, d), F32), jax.ShapeDtypeStruct((ts, d), F32)],
        compiler_params=_cparams("arbitrary"),
        name="combine",
    )(*([y_planes] * TOP_K), wts, h2, x1, gf.reshape(1, d), gtp, gts)


def kernel(x_prompt, x_sample, state_conv, c_prompt, c_sample, w_ada, b_ada, g_norm1, w_in, a_norm_g, a_norm_b, w_s, b_s, conv_w, conv_b, b_norm_g, b_norm_b, w_a_out, w_b_out, w_o, g_norm2, w_router, router_bias, w_gate_e, w_up_e, w_down_e, w_gate_s, w_up_s, w_down_s, g_final):
    depth = w_ada.shape[0]
    assert depth == 1, "single-layer trunk"
    nb, seq, d = x_prompt.shape
    ts, dec_seq, _ = x_sample.shape
    assert dec_seq == 1
    tp = nb * seq
    t = tp + ts
    l = 0

    xp = x_prompt.reshape(tp, d)
    xs = x_sample.reshape(ts, d)

    pad = 16
    n_mod = w_ada.shape[2] // d
    c_all = jnp.concatenate([c_sample, c_prompt, jnp.zeros((pad - nb, d), F32)], axis=0)
    mod_s, mod_p = _ada(c_all, ts, w_ada[l], b_ada[l], n_mod)
    mp = [mod_p[k, :nb].reshape(nb, 1, d) for k in range(n_mod)]
    ms = [mod_s[k] for k in range(n_mod)]

    w_in16 = w_in[l].astype(BF16)
    h, u = _hmod(xp, xs, g_norm1[l], mp[1], mp[0], ms[1], ms[0], w_in16, seq)
    blk = lambda s: _col_block(w_in16, d, s)
    vec = lambda a: a.reshape(1, d)
    v32, v16 = _rows_call(_seg_v_kernel, h, ts, 1024, [blk(1)], [vec(a_norm_g[l]), vec(a_norm_b[l])],
                          [F32, BF16], "seg_v")
    (glu,) = _rows_call(_seg_glu_kernel, h, ts, 1024, [blk(2), blk(3)], [], [F32], "seg_glu")
    ga, gb = _rows_call(_seg_gate_kernel, h, ts, 1024, [blk(4), blk(5)], [], [BF16, BF16], "seg_gate")

    gdim = d // A_GROUPS
    cb, ng, nbb = vec(conv_b[l]), vec(b_norm_g[l]), vec(b_norm_b[l])
    yap, ybp = _mixp(u, v16, glu, w_s[l], b_s[l].T, conv_w[l], cb, ng, nbb, tp, seq)
    state_t = jnp.transpose(state_conv[l], (1, 0, 2))
    yas, ybs, conv_t = _mixs(u, v32, glu, state_t, vec(jnp.repeat(w_s[l][:, 0, 0], gdim)),
                             vec(jnp.repeat(b_s[l][:, 0], gdim)), conv_w[l], cb, ng, nbb, tp)
    conv_s = jnp.transpose(conv_t, (1, 0, 2))[None]

    wr = w_router[l]
    wrh = wr.astype(BF16)
    wrl = (wr - wrh.astype(F32)).astype(BF16)
    x1, h2, h2p, logits = _proj(yap, ybp, yas, ybs, ga, gb, xp, xs,
                                w_a_out[l].astype(BF16), w_b_out[l].astype(BF16), w_o[l].astype(BF16),
                                wrh, wrl, g_norm2[l], mp[2], mp[4], mp[3], ms[2], ms[4], ms[3], seq)

    n_blocks = -(-(t * TOP_K) // MOE_BLOCK) + N_EXPERTS
    idx, pos, wts, cnt = _route(logits, router_bias[l])
    slot, blk_start, blk_count = _slots(cnt, idx, pos)
    slot_tab = slot[:, :TOP_K].reshape(t * TOP_K)
    inv, x_sorted, shared = _dispatch(slot_tab, h2p, n_blocks * MOE_BLOCK, t * TOP_K, w_gate_s[l].astype(BF16),
                                      w_up_s[l].astype(BF16), w_down_s[l].astype(BF16))
    table = inv.reshape(n_blocks, MOE_BLOCK)
    y_planes = _experts(blk_start[0], blk_count[0], table, x_sorted, t, w_gate_e[l], w_up_e[l], w_down_e[l])
    yp, ysm = _combine(y_planes, wts, shared, x1, g_final, mp[5], ms[5], tp, seq)

    n_past = state_conv.shape[2]
    last_rows = lambda a, n: jnp.stack([a[(b + 1) * seq - n:(b + 1) * seq] for b in range(nb)])[None]
    conv_p = last_rows(glu, n_past)
    chunkv_p = last_rows(v32, CHUNK)
    chunkv_s = v32[tp:].reshape(1, ts, 1, d)
    return (yp.reshape(nb, seq, d), ysm.reshape(ts, 1, d), conv_p, conv_s, chunkv_p, chunkv_s)
```
